```python
import math
import jax, jax.numpy as jnp
from jax import lax
import numpy as np

D_MODEL = 2048
BATCH = 1
SEQ = 8192
DEPTH = 2

MIX_WIDTH = D_MODEL
GW = MIX_WIDTH // 4
GDN_HEAD_DIM = 128
GDN_HEADS = GW // GDN_HEAD_DIM
GDN_CONV = 4
GDN_CHUNK = 64
SSM_HEAD_DIM = 64
SSM_HEADS = GW // SSM_HEAD_DIM
SSM_GROUPS = 2
SSM_STATE = 128
SSM_CONV = 4
SSM_CHUNK = 128
SSM_CONV_DIM = GW + 2 * SSM_GROUPS * SSM_STATE
ATTN_HEAD_DIM = 64
ATTN_Q_HEADS = GW // ATTN_HEAD_DIM
ATTN_KV_HEADS = ATTN_Q_HEADS // 4
WINDOW = 128
ROPE_THETA = 10000.0
SC_CONV = 3
SC_GROUPS = 4
MLP_HIDDEN = 4 * D_MODEL
EPS = 1e-6

IN_WIDTH = (4 * GW + 2 * GDN_HEADS) + (GW + SSM_CONV_DIM + SSM_HEADS) + (GW + 2 * ATTN_KV_HEADS * ATTN_HEAD_DIM) + 3 * GW

kernel_name = 'hymba_style_hybrid_gdn_ssd_swa_shortconv'


def in_proj_split_points():
    sizes = (GW, GW, GW, GW, GDN_HEADS, GDN_HEADS,
             GW, SSM_CONV_DIM, SSM_HEADS,
             GW, ATTN_KV_HEADS * ATTN_HEAD_DIM, ATTN_KV_HEADS * ATTN_HEAD_DIM,
             GW, GW, GW)
    return [int(s) for s in np.cumsum(sizes)[:-1]]


def rms_norm(x, w):
    xf = x.astype(jnp.float32)
    y = xf * lax.rsqrt(jnp.mean(xf * xf, axis=-1, keepdims=True) + EPS)
    return (y * w.astype(jnp.float32)).astype(x.dtype)


def l2_norm(x):
    return x * lax.rsqrt(jnp.sum(x * x, axis=-1, keepdims=True) + EPS)


def causal_dwconv(x, w):
    K, C = w.shape
    return lax.conv_general_dilated(x, w[:, None, :].astype(x.dtype), window_strides=(1,),
                                    padding=[(K - 1, 0)], dimension_numbers=('NWC', 'WIO', 'NWC'),
                                    feature_group_count=C)


def rope_tables(positions):
    inv_freq = ROPE_THETA ** (-jnp.arange(0, ATTN_HEAD_DIM, 2, dtype=jnp.float32) / ATTN_HEAD_DIM)
    ang = positions.astype(jnp.float32)[..., None] * inv_freq
    return jnp.cos(ang), jnp.sin(ang)


def apply_rope(x, cos, sin):
    x1, x2 = jnp.split(x, 2, axis=-1)
    return jnp.concatenate([x1 * cos - x2 * sin, x2 * cos + x1 * sin], axis=-1).astype(x.dtype)


def gated_deltanet(q, k, v, z, b, a, conv_w, a_log, dt_bias, norm_w):
    Bsz, L, _ = q.shape
    H, dk, Cs = GDN_HEADS, GDN_HEAD_DIM, GDN_CHUNK
    Nc = L // Cs
    f32 = jnp.float32
    qkv = jax.nn.silu(causal_dwconv(jnp.concatenate([q, k, v], axis=-1), conv_w))
    q, k, v = [t.reshape(Bsz, L, H, dk).astype(f32) for t in jnp.split(qkv, 3, axis=-1)]
    q = l2_norm(q) * (dk ** -0.5)
    k = l2_norm(k)
    beta = jax.nn.sigmoid(b.astype(f32))
    g = -jnp.exp(a_log.astype(f32)) * jax.nn.softplus(a.astype(f32) + dt_bias.astype(f32))

    def to_chunks(t):
        return t.reshape(Bsz, Nc, Cs, H, -1).transpose(0, 3, 1, 2, 4)

    q, k, v = to_chunks(q), to_chunks(k), to_chunks(v)
    beta = to_chunks(beta[..., None])[..., 0]
    gc = jnp.cumsum(to_chunks(g[..., None])[..., 0], axis=-1)
    tri_incl = jnp.tril(jnp.ones((Cs, Cs), dtype=bool))
    tri_strict = jnp.tril(jnp.ones((Cs, Cs), dtype=bool), -1)
    decay = jnp.exp(jnp.where(tri_incl, gc[..., :, None] - gc[..., None, :], -jnp.inf))
    k_beta = k * beta[..., None]
    m = jnp.where(tri_strict, jnp.einsum('bhncd,bhnsd->bhncs', k_beta, k) * decay, 0.0)
    eye = jnp.eye(Cs, dtype=f32)
    t_inv = lax.linalg.triangular_solve(eye + m, jnp.broadcast_to(eye, m.shape), left_side=True,
                                        lower=True, unit_diagonal=True)
    u = t_inv @ (v * beta[..., None])
    w = t_inv @ (k_beta * jnp.exp(gc)[..., None])
    attn = jnp.where(tri_incl, jnp.einsum('bhncd,bhnsd->bhncs', q, k) * decay, 0.0)
    q_dec = q * jnp.exp(gc)[..., None]
    k_dec = k * jnp.exp(gc[..., -1:] - gc)[..., None]
    chunk_dec = jnp.exp(gc[..., -1])

    def step(S, inp):
        u_c, w_c, qd_c, kd_c, at_c, cd_c = inp
        v_new = u_c - w_c @ S
        o = qd_c @ S + at_c @ v_new
        S = S * cd_c[..., None, None] + jnp.swapaxes(kd_c, -1, -2) @ v_new
        return S, o

    xs = tuple(jnp.moveaxis(t, 2, 0) for t in (u, w, q_dec, k_dec, attn, chunk_dec))
    _, o = lax.scan(step, jnp.zeros((Bsz, H, dk, dk), f32), xs)
    o = o.transpose(1, 0, 3, 2, 4).reshape(Bsz, L, H, dk)
    zf = z.reshape(Bsz, L, H, dk).astype(f32)
    o = rms_norm(o, norm_w) * jax.nn.silu(zf)
    return o.reshape(Bsz, L, GW).astype(z.dtype)


def mamba2_ssd(z, xbc, dt, conv_w, conv_b, a_log, dt_bias, d_skip, norm_w):
    Bsz, L, _ = z.shape
    G, R, P, N, Cs = SSM_GROUPS, SSM_HEADS // SSM_GROUPS, SSM_HEAD_DIM, SSM_STATE, SSM_CHUNK
    Nc = L // Cs
    f32 = jnp.float32
    xbc = jax.nn.silu(causal_dwconv(xbc, conv_w) + conv_b)
    xs, bm, cm = jnp.split(xbc.astype(f32), [GW, GW + G * N], axis=-1)
    xs = xs.reshape(Bsz, Nc, Cs, G, R, P)
    bm = bm.reshape(Bsz, Nc, Cs, G, N)
    cm = cm.reshape(Bsz, Nc, Cs, G, N)
    dt = jax.nn.softplus(dt.astype(f32) + dt_bias.astype(f32)).reshape(Bsz, Nc, Cs, G, R)
    a = -jnp.exp(a_log.astype(f32)).reshape(G, R)
    da_cs = jnp.cumsum((dt * a).transpose(0, 3, 4, 1, 2), axis=-1)
    tri = jnp.tril(jnp.ones((Cs, Cs), dtype=bool))
    lmat = jnp.exp(jnp.where(tri, da_cs[..., :, None] - da_cs[..., None, :], -jnp.inf))
    xdt = xs * dt[..., None]
    cb = jnp.einsum('bclgn,bcsgn->bgcls', cm, bm)
    y_diag = jnp.einsum('bgcls,bgrcls,bcsgrp->bclgrp', cb, lmat, xdt)
    decay_states = jnp.exp(da_cs[..., -1:] - da_cs)
    states = jnp.einsum('bclgn,bgrcl,bclgrp->bcgrpn', bm, decay_states, xdt)
    chunk_dec = jnp.exp(da_cs[..., -1])

    def step(S, inp):
        st, dec = inp
        return S * dec[..., None, None] + st, S

    _, prev = lax.scan(step, jnp.zeros((Bsz, G, R, P, N), f32),
                       (jnp.moveaxis(states, 1, 0), jnp.moveaxis(chunk_dec, 3, 0)))
    prev = jnp.moveaxis(prev, 0, 1)
    y_off = jnp.einsum('bclgn,bcgrpn,bgrcl->bclgrp', cm, prev, jnp.exp(da_cs))
    y = y_diag + y_off + xs * d_skip.astype(f32).reshape(G, R)[:, :, None]
    y = y.reshape(Bsz, L, GW) * jax.nn.silu(z.astype(f32))
    y = rms_norm(y.reshape(Bsz, L, G, GW // G), norm_w.reshape(G, GW // G))
    return y.reshape(Bsz, L, GW).astype(z.dtype)


def band_blocks(t):
    Bsz, L = t.shape[:2]
    cur = t.reshape(Bsz, L // WINDOW, WINDOW, *t.shape[2:])
    prev = jnp.pad(cur[:, :-1], ((0, 0), (1, 0), (0, 0), (0, 0), (0, 0)))
    return jnp.concatenate([prev, cur], axis=2)


def sliding_window_attention(q, k, v, sinks, cos, sin):
    Bsz, L, _ = q.shape
    Hkv, R, hd, W = ATTN_KV_HEADS, ATTN_Q_HEADS // ATTN_KV_HEADS, ATTN_HEAD_DIM, WINDOW
    nb = L // W
    q = apply_rope(q.reshape(Bsz, L, Hkv, R, hd), cos[:, :, None, None], sin[:, :, None, None])
    k = apply_rope(k.reshape(Bsz, L, Hkv, hd), cos[:, :, None], sin[:, :, None])
    v = v.reshape(Bsz, L, Hkv, hd)
    qb = q.reshape(Bsz, nb, W, Hkv, R, hd)
    kb, vb = band_blocks(k), band_blocks(v)
    s = jnp.einsum('bnqgrd,bnkgd->bngrqk', qb, kb, preferred_element_type=jnp.float32) * (hd ** -0.5)
    qi = jnp.arange(W)[:, None]
    kj = jnp.arange(2 * W)[None, :]
    rel = qi + W - kj
    band = (rel >= 0) & (rel < W)
    mask = band[None] & ((jnp.arange(nb)[:, None, None] > 0) | (kj >= W)[None])
    s = jnp.where(mask[None, :, None, None], s, -jnp.inf)
    sink = sinks.astype(jnp.float32).reshape(Hkv, R)[None, None, :, :, None, None]
    mx = jnp.maximum(jnp.max(s, axis=-1, keepdims=True), sink)
    p = jnp.exp(s - mx)
    p = p / (jnp.sum(p, axis=-1, keepdims=True) + jnp.exp(sink - mx))
    o = jnp.einsum('bngrqk,bnkgd->bnqgrd', p.astype(vb.dtype), vb)
    return o.reshape(Bsz, L, Hkv * R * hd)


def short_conv(b, c, h, conv_w):
    return b * causal_dwconv(c * h, conv_w)


def hybrid_mixer(h, cos, sin, w_in, w_out, gdn_conv_w, gdn_a_log, gdn_dt_bias, gdn_norm_w,
                 ssm_conv_w, ssm_conv_b, ssm_a_log, ssm_dt_bias, ssm_d, ssm_norm_w, attn_sinks, sc_conv_w):
    proj = h @ w_in
    (gq, gk, gv, gz, gb, ga, sz, sxbc, sdt, aq, ak, av, cb, cc, ch) = jnp.split(proj, in_proj_split_points(), axis=-1)
    y_a = gated_deltanet(gq, gk, gv, gz, gb, ga, gdn_conv_w, gdn_a_log, gdn_dt_bias, gdn_norm_w)
    y_b = mamba2_ssd(sz, sxbc, sdt, ssm_conv_w, ssm_conv_b, ssm_a_log, ssm_dt_bias, ssm_d, ssm_norm_w)
    y_c = sliding_window_attention(aq, ak, av, attn_sinks, cos, sin)
    y_d = short_conv(cb, cc, ch, sc_conv_w)
    return jnp.concatenate([y_a, y_b, y_c, y_d], axis=-1) @ w_out


def setup_inputs(seed: int = 0) -> dict:
    key = jax.random.key(seed)
    ks = jax.random.split(key, 26)
    f32 = jnp.float32

    def nrm(k, shape, s):
        return jax.random.normal(k, shape, f32) * s

    def gain(k, shape):
        return 1.0 + 0.05 * jax.random.normal(k, shape, f32)

    def dt_bias(k, shape):
        dt = jnp.exp(jax.random.uniform(k, shape, f32, math.log(1e-3), math.log(1e-1)))
        return dt + jnp.log(-jnp.expm1(-dt))

    def a_log(k, shape):
        return jnp.log(jax.random.uniform(k, shape, f32, 1.0, 16.0))

    return {
        'x': nrm(ks[0], (BATCH, SEQ, D_MODEL), 1.0),
        'c': nrm(ks[1], (BATCH, D_MODEL), 1.0),
        'positions': jnp.broadcast_to(jnp.arange(SEQ, dtype=jnp.int32), (BATCH, SEQ)),
        'ada_w': nrm(ks[2], (DEPTH, D_MODEL, 6 * D_MODEL), 0.5 * D_MODEL ** -0.5),
        'ada_b': nrm(ks[3], (DEPTH, 6 * D_MODEL), 0.02),
        'norm_pre_mix': gain(ks[4], (DEPTH, D_MODEL)),
        'norm_post_mix': gain(ks[5], (DEPTH, D_MODEL)),
        'norm_pre_mlp': gain(ks[6], (DEPTH, D_MODEL)),
        'norm_post_mlp': gain(ks[7], (DEPTH, D_MODEL)),
        'w_in': nrm(ks[8], (DEPTH, D_MODEL, IN_WIDTH), D_MODEL ** -0.5),
        'w_out': nrm(ks[9], (DEPTH, MIX_WIDTH, D_MODEL), MIX_WIDTH ** -0.5),
        'gdn_conv_w': nrm(ks[10], (DEPTH, GDN_CONV, 3 * GW), GDN_CONV ** -0.5),
        'gdn_a_log': a_log(ks[11], (DEPTH, GDN_HEADS)),
        'gdn_dt_bias': dt_bias(ks[12], (DEPTH, GDN_HEADS)),
        'gdn_norm_w': gain(ks[13], (DEPTH, GDN_HEAD_DIM)),
        'ssm_conv_w': nrm(ks[14], (DEPTH, SSM_CONV, SSM_CONV_DIM), SSM_CONV ** -0.5),
        'ssm_conv_b': nrm(ks[15], (DEPTH, SSM_CONV_DIM), 0.02),
        'ssm_a_log': a_log(ks[16], (DEPTH, SSM_HEADS)),
        'ssm_dt_bias': dt_bias(ks[17], (DEPTH, SSM_HEADS)),
        'ssm_d': gain(ks[18], (DEPTH, SSM_HEADS)),
        'ssm_norm_w': gain(ks[19], (DEPTH, GW)),
        'attn_sinks': nrm(ks[20], (DEPTH, ATTN_Q_HEADS), 0.5),
        'sc_conv_w': nrm(ks[21], (DEPTH, SC_CONV, GW), SC_CONV ** -0.5),
        'w_up': nrm(ks[22], (DEPTH, D_MODEL, MLP_HIDDEN), D_MODEL ** -0.5),
        'w_down': nrm(ks[23], (DEPTH, MLP_HIDDEN, D_MODEL), MLP_HIDDEN ** -0.5),
    }


def reference(x, c, positions, ada_w, ada_b, norm_pre_mix, norm_post_mix, norm_pre_mlp, norm_post_mlp,
              w_in, w_out, gdn_conv_w, gdn_a_log, gdn_dt_bias, gdn_norm_w, ssm_conv_w, ssm_conv_b,
              ssm_a_log, ssm_dt_bias, ssm_d, ssm_norm_w, attn_sinks, sc_conv_w, w_up, w_down):
    cos, sin = rope_tables(positions)
    c_act = jax.nn.silu(c)
    for i in range(DEPTH):
        mod = (c_act @ ada_w[i] + ada_b[i])[:, None, :]
        shift_a, scale_a, gate_a, shift_m, scale_m, gate_m = jnp.split(mod, 6, axis=-1)
        h = rms_norm(x, norm_pre_mix[i]) * (1.0 + scale_a) + shift_a
        y = hybrid_mixer(h, cos, sin, w_in[i], w_out[i], gdn_conv_w[i], gdn_a_log[i], gdn_dt_bias[i],
                         gdn_norm_w[i], ssm_conv_w[i], ssm_conv_b[i], ssm_a_log[i], ssm_dt_bias[i],
                         ssm_d[i], ssm_norm_w[i], attn_sinks[i], sc_conv_w[i])
        x = x + gate_a * rms_norm(y, norm_post_mix[i])
        h = rms_norm(x, norm_pre_mlp[i]) * (1.0 + scale_m) + shift_m
        y = jnp.square(jax.nn.relu(h @ w_up[i])) @ w_down[i]
        x = x + gate_m * rms_norm(y, norm_post_mlp[i])
    return x
```

```python
import functools

import jax
import jax.numpy as jnp
from jax import lax
from jax.experimental import pallas as pl
from jax.experimental.pallas import tpu as pltpu

F32 = jnp.float32
BF16 = jnp.bfloat16

D_MODEL = 2048
GW = 512
GDN_HEADS, GDN_HEAD_DIM, GDN_CHUNK = 4, 128, 64
SSM_HEADS, SSM_HEAD_DIM, SSM_STATE, SSM_CHUNK = 8, 64, 128, 128
ATTN_HEAD_DIM, ATTN_Q_HEADS, WINDOW = 64, 8, 128
ROPE_THETA = 10000.0
MLP_HIDDEN = 4 * D_MODEL
EPS = 1e-6

SUBLANES = 8
LANES = 128

IN_COLS = 11 * GW + 4 * LANES
SLAB_GQ, SLAB_GK, SLAB_GV, SLAB_GZ, SLAB_SZ, SLAB_SX, SLAB_SBC, SLAB_AQ, SLAB_CB, SLAB_CC, SLAB_CH = range(11)
SLAB128_AK, SLAB128_AV, SLAB128_SMALL = 44, 45, 46
LANE_BETA, LANE_ALPHA, LANE_DT = 0, 4, 8

VMEM_LIMIT = 56 * 1024 * 1024


def _cparams(*sem):
    return pltpu.CompilerParams(dimension_semantics=sem, vmem_limit_bytes=VMEM_LIMIT)


def _sigmoid(x):
    return jax.nn.sigmoid(x)


def _silu(x):
    return x * jax.nn.sigmoid(x)


def _mm(a, b):
    return jnp.dot(a.astype(BF16), b.astype(BF16), preferred_element_type=F32)


def _mm_nt(a, b):
    return lax.dot_general(a.astype(BF16), b.astype(BF16), (((1,), (1,)), ((), ())),
                           preferred_element_type=F32)


def _mm_tn(a, b):
    return lax.dot_general(a.astype(BF16), b.astype(BF16), (((0,), (0,)), ((), ())),
                           preferred_element_type=F32)


def _blk(idx, size):
    assert size & (size - 1) == 0
    return lax.shift_right_logical(idx, size.bit_length() - 1)


def _split3(x):
    x1 = x.astype(BF16)
    r1 = x - x1.astype(F32)
    x2 = r1.astype(BF16)
    r2 = r1 - x2.astype(F32)
    return x1, x2, r2.astype(BF16)


def _chunk_cumsum(g, chunk):
    rows = g.shape[0]
    r = lax.broadcasted_iota(jnp.int32, (2 * rows, rows), 0)
    s = lax.broadcasted_iota(jnp.int32, (2 * rows, rows), 1)
    rr = jnp.where(r >= rows, r - rows, r)
    same = _blk(rr, chunk) == _blk(s, chunk)
    sel = same & ((r >= rows) | (s <= rr))
    mat = jnp.where(sel, 1.0, 0.0).astype(BF16)
    acc = None
    for part in _split3(g):
        t = jnp.dot(mat, part, preferred_element_type=F32)
        acc = t if acc is None else acc + t
    return acc[:rows], acc[rows:]


def _causal_conv(x, prev, w, taps):
    rows = x.shape[0]
    xp = jnp.concatenate([prev, x], axis=0)
    acc = x * w[taps - 1:taps]
    for d in range(1, taps):
        acc = acc + xp[SUBLANES - d:SUBLANES - d + rows] * w[taps - 1 - d:taps - d]
    return acc


MOD_TN = 1024
MOD_ROWS = 64


def _mod_kernel(c_ref, w_ref, b_ref, o_ref):
    d = c_ref.shape[0]

    def body(r, acc):
        rows = pl.ds(pl.multiple_of(r * MOD_ROWS, MOD_ROWS), MOD_ROWS)
        cc = _silu(c_ref[rows, :])
        prod = w_ref[0, rows, :] * cc
        return acc + jnp.sum(prod.reshape(MOD_ROWS // SUBLANES, SUBLANES, MOD_TN), axis=0)

    acc = lax.fori_loop(0, d // MOD_ROWS, body, jnp.zeros((SUBLANES, MOD_TN), F32))
    o_ref[0] = jnp.sum(acc, axis=0, keepdims=True) + b_ref[0]


def _modulation(c_col, ada_w, ada_b):
    depth, d, n = ada_w.shape
    out = pl.pallas_call(
        _mod_kernel,
        grid=(depth, n // MOD_TN),
        in_specs=[
            pl.BlockSpec((d, 1), lambda i, j: (0, 0)),
            pl.BlockSpec((1, d, MOD_TN), lambda i, j: (i, 0, j)),
            pl.BlockSpec((1, 1, MOD_TN), lambda i, j: (i, 0, j)),
        ],
        out_specs=pl.BlockSpec((1, 1, MOD_TN), lambda i, j: (i, 0, j)),
        out_shape=jax.ShapeDtypeStruct((depth, 1, n), F32),
        compiler_params=_cparams("parallel", "parallel"),
        name="adaln_mod",
    )(c_col, ada_w, ada_b.reshape(depth, 1, n))
    return out


ROPE_TB = 1024


def _rope_kernel(pos_ref, invf_ref, sign_ref, cos_ref, sin_ref):
    ang = pos_ref[...].astype(F32) * invf_ref[...]
    cos_ref[...] = jnp.cos(ang)
    sin_ref[...] = jnp.sin(ang) * sign_ref[...]


def _rope_tables(pos_col):
    seq = pos_col.shape[0]
    half = ATTN_HEAD_DIM // 2
    inv_freq = ROPE_THETA ** (-jnp.arange(0, ATTN_HEAD_DIM, 2, dtype=F32) / ATTN_HEAD_DIM)
    invf = jnp.tile(inv_freq, LANES // half).reshape(1, LANES)
    lane = jnp.arange(LANES)
    sign = jnp.where((lane % ATTN_HEAD_DIM) < half, -1.0, 1.0).astype(F32).reshape(1, LANES)
    tb = min(ROPE_TB, seq)
    return pl.pallas_call(
        _rope_kernel,
        grid=(seq // tb,),
        in_specs=[
            pl.BlockSpec((tb, 1), lambda i: (i, 0)),
            pl.BlockSpec((1, LANES), lambda i: (0, 0)),
            pl.BlockSpec((1, LANES), lambda i: (0, 0)),
        ],
        out_specs=[pl.BlockSpec((tb, LANES), lambda i: (i, 0))] * 2,
        out_shape=[jax.ShapeDtypeStruct((seq, LANES), F32)] * 2,
        compiler_params=_cparams("parallel"),
        name="rope_tables",
    )(pos_col, invf, sign)


NORM_ROWS = 128


def _prenorm_to(x_ref, nw_ref, sc_ref, sh_ref, h_ref):
    tm = x_ref.shape[0]

    def body(r, carry):
        rows = pl.ds(pl.multiple_of(r * NORM_ROWS, NORM_ROWS), NORM_ROWS)
        x = x_ref[rows, :]
        y = x * lax.rsqrt(jnp.mean(x * x, axis=-1, keepdims=True) + EPS) * nw_ref[...]
        h_ref[rows, :] = (y * (1.0 + sc_ref[...]) + sh_ref[...]).astype(BF16)
        return carry

    lax.fori_loop(0, tm // NORM_ROWS, body, 0)


INPROJ_TM = 1024
INPROJ_TN = 512


def _inproj_kernel(x_ref, nw_ref, sc_ref, sh_ref, w_ref, o_ref, h_ref):
    @pl.when(pl.program_id(1) == 0)
    def _():
        _prenorm_to(x_ref, nw_ref, sc_ref, sh_ref, h_ref)

    o_ref[...] = jnp.dot(h_ref[...], w_ref[...], preferred_element_type=F32)


def _in_projection(x, norm_w, scale, shift, w_bf16):
    seq, d = x.shape
    n = w_bf16.shape[1]
    tm = min(INPROJ_TM, seq)
    row = lambda i, j: (0, 0)
    return pl.pallas_call(
        _inproj_kernel,
        grid=(seq // tm, n // INPROJ_TN),
        in_specs=[
            pl.BlockSpec((tm, d), lambda i, j: (i, 0)),
            pl.BlockSpec((1, d), row),
            pl.BlockSpec((1, d), row),
            pl.BlockSpec((1, d), row),
            pl.BlockSpec((d, INPROJ_TN), lambda i, j: (0, j)),
        ],
        out_specs=pl.BlockSpec((tm, INPROJ_TN), lambda i, j: (i, j)),
        out_shape=jax.ShapeDtypeStruct((seq, n), F32),
        scratch_shapes=[pltpu.VMEM((tm, d), BF16)],
        compiler_params=_cparams("parallel", "arbitrary"),
        name="in_proj",
    )(x, norm_w, scale, shift, w_bf16)


GDN_TB = 256


def _l2norm(x):
    return x * lax.rsqrt(jnp.sum(x * x, axis=-1, keepdims=True) + EPS)


def _unit_lower_inverse_minus_identity(m, r, s):
    same8 = _blk(r, 8) == _blk(s, 8)
    same16 = _blk(r, 16) == _blk(s, 16)
    same32 = _blk(r, 32) == _blk(s, 32)
    n1 = jnp.where(same8, -m, 0.0)
    n2 = _mm(n1, n1)
    y = n1 + n2 + _mm(n1, n2)
    n4 = _mm(n2, n2)
    y = y + n4 + _mm(y, n4)
    for blk in (jnp.where(same16 & ~same8, m, 0.0),
                jnp.where(same32 & ~same16, m, 0.0),
                jnp.where(~same32, m, 0.0)):
        c = blk + _mm(y, blk)
        y = y - c - _mm(c, y)
    return y


def _gdn_kernel(q_ref, k_ref, v_ref, z_ref, qp_ref, kp_ref, vp_ref, sm_ref, cw_ref, lp_ref, nw_ref,
                o_ref, state_ref):
    step = pl.program_id(0)
    tb = q_ref.shape[0]
    hd = GDN_HEAD_DIM
    nchunks = tb // GDN_CHUNK

    @pl.when(step == 0)
    def _():
        state_ref[...] = jnp.zeros_like(state_ref)

    keep_prev = jnp.where(step == 0, 0.0, 1.0)
    cw = cw_ref[...]

    def conv_silu(x_ref, p_ref, idx):
        w = cw[:, idx * GW:(idx + 1) * GW]
        return _silu(_causal_conv(x_ref[...], p_ref[...] * keep_prev, w, 4))

    q = conv_silu(q_ref, qp_ref, 0)
    k = conv_silu(k_ref, kp_ref, 1)
    v = conv_silu(v_ref, vp_ref, 2)
    z = z_ref[...]

    small = sm_ref[...]
    a_log = lp_ref[0:1, :]
    dt_bias = lp_ref[1:2, :]
    beta_all = _sigmoid(small)
    g_all = -jnp.exp(a_log) * jax.nn.softplus(small + dt_bias)
    gc_all, gtot_all = _chunk_cumsum(g_all, GDN_CHUNK)
    gc_t = gc_all.T

    r = lax.broadcasted_iota(jnp.int32, (tb, tb), 0)
    s = lax.broadcasted_iota(jnp.int32, (tb, tb), 1)
    same_chunk = _blk(r, GDN_CHUNK) == _blk(s, GDN_CHUNK)
    mask_incl = same_chunk & (s <= r)
    mask_strict = same_chunk & (s < r)

    for h in range(GDN_HEADS):
        cols = slice(h * hd, (h + 1) * hd)
        qh = _l2norm(q[:, cols]) * (hd ** -0.5)
        kh = _l2norm(k[:, cols])
        vh = v[:, cols]
        beta = beta_all[:, LANE_BETA + h:LANE_BETA + h + 1]
        gcol = gc_all[:, LANE_ALPHA + h:LANE_ALPHA + h + 1]
        grow = gc_t[LANE_ALPHA + h:LANE_ALPHA + h + 1, :]
        gtot = gtot_all[:, LANE_ALPHA + h:LANE_ALPHA + h + 1]

        decay = jnp.exp(jnp.where(mask_incl, gcol - grow, -jnp.inf))
        k_beta = kh * beta
        m = jnp.where(mask_strict, _mm_nt(k_beta, kh) * decay, 0.0)
        attn = _mm_nt(qh, kh) * decay
        y = _unit_lower_inverse_minus_identity(m, r, s)

        e_gc = jnp.exp(gcol)
        rhs = jnp.concatenate([vh * beta, k_beta * e_gc], axis=1)
        uw = rhs + _mm(y, rhs)
        u, w = uw[:, :hd], uw[:, hd:]
        q_dec = qh * e_gc
        k_dec = kh * jnp.exp(gtot - gcol)
        chunk_dec = jnp.exp(gtot)

        state = state_ref[h]
        v_new, o_inter = [], []
        for c in range(nchunks):
            rows = slice(c * GDN_CHUNK, (c + 1) * GDN_CHUNK)
            vn = u[rows] - _mm(w[rows], state)
            o_inter.append(_mm(q_dec[rows], state))
            state = state * chunk_dec[c * GDN_CHUNK:c * GDN_CHUNK + 1, :] + _mm_tn(k_dec[rows], vn)
            v_new.append(vn)
        state_ref[h] = state
        o = jnp.concatenate(o_inter, axis=0) + _mm(attn, jnp.concatenate(v_new, axis=0))

        o = o * lax.rsqrt(jnp.mean(o * o, axis=-1, keepdims=True) + EPS) * nw_ref[...]
        o_ref[:, cols] = (o * _silu(z[:, cols])).astype(o_ref.dtype)


def _prev_rows_spec(tb, width, slab):
    blocks = tb // SUBLANES
    return pl.BlockSpec((SUBLANES, width), lambda i: (jnp.maximum(i * blocks - 1, 0), slab))


def _gated_deltanet(proj, conv_w, a_log, dt_bias, norm_w):
    seq = proj.shape[0]
    tb = min(GDN_TB, seq)
    lane_params = jnp.zeros((SUBLANES, LANES), F32)
    lane_params = lane_params.at[0, LANE_ALPHA:LANE_ALPHA + GDN_HEADS].set(a_log)
    lane_params = lane_params.at[1, LANE_ALPHA:LANE_ALPHA + GDN_HEADS].set(dt_bias)
    slab = lambda j: pl.BlockSpec((tb, GW), lambda i: (i, j))
    full = lambda shape: pl.BlockSpec(shape, lambda i: (0, 0))
    return pl.pallas_call(
        _gdn_kernel,
        grid=(seq // tb,),
        in_specs=[
            slab(SLAB_GQ), slab(SLAB_GK), slab(SLAB_GV), slab(SLAB_GZ),
            _prev_rows_spec(tb, GW, SLAB_GQ), _prev_rows_spec(tb, GW, SLAB_GK), _prev_rows_spec(tb, GW, SLAB_GV),
            pl.BlockSpec((tb, LANES), lambda i: (i, SLAB128_SMALL)),
            full((4, 3 * GW)), full((SUBLANES, LANES)), full((1, GDN_HEAD_DIM)),
        ],
        out_specs=pl.BlockSpec((tb, GW), lambda i: (i, 0)),
        out_shape=jax.ShapeDtypeStruct((seq, GW), BF16),
        scratch_shapes=[pltpu.VMEM((GDN_HEADS, GDN_HEAD_DIM, GDN_HEAD_DIM), F32)],
        compiler_params=_cparams("arbitrary"),
        name="gated_deltanet",
    )(proj, proj, proj, proj, proj, proj, proj, proj, conv_w, lane_params, norm_w.reshape(1, -1))


SSD_TB = 256


def _ssd_kernel(z_ref, x_ref, bc_ref, xp_ref, bcp_ref, sm_ref, cw_ref, cb_ref, lp_ref, nw_ref,
                o_ref, state_ref):
    step = pl.program_id(0)
    tb = z_ref.shape[0]
    nstate = SSM_STATE
    nchunks = tb // SSM_CHUNK
    pairs = SSM_HEADS // 2
    pairs_per_group = pairs // 2

    @pl.when(step == 0)
    def _():
        state_ref[...] = jnp.zeros_like(state_ref)

    keep_prev = jnp.where(step == 0, 0.0, 1.0)
    cw = cw_ref[...]
    cbias = cb_ref[...]
    xs = _silu(_causal_conv(x_ref[...], xp_ref[...] * keep_prev, cw[:, :GW], 4) + cbias[:, :GW])
    bc = _silu(_causal_conv(bc_ref[...], bcp_ref[...] * keep_prev, cw[:, GW:], 4) + cbias[:, GW:])
    z = z_ref[...]

    small = sm_ref[...]
    a_neg = -jnp.exp(lp_ref[0:1, :])
    dt_all = jax.nn.softplus(small + lp_ref[1:2, :])
    dskip_all = lp_ref[2:3, :]
    gc_all, gtot_all = _chunk_cumsum(dt_all * a_neg, SSM_CHUNK)
    gc_t = gc_all.T

    r = lax.broadcasted_iota(jnp.int32, (tb, tb), 0)
    s = lax.broadcasted_iota(jnp.int32, (tb, tb), 1)
    mask_incl = (_blk(r, SSM_CHUNK) == _blk(s, SSM_CHUNK)) & (s <= r)
    lane = lax.broadcasted_iota(jnp.int32, (1, LANES), 1)
    first_half = lane < SSM_HEAD_DIM

    for grp in range(2):
        b_g = bc[:, grp * nstate:(grp + 1) * nstate]
        c_g = bc[:, 2 * nstate + grp * nstate:2 * nstate + (grp + 1) * nstate]
        cb = _mm_nt(c_g, b_g)
        ys, sumsq = [], None
        for pp in range(pairs_per_group):
            pair = grp * pairs_per_group + pp
            cols = slice(pair * LANES, (pair + 1) * LANES)
            x_pair = xs[:, cols]
            heads = (2 * pair, 2 * pair + 1)
            col = [l for l in (LANE_DT + heads[0], LANE_DT + heads[1])]
            dt_lane = jnp.where(first_half, dt_all[:, col[0]:col[0] + 1], dt_all[:, col[1]:col[1] + 1])
            dskip_lane = jnp.where(first_half, dskip_all[:, col[0]:col[0] + 1], dskip_all[:, col[1]:col[1] + 1])
            xdt = x_pair * dt_lane
            xdt_half = (jnp.where(first_half, xdt, 0.0), jnp.where(first_half, 0.0, xdt))

            y_diag = None
            c_dec, b_dec, cd = [], [], []
            for hh in range(2):
                gcol = gc_all[:, col[hh]:col[hh] + 1]
                grow = gc_t[col[hh]:col[hh] + 1, :]
                gtot = gtot_all[:, col[hh]:col[hh] + 1]
                lmat = jnp.exp(jnp.where(mask_incl, gcol - grow, -jnp.inf))
                t = _mm(cb * lmat, xdt_half[hh])
                y_diag = t if y_diag is None else y_diag + t
                c_dec.append(c_g * jnp.exp(gcol))
                b_dec.append(b_g * jnp.exp(gtot - gcol))
                cd.append(jnp.exp(gtot))

            state = state_ref[pair]
            y_off = []
            for c in range(nchunks):
                rows = slice(c * SSM_CHUNK, (c + 1) * SSM_CHUNK)
                y_off.append(jnp.where(first_half, _mm(c_dec[0][rows], state), _mm(c_dec[1][rows], state)))
                cd_lane = jnp.where(first_half, cd[0][c * SSM_CHUNK:c * SSM_CHUNK + 1, :],
                                    cd[1][c * SSM_CHUNK:c * SSM_CHUNK + 1, :])
                upd = jnp.where(first_half, _mm_tn(b_dec[0][rows], xdt[rows]), _mm_tn(b_dec[1][rows], xdt[rows]))
                state = state * cd_lane + upd
            state_ref[pair] = state

            y = y_diag + jnp.concatenate(y_off, axis=0) + x_pair * dskip_lane
            y = y * _silu(z[:, cols])
            ss = jnp.sum(y * y, axis=-1, keepdims=True)
            sumsq = ss if sumsq is None else sumsq + ss
            ys.append((cols, y))
        inv = lax.rsqrt(sumsq / (pairs_per_group * LANES) + EPS)
        for cols, y in ys:
            o_ref[:, cols] = (y * inv * nw_ref[:, cols]).astype(o_ref.dtype)


def _mamba2_ssd(proj, conv_w, conv_b, a_log, dt_bias, d_skip, norm_w):
    seq = proj.shape[0]
    tb = min(SSD_TB, seq)
    lane_params = jnp.zeros((SUBLANES, LANES), F32)
    lane_params = lane_params.at[0, LANE_DT:LANE_DT + SSM_HEADS].set(a_log)
    lane_params = lane_params.at[1, LANE_DT:LANE_DT + SSM_HEADS].set(dt_bias)
    lane_params = lane_params.at[2, LANE_DT:LANE_DT + SSM_HEADS].set(d_skip)
    slab = lambda j: pl.BlockSpec((tb, GW), lambda i: (i, j))
    full = lambda shape: pl.BlockSpec(shape, lambda i: (0, 0))
    return pl.pallas_call(
        _ssd_kernel,
        grid=(seq // tb,),
        in_specs=[
            slab(SLAB_SZ), slab(SLAB_SX), slab(SLAB_SBC),
            _prev_rows_spec(tb, GW, SLAB_SX), _prev_rows_spec(tb, GW, SLAB_SBC),
            pl.BlockSpec((tb, LANES), lambda i: (i, SLAB128_SMALL)),
            full((4, 2 * GW)), full((1, 2 * GW)), full((SUBLANES, LANES)), full((1, GW)),
        ],
        out_specs=pl.BlockSpec((tb, GW), lambda i: (i, 0)),
        out_shape=jax.ShapeDtypeStruct((seq, GW), BF16),
        scratch_shapes=[pltpu.VMEM((SSM_HEADS // 2, SSM_STATE, LANES), F32)],
        compiler_params=_cparams("arbitrary"),
        name="mamba2_ssd",
    )(proj, proj, proj, proj, proj, proj, conv_w, conv_b.reshape(1, -1), lane_params, norm_w.reshape(1, -1))


def _rotate_half(x, lane_in_head_low):
    width = x.shape[1]
    half = ATTN_HEAD_DIM // 2
    ahead = pltpu.roll(x, width - half, axis=1)
    behind = pltpu.roll(x, half, axis=1)
    return jnp.where(lane_in_head_low, ahead, behind)


def _swa_kernel(q_ref, k_ref, v_ref, cos_ref, sin_ref, sink_ref, o_ref, kprev_ref, vprev_ref):
    step = pl.program_id(0)
    tb = q_ref.shape[0]
    hd = ATTN_HEAD_DIM

    @pl.when(step == 0)
    def _():
        kprev_ref[...] = jnp.zeros_like(kprev_ref)
        vprev_ref[...] = jnp.zeros_like(vprev_ref)

    cos = cos_ref[...]
    sin = sin_ref[...]
    lane = lax.broadcasted_iota(jnp.int32, (1, LANES), 1)
    low = (lane & (hd - 1)) < (hd // 2)
    first_half = lane < hd

    k_cur = k_ref[...]
    k_cur = k_cur * cos + _rotate_half(k_cur, low) * sin
    v_cur = v_ref[...]
    k_all = jnp.concatenate([kprev_ref[...], k_cur], axis=0)
    v_all = jnp.concatenate([vprev_ref[...], v_cur], axis=0)
    k_swap = pltpu.roll(k_all, hd, axis=1)
    v_swap = pltpu.roll(v_all, hd, axis=1)

    def placed(x, x_swap, kv_head, pos):
        src = x if kv_head == pos else x_swap
        keep = first_half if pos == 0 else ~first_half
        return jnp.where(keep, src, 0.0).astype(BF16)

    qi = lax.broadcasted_iota(jnp.int32, (tb, 2 * tb), 0)
    kj = lax.broadcasted_iota(jnp.int32, (tb, 2 * tb), 1)
    rel = qi + tb - kj
    mask = (rel >= 0) & (rel < tb) & ((kj >= tb) | (step > 0))

    n_pairs = ATTN_Q_HEADS // 2
    for pair in range(n_pairs):
        cols = slice(pair * LANES, (pair + 1) * LANES)
        q_pair = q_ref[:, cols]
        q_pair = (q_pair * cos + _rotate_half(q_pair, low) * sin).astype(BF16)
        out = None
        for pos in range(2):
            head = 2 * pair + pos
            kv_head = head // (ATTN_Q_HEADS // 2)
            sc = _mm_nt(q_pair, placed(k_all, k_swap, kv_head, pos)) * (hd ** -0.5)
            sc = jnp.where(mask, sc, -jnp.inf)
            sink = sink_ref[head:head + 1, 0:1]
            mx = jnp.maximum(jnp.max(sc, axis=-1, keepdims=True), sink)
            p = jnp.exp(sc - mx)
            denom = jnp.sum(p, axis=-1, keepdims=True) + jnp.exp(sink - mx)
            o = _mm(p, placed(v_all, v_swap, kv_head, pos)) / denom
            out = o if out is None else out + o
        o_ref[:, cols] = out.astype(o_ref.dtype)

    kprev_ref[...] = k_cur
    vprev_ref[...] = v_cur


def _sliding_window_attention(proj, cos, sin, sinks):
    seq = proj.shape[0]
    tb = WINDOW
    sink_rows = jnp.broadcast_to(sinks.astype(F32)[:, None], (ATTN_Q_HEADS, LANES))
    return pl.pallas_call(
        _swa_kernel,
        grid=(seq // tb,),
        in_specs=[
            pl.BlockSpec((tb, GW), lambda i: (i, SLAB_AQ)),
            pl.BlockSpec((tb, LANES), lambda i: (i, SLAB128_AK)),
            pl.BlockSpec((tb, LANES), lambda i: (i, SLAB128_AV)),
            pl.BlockSpec((tb, LANES), lambda i: (i, 0)),
            pl.BlockSpec((tb, LANES), lambda i: (i, 0)),
            pl.BlockSpec((ATTN_Q_HEADS, LANES), lambda i: (0, 0)),
        ],
        out_specs=pl.BlockSpec((tb, GW), lambda i: (i, 0)),
        out_shape=jax.ShapeDtypeStruct((seq, GW), BF16),
        scratch_shapes=[pltpu.VMEM((tb, LANES), F32), pltpu.VMEM((tb, LANES), F32)],
        compiler_params=_cparams("arbitrary"),
        name="sliding_window_attn",
    )(proj, proj, proj, cos, sin, sink_rows)


SCONV_TB = 512


def _sconv_kernel(b_ref, c_ref, h_ref, cp_ref, hp_ref, w_ref, o_ref):
    keep_prev = jnp.where(pl.program_id(0) == 0, 0.0, 1.0)
    ch = c_ref[...] * h_ref[...]
    ch_prev = cp_ref[...] * hp_ref[...] * keep_prev
    o_ref[...] = (b_ref[...] * _causal_conv(ch, ch_prev, w_ref[...], 3)).astype(o_ref.dtype)


def _short_conv(proj, conv_w):
    seq = proj.shape[0]
    tb = min(SCONV_TB, seq)
    slab = lambda j: pl.BlockSpec((tb, GW), lambda i: (i, j))
    return pl.pallas_call(
        _sconv_kernel,
        grid=(seq // tb,),
        in_specs=[
            slab(SLAB_CB), slab(SLAB_CC), slab(SLAB_CH),
            _prev_rows_spec(tb, GW, SLAB_CC), _prev_rows_spec(tb, GW, SLAB_CH),
            pl.BlockSpec((3, GW), lambda i: (0, 0)),
        ],
        out_specs=pl.BlockSpec((tb, GW), lambda i: (i, 0)),
        out_shape=jax.ShapeDtypeStruct((seq, GW), BF16),
        compiler_params=_cparams("parallel"),
        name="short_conv",
    )(proj, proj, proj, proj, proj, conv_w)


OUTPROJ_TM = 512


def _outproj_kernel(ya_ref, yb_ref, yc_ref, yd_ref, w_ref, x_ref, gate_ref, nw_ref, o_ref):
    y = jnp.concatenate([ya_ref[...], yb_ref[...], yc_ref[...], yd_ref[...]], axis=1)
    y = jnp.dot(y, w_ref[...], preferred_element_type=F32)
    y = y * lax.rsqrt(jnp.mean(y * y, axis=-1, keepdims=True) + EPS) * nw_ref[...]
    o_ref[...] = x_ref[...] + gate_ref[...] * y


def _out_projection(ys, w_bf16, x, gate, norm_w):
    seq, d = x.shape
    tm = min(OUTPROJ_TM, seq)
    row = lambda i: (0, 0)
    return pl.pallas_call(
        _outproj_kernel,
        grid=(seq // tm,),
        in_specs=[pl.BlockSpec((tm, GW), lambda i: (i, 0))] * 4 + [
            pl.BlockSpec((4 * GW, d), row),
            pl.BlockSpec((tm, d), lambda i: (i, 0)),
            pl.BlockSpec((1, d), row),
            pl.BlockSpec((1, d), row),
        ],
        out_specs=pl.BlockSpec((tm, d), lambda i: (i, 0)),
        out_shape=jax.ShapeDtypeStruct((seq, d), F32),
        compiler_params=_cparams("parallel"),
        name="out_proj",
    )(*ys, w_bf16, x, gate, norm_w)


MLP_TM = 512
MLP_TH = 512


def _mlp_kernel(x_ref, nw_ref, sc_ref, sh_ref, wu_ref, wd_ref, gate_ref, pw_ref, o_ref, h_ref, acc_ref):
    j = pl.program_id(1)

    @pl.when(j == 0)
    def _():
        _prenorm_to(x_ref, nw_ref, sc_ref, sh_ref, h_ref)
        acc_ref[...] = jnp.zeros_like(acc_ref)

    hid = jnp.dot(h_ref[...], wu_ref[...], preferred_element_type=F32)
    hid = jnp.square(jnp.maximum(hid, 0.0)).astype(BF16)
    acc_ref[...] += jnp.dot(hid, wd_ref[...], preferred_element_type=F32)

    @pl.when(j == pl.num_programs(1) - 1)
    def _():
        def body(r, carry):
            rows = pl.ds(pl.multiple_of(r * NORM_ROWS, NORM_ROWS), NORM_ROWS)
            y = acc_ref[rows, :]
            y = y * lax.rsqrt(jnp.mean(y * y, axis=-1, keepdims=True) + EPS) * pw_ref[...]
            o_ref[rows, :] = x_ref[rows, :] + gate_ref[...] * y
            return carry

        lax.fori_loop(0, x_ref.shape[0] // NORM_ROWS, body, 0)


def _mlp(x, norm_w, scale, shift, wu_bf16, wd_bf16, gate, post_w):
    seq, d = x.shape
    hidden = wu_bf16.shape[1]
    tm = min(MLP_TM, seq)
    row = lambda i, j: (0, 0)
    return pl.pallas_call(
        _mlp_kernel,
        grid=(seq // tm, hidden // MLP_TH),
        in_specs=[
            pl.BlockSpec((tm, d), lambda i, j: (i, 0)),
            pl.BlockSpec((1, d), row), pl.BlockSpec((1, d), row), pl.BlockSpec((1, d), row),
            pl.BlockSpec((d, MLP_TH), lambda i, j: (0, j)),
            pl.BlockSpec((MLP_TH, d), lambda i, j: (j, 0)),
            pl.BlockSpec((1, d), row), pl.BlockSpec((1, d), row),
        ],
        out_specs=pl.BlockSpec((tm, d), lambda i, j: (i, 0)),
        out_shape=jax.ShapeDtypeStruct((seq, d), F32),
        scratch_shapes=[pltpu.VMEM((tm, d), BF16), pltpu.VMEM((tm, d), F32)],
        compiler_params=_cparams("parallel", "arbitrary"),
        name="mlp",
    )(x, norm_w, scale, shift, wu_bf16, wd_bf16, gate, post_w)


def _permute_in_proj(w_in):
    sizes = (GW, GW, GW, GW, GDN_HEADS, GDN_HEADS,
             GW, GW + 4 * SSM_STATE, SSM_HEADS,
             GW, 2 * ATTN_HEAD_DIM, 2 * ATTN_HEAD_DIM,
             GW, GW, GW)
    offs = [0]
    for sz in sizes:
        offs.append(offs[-1] + sz)
    (gq, gk, gv, gz, gb, ga, sz_, sxbc, sdt, aq, ak, av, cb, cc, ch) = [
        w_in[:, offs[i]:offs[i + 1]] for i in range(len(sizes))]
    pad = jnp.zeros((w_in.shape[0], 2 * LANES - (2 * GDN_HEADS + SSM_HEADS)), w_in.dtype)
    out = jnp.concatenate([gq, gk, gv, gz, sz_, sxbc, aq, cb, cc, ch, ak, av, gb, ga, sdt, pad], axis=1)
    assert out.shape[1] == IN_COLS
    return out.astype(BF16)


def kernel(x, c, positions, ada_w, ada_b, norm_pre_mix, norm_post_mix, norm_pre_mlp, norm_post_mlp,
           w_in, w_out, gdn_conv_w, gdn_a_log, gdn_dt_bias, gdn_norm_w, ssm_conv_w, ssm_conv_b,
           ssm_a_log, ssm_dt_bias, ssm_d, ssm_norm_w, attn_sinks, sc_conv_w, w_up, w_down):
    batch, seq, d = x.shape
    assert batch == 1 and d == D_MODEL
    depth = ada_w.shape[0]
    xs = x.reshape(seq, d)

    mod = _modulation(c.reshape(d, 1), ada_w, ada_b)
    cos, sin = _rope_tables(positions.reshape(seq, 1))
    row = lambda v: v.reshape(1, -1)

    for i in range(depth):
        shift_a, scale_a, gate_a, shift_m, scale_m, gate_m = [mod[i, :, k * d:(k + 1) * d] for k in range(6)]
        proj = _in_projection(xs, row(norm_pre_mix[i]), scale_a, shift_a, _permute_in_proj(w_in[i]))
        y_a = _gated_deltanet(proj, gdn_conv_w[i], gdn_a_log[i], gdn_dt_bias[i], gdn_norm_w[i])
        y_b = _mamba2_ssd(proj, ssm_conv_w[i], ssm_conv_b[i], ssm_a_log[i], ssm_dt_bias[i], ssm_d[i],
                          ssm_norm_w[i])
        y_c = _sliding_window_attention(proj, cos, sin, attn_sinks[i])
        y_d = _short_conv(proj, sc_conv_w[i])
        xs = _out_projection((y_a, y_b, y_c, y_d), w_out[i].astype(BF16), xs, gate_a, row(norm_post_mix[i]))
        xs = _mlp(xs, row(norm_pre_mlp[i]), scale_m, shift_m, w_up[i].astype(BF16), w_down[i].astype(BF16),
                  gate_m, row(norm_post_mlp[i]))
    return xs.reshape(batch, seq, d)
```

```python
import functools

import jax
import jax.numpy as jnp
from jax import lax
from jax.experimental import pallas as pl
from jax.experimental.pallas import tpu as pltpu

F32 = jnp.float32
BF16 = jnp.bfloat16

D_MODEL = 2048
GW = 512
GDN_HEADS, GDN_HEAD_DIM, GDN_CHUNK = 4, 128, 64
SSM_HEADS, SSM_HEAD_DIM, SSM_STATE, SSM_CHUNK = 8, 64, 128, 128
ATTN_HEAD_DIM, ATTN_Q_HEADS, WINDOW = 64, 8, 128
ROPE_THETA = 10000.0
MLP_HIDDEN = 4 * D_MODEL
EPS = 1e-6

SUBLANES = 8
LANES = 128

IN_COLS = 11 * GW + 4 * LANES
SLAB_GQ, SLAB_GK, SLAB_GV, SLAB_GZ, SLAB_SZ, SLAB_SX, SLAB_SBC, SLAB_AQ, SLAB_CB, SLAB_CC, SLAB_CH = range(11)
SLAB128_AK, SLAB128_AV, SLAB128_SMALL = 44, 45, 46
LANE_BETA, LANE_ALPHA, LANE_DT = 0, 4, 8

VMEM_LIMIT = 56 * 1024 * 1024


def _cparams(*sem):
    return pltpu.CompilerParams(dimension_semantics=sem, vmem_limit_bytes=VMEM_LIMIT)


def _sigmoid(x):
    return jax.nn.sigmoid(x)


def _silu(x):
    return x * jax.nn.sigmoid(x)


def _mm(a, b):
    return jnp.dot(a.astype(BF16), b.astype(BF16), preferred_element_type=F32)


def _mm_nt(a, b):
    return lax.dot_general(a.astype(BF16), b.astype(BF16), (((1,), (1,)), ((), ())),
                           preferred_element_type=F32)


def _mm_tn(a, b):
    return lax.dot_general(a.astype(BF16), b.astype(BF16), (((0,), (0,)), ((), ())),
                           preferred_element_type=F32)


def _blk(idx, size):
    assert size & (size - 1) == 0
    return lax.shift_right_logical(idx, size.bit_length() - 1)


def _split3(x):
    x1 = x.astype(BF16)
    r1 = x - x1.astype(F32)
    x2 = r1.astype(BF16)
    r2 = r1 - x2.astype(F32)
    return x1, x2, r2.astype(BF16)


def _chunk_cumsum(g, chunk):
    rows = g.shape[0]
    r = lax.broadcasted_iota(jnp.int32, (2 * rows, rows), 0)
    s = lax.broadcasted_iota(jnp.int32, (2 * rows, rows), 1)
    rr = jnp.where(r >= rows, r - rows, r)
    same = _blk(rr, chunk) == _blk(s, chunk)
    sel = same & ((r >= rows) | (s <= rr))
    mat = jnp.where(sel, 1.0, 0.0).astype(BF16)
    acc = None
    for part in _split3(g):
        t = jnp.dot(mat, part, preferred_element_type=F32)
        acc = t if acc is None else acc + t
    return acc[:rows], acc[rows:]


def _causal_conv(x, prev, w, taps):
    rows = x.shape[0]
    xp = jnp.concatenate([prev, x], axis=0)
    acc = x * w[taps - 1:taps]
    for d in range(1, taps):
        acc = acc + xp[SUBLANES - d:SUBLANES - d + rows] * w[taps - 1 - d:taps - d]
    return acc


def _causal_conv_staged(x_ref, prev, w, taps, stage_ref):
    rows = x_ref.shape[0]
    x = x_ref[...]
    stage_ref[0:SUBLANES, :] = prev
    stage_ref[SUBLANES:, :] = x
    acc = x * w[taps - 1:taps]
    for d in range(1, taps):
        acc = acc + stage_ref[pl.ds(SUBLANES - d, rows), :] * w[taps - 1 - d:taps - d]
    return acc


MOD_TN = 1024
MOD_ROWS = 64


def _mod_kernel(c_ref, w_ref, b_ref, o_ref):
    d = c_ref.shape[0]

    def body(r, acc):
        rows = pl.ds(pl.multiple_of(r * MOD_ROWS, MOD_ROWS), MOD_ROWS)
        cc = _silu(c_ref[rows, :])
        prod = w_ref[0, rows, :] * cc
        return acc + jnp.sum(prod.reshape(MOD_ROWS // SUBLANES, SUBLANES, MOD_TN), axis=0)

    acc = lax.fori_loop(0, d // MOD_ROWS, body, jnp.zeros((SUBLANES, MOD_TN), F32))
    o_ref[0] = jnp.sum(acc, axis=0, keepdims=True) + b_ref[0]


def _modulation(c_col, ada_w, ada_b):
    depth, d, n = ada_w.shape
    out = pl.pallas_call(
        _mod_kernel,
        grid=(depth, n // MOD_TN),
        in_specs=[
            pl.BlockSpec((d, 1), lambda i, j: (0, 0)),
            pl.BlockSpec((1, d, MOD_TN), lambda i, j: (i, 0, j)),
            pl.BlockSpec((1, 1, MOD_TN), lambda i, j: (i, 0, j)),
        ],
        out_specs=pl.BlockSpec((1, 1, MOD_TN), lambda i, j: (i, 0, j)),
        out_shape=jax.ShapeDtypeStruct((depth, 1, n), F32),
        compiler_params=_cparams("parallel", "parallel"),
        name="adaln_mod",
    )(c_col, ada_w, ada_b.reshape(depth, 1, n))
    return out


ROPE_TB = 1024


def _rope_kernel(pos_ref, invf_ref, sign_ref, cos_ref, sin_ref):
    ang = pos_ref[...].astype(F32) * invf_ref[...]
    cos_ref[...] = jnp.cos(ang)
    sin_ref[...] = jnp.sin(ang) * sign_ref[...]


def _rope_tables(pos_col):
    seq = pos_col.shape[0]
    half = ATTN_HEAD_DIM // 2
    inv_freq = ROPE_THETA ** (-jnp.arange(0, ATTN_HEAD_DIM, 2, dtype=F32) / ATTN_HEAD_DIM)
    invf = jnp.tile(inv_freq, LANES // half).reshape(1, LANES)
    lane = jnp.arange(LANES)
    sign = jnp.where((lane % ATTN_HEAD_DIM) < half, -1.0, 1.0).astype(F32).reshape(1, LANES)
    tb = min(ROPE_TB, seq)
    return pl.pallas_call(
        _rope_kernel,
        grid=(seq // tb,),
        in_specs=[
            pl.BlockSpec((tb, 1), lambda i: (i, 0)),
            pl.BlockSpec((1, LANES), lambda i: (0, 0)),
            pl.BlockSpec((1, LANES), lambda i: (0, 0)),
        ],
        out_specs=[pl.BlockSpec((tb, LANES), lambda i: (i, 0))] * 2,
        out_shape=[jax.ShapeDtypeStruct((seq, LANES), F32)] * 2,
        compiler_params=_cparams("parallel"),
        name="rope_tables",
    )(pos_col, invf, sign)


NORM_ROWS = 128


def _prenorm_to(x_ref, nw_ref, sc_ref, sh_ref, h_ref):
    tm = x_ref.shape[0]

    def body(r, carry):
        rows = pl.ds(pl.multiple_of(r * NORM_ROWS, NORM_ROWS), NORM_ROWS)
        x = x_ref[rows, :]
        y = x * lax.rsqrt(jnp.mean(x * x, axis=-1, keepdims=True) + EPS) * nw_ref[...]
        h_ref[rows, :] = (y * (1.0 + sc_ref[...]) + sh_ref[...]).astype(BF16)
        return carry

    lax.fori_loop(0, tm // NORM_ROWS, body, 0)


INPROJ_TM = 1024
INPROJ_TN = 1536

MOD_SHIFT_A, MOD_SCALE_A, MOD_GATE_A, MOD_SHIFT_M, MOD_SCALE_M, MOD_GATE_M = range(6)


def _layer_spec(shape, layer, *rest):
    rest = rest or (0,) * len(shape)

    def index_map(*grid_idx):
        return (layer,) + tuple(r(*grid_idx) if callable(r) else r for r in rest)

    return pl.BlockSpec((None,) + tuple(shape), index_map)


def _inproj_kernel(x_ref, nw_ref, sc_ref, sh_ref, w_ref, o_ref, h_ref):
    @pl.when(pl.program_id(1) == 0)
    def _():
        _prenorm_to(x_ref, nw_ref, sc_ref, sh_ref, h_ref)

    o_ref[...] = jnp.dot(h_ref[...], w_ref[...], preferred_element_type=F32)


def _in_projection(x, layer, norm_w, mod, w_bf16):
    seq, d = x.shape
    n = w_bf16.shape[2]
    tm = min(INPROJ_TM, seq)
    return pl.pallas_call(
        _inproj_kernel,
        grid=(seq // tm, n // INPROJ_TN),
        in_specs=[
            pl.BlockSpec((tm, d), lambda i, j: (i, 0)),
            _layer_spec((1, d), layer),
            _layer_spec((1, d), layer, 0, MOD_SCALE_A),
            _layer_spec((1, d), layer, 0, MOD_SHIFT_A),
            _layer_spec((d, INPROJ_TN), layer, 0, lambda i, j: j),
        ],
        out_specs=pl.BlockSpec((tm, INPROJ_TN), lambda i, j: (i, j)),
        out_shape=jax.ShapeDtypeStruct((seq, n), F32),
        scratch_shapes=[pltpu.VMEM((tm, d), BF16)],
        compiler_params=_cparams("parallel", "arbitrary"),
        name="in_proj",
    )(x, norm_w, mod, mod, w_bf16)


GDN_TB = 256


def _l2norm(x):
    return x * lax.rsqrt(jnp.sum(x * x, axis=-1, keepdims=True) + EPS)


GDN_BLK = 128


def _unit_lower_inverse_minus_identity(ms, r, s):
    same8 = _blk(r, 8) == _blk(s, 8)
    same16 = _blk(r, 16) == _blk(s, 16)
    same32 = _blk(r, 32) == _blk(s, 32)
    n1 = [jnp.where(same8, -m, 0.0) for m in ms]
    n2 = [_mm(a, a) for a in n1]
    n3 = [_mm(a, b) for a, b in zip(n1, n2)]
    n4 = [_mm(b, b) for b in n2]
    ys = [a + b + c for a, b, c in zip(n1, n2, n3)]
    ts = [_mm(y, d) for y, d in zip(ys, n4)]
    ys = [y + d + t for y, d, t in zip(ys, n4, ts)]
    for sel in (same16 & ~same8, same32 & ~same16, ~same32):
        blks = [jnp.where(sel, m, 0.0) for m in ms]
        cs = [b + _mm(y, b) for y, b in zip(ys, blks)]
        ts = [_mm(c, y) for c, y in zip(cs, ys)]
        ys = [y - c - t for y, c, t in zip(ys, cs, ts)]
    return ys


def _gdn_kernel(q_ref, k_ref, v_ref, z_ref, qp_ref, kp_ref, vp_ref, sm_ref, cw_ref, lp_ref, nw_ref,
                o_ref, state_ref, conv_ref):
    step = pl.program_id(0)
    tb = q_ref.shape[0]
    hd = GDN_HEAD_DIM
    nblk = tb // GDN_BLK
    chunks_per_blk = GDN_BLK // GDN_CHUNK

    @pl.when(step == 0)
    def _():
        state_ref[...] = jnp.zeros_like(state_ref)

    keep_prev = jnp.where(step == 0, 0.0, 1.0)
    cw = cw_ref[...]

    def conv_silu(x_ref, p_ref, idx):
        w = cw[:, idx * GW:(idx + 1) * GW]
        return _silu(_causal_conv_staged(x_ref, p_ref[...] * keep_prev, w, 4, conv_ref.at[idx]))

    q = conv_silu(q_ref, qp_ref, 0)
    k = conv_silu(k_ref, kp_ref, 1)
    v = conv_silu(v_ref, vp_ref, 2)
    z = z_ref[...]

    small = sm_ref[...]
    a_log = lp_ref[0:1, :]
    dt_bias = lp_ref[1:2, :]
    beta_all = _sigmoid(small)
    g_all = -jnp.exp(a_log) * jax.nn.softplus(small + dt_bias)
    gc_all, gtot_all = _chunk_cumsum(g_all, GDN_CHUNK)
    gc_t = gc_all.T

    r = lax.broadcasted_iota(jnp.int32, (GDN_BLK, GDN_BLK), 0)
    s = lax.broadcasted_iota(jnp.int32, (GDN_BLK, GDN_BLK), 1)
    same_chunk = _blk(r, GDN_CHUNK) == _blk(s, GDN_CHUNK)
    mask_incl = same_chunk & (s <= r)
    mask_strict = same_chunk & (s < r)

    problems = [(slice(b * GDN_BLK, (b + 1) * GDN_BLK), h) for b in range(nblk) for h in range(GDN_HEADS)]
    head_cols = lambda h: slice(h * hd, (h + 1) * hd)
    gate = lambda arr, rows, lane: arr[rows, lane:lane + 1]

    qn = [_l2norm(q[rows, head_cols(h)]) * (hd ** -0.5) for rows, h in problems]
    kn = [_l2norm(k[rows, head_cols(h)]) for rows, h in problems]
    beta = [gate(beta_all, rows, LANE_BETA + h) for rows, h in problems]
    gcol = [gate(gc_all, rows, LANE_ALPHA + h) for rows, h in problems]
    gtot = [gate(gtot_all, rows, LANE_ALPHA + h) for rows, h in problems]
    grow = [gc_t[LANE_ALPHA + h:LANE_ALPHA + h + 1, rows] for rows, h in problems]

    decay = [jnp.exp(jnp.where(mask_incl, gc - gr, -jnp.inf)) for gc, gr in zip(gcol, grow)]
    k_beta = [kk * b for kk, b in zip(kn, beta)]
    k16 = [kk.astype(BF16) for kk in kn]
    ms = [jnp.where(mask_strict, _mm_nt(kb, kk) * d, 0.0) for kb, kk, d in zip(k_beta, k16, decay)]
    attn = [_mm_nt(qq, kk) * d for qq, kk, d in zip(qn, k16, decay)]
    ys = _unit_lower_inverse_minus_identity(ms, r, s)

    e_gc = [jnp.exp(gc) for gc in gcol]
    rhs = [jnp.concatenate([v[rows, head_cols(h)] * b, kb * e], axis=1)
           for (rows, h), b, kb, e in zip(problems, beta, k_beta, e_gc)]
    uw = [x + _mm(y, x) for y, x in zip(ys, rhs)]
    q_dec = [(qq * e).astype(BF16) for qq, e in zip(qn, e_gc)]
    k_dec = [(kk * jnp.exp(gt - gc)).astype(BF16) for kk, gt, gc in zip(kn, gtot, gcol)]
    chunk_dec = [jnp.exp(gt) for gt in gtot]

    states = [state_ref[h] for h in range(GDN_HEADS)]
    v_new = [[] for _ in problems]
    o_inter = [[] for _ in problems]
    for b in range(nblk):
        for c in range(chunks_per_blk):
            rows = slice(c * GDN_CHUNK, (c + 1) * GDN_CHUNK)
            for h in range(GDN_HEADS):
                p = b * GDN_HEADS + h
                st16 = states[h].astype(BF16)
                vn = uw[p][rows, :hd] - jnp.dot(uw[p][rows, hd:].astype(BF16), st16, preferred_element_type=F32)
                o_inter[p].append(jnp.dot(q_dec[p][rows], st16, preferred_element_type=F32))
                states[h] = (states[h] * chunk_dec[p][c * GDN_CHUNK:c * GDN_CHUNK + 1, :]
                             + _mm_tn(k_dec[p][rows], vn))
                v_new[p].append(vn)
    for h in range(GDN_HEADS):
        state_ref[h] = states[h]

    for p, (rows, h) in enumerate(problems):
        o = jnp.concatenate(o_inter[p], axis=0) + _mm(attn[p], jnp.concatenate(v_new[p], axis=0))
        o = o * lax.rsqrt(jnp.mean(o * o, axis=-1, keepdims=True) + EPS) * nw_ref[...]
        o_ref[rows, head_cols(h)] = (o * _silu(z[rows, head_cols(h)])).astype(o_ref.dtype)


def _prev_rows_spec(tb, width, slab):
    blocks = tb // SUBLANES
    return pl.BlockSpec((SUBLANES, width), lambda i: (jnp.maximum(i * blocks - 1, 0), slab))


def _lane_params(lane0, *rows):
    stacked = jnp.stack([r.astype(F32) for r in rows], axis=1)
    nrows, heads = stacked.shape[1:]
    return jnp.pad(stacked, ((0, 0), (0, SUBLANES - nrows), (lane0, LANES - lane0 - heads)))


def _gated_deltanet(proj, layer, conv_w, lane_params, norm_w):
    seq = proj.shape[0]
    tb = min(GDN_TB, seq)
    slab = lambda j: pl.BlockSpec((tb, GW), lambda i: (i, j))
    full = lambda shape: _layer_spec(shape, layer)
    return pl.pallas_call(
        _gdn_kernel,
        grid=(seq // tb,),
        in_specs=[
            slab(SLAB_GQ), slab(SLAB_GK), slab(SLAB_GV), slab(SLAB_GZ),
            _prev_rows_spec(tb, GW, SLAB_GQ), _prev_rows_spec(tb, GW, SLAB_GK), _prev_rows_spec(tb, GW, SLAB_GV),
            pl.BlockSpec((tb, LANES), lambda i: (i, SLAB128_SMALL)),
            full((4, 3 * GW)), full((SUBLANES, LANES)), full((1, GDN_HEAD_DIM)),
        ],
        out_specs=pl.BlockSpec((tb, GW), lambda i: (i, 0)),
        out_shape=jax.ShapeDtypeStruct((seq, GW), BF16),
        scratch_shapes=[pltpu.VMEM((GDN_HEADS, GDN_HEAD_DIM, GDN_HEAD_DIM), F32),
                        pltpu.VMEM((3, tb + SUBLANES, GW), F32)],
        compiler_params=_cparams("arbitrary"),
        name="gated_deltanet",
    )(proj, proj, proj, proj, proj, proj, proj, proj, conv_w, lane_params, norm_w)


SSD_TB = 256


def _ssd_kernel(z_ref, x_ref, bc_ref, xp_ref, bcp_ref, sm_ref, cw_ref, cb_ref, lp_ref, nw_ref,
                o_ref, state_ref):
    step = pl.program_id(0)
    tb = z_ref.shape[0]
    nstate = SSM_STATE
    nchunks = tb // SSM_CHUNK
    pairs = SSM_HEADS // 2
    pairs_per_group = pairs // 2

    @pl.when(step == 0)
    def _():
        state_ref[...] = jnp.zeros_like(state_ref)

    keep_prev = jnp.where(step == 0, 0.0, 1.0)
    cw = cw_ref[...]
    cbias = cb_ref[...]
    xs = _silu(_causal_conv(x_ref[...], xp_ref[...] * keep_prev, cw[:, :GW], 4) + cbias[:, :GW])
    bc = _silu(_causal_conv(bc_ref[...], bcp_ref[...] * keep_prev, cw[:, GW:], 4) + cbias[:, GW:])
    z = z_ref[...]

    small = sm_ref[...]
    a_neg = -jnp.exp(lp_ref[0:1, :])
    dt_all = jax.nn.softplus(small + lp_ref[1:2, :])
    dskip_all = lp_ref[2:3, :]
    gc_all, gtot_all = _chunk_cumsum(dt_all * a_neg, SSM_CHUNK)
    gc_t = gc_all.T

    r = lax.broadcasted_iota(jnp.int32, (tb, tb), 0)
    s = lax.broadcasted_iota(jnp.int32, (tb, tb), 1)
    mask_incl = (_blk(r, SSM_CHUNK) == _blk(s, SSM_CHUNK)) & (s <= r)
    lane = lax.broadcasted_iota(jnp.int32, (1, LANES), 1)
    first_half = lane < SSM_HEAD_DIM

    for grp in range(2):
        b_g = bc[:, grp * nstate:(grp + 1) * nstate]
        c_g = bc[:, 2 * nstate + grp * nstate:2 * nstate + (grp + 1) * nstate]
        cb = _mm_nt(c_g, b_g)
        ys, sumsq = [], None
        for pp in range(pairs_per_group):
            pair = grp * pairs_per_group + pp
            cols = slice(pair * LANES, (pair + 1) * LANES)
            x_pair = xs[:, cols]
            heads = (2 * pair, 2 * pair + 1)
            col = [l for l in (LANE_DT + heads[0], LANE_DT + heads[1])]
            dt_lane = jnp.where(first_half, dt_all[:, col[0]:col[0] + 1], dt_all[:, col[1]:col[1] + 1])
            dskip_lane = jnp.where(first_half, dskip_all[:, col[0]:col[0] + 1], dskip_all[:, col[1]:col[1] + 1])
            xdt = x_pair * dt_lane
            xdt_half = (jnp.where(first_half, xdt, 0.0), jnp.where(first_half, 0.0, xdt))

            y_diag = None
            c_dec, b_dec, cd = [], [], []
            for hh in range(2):
                gcol = gc_all[:, col[hh]:col[hh] + 1]
                grow = gc_t[col[hh]:col[hh] + 1, :]
                gtot = gtot_all[:, col[hh]:col[hh] + 1]
                lmat = jnp.exp(jnp.where(mask_incl, gcol - grow, -jnp.inf))
                t = _mm(cb * lmat, xdt_half[hh])
                y_diag = t if y_diag is None else y_diag + t
                c_dec.append(c_g * jnp.exp(gcol))
                b_dec.append(b_g * jnp.exp(gtot - gcol))
                cd.append(jnp.exp(gtot))

            state = state_ref[pair]
            y_off = []
            for c in range(nchunks):
                rows = slice(c * SSM_CHUNK, (c + 1) * SSM_CHUNK)
                y_off.append(jnp.where(first_half, _mm(c_dec[0][rows], state), _mm(c_dec[1][rows], state)))
                cd_lane = jnp.where(first_half, cd[0][c * SSM_CHUNK:c * SSM_CHUNK + 1, :],
                                    cd[1][c * SSM_CHUNK:c * SSM_CHUNK + 1, :])
                upd = jnp.where(first_half, _mm_tn(b_dec[0][rows], xdt[rows]), _mm_tn(b_dec[1][rows], xdt[rows]))
                state = state * cd_lane + upd
            state_ref[pair] = state

            y = y_diag + jnp.concatenate(y_off, axis=0) + x_pair * dskip_lane
            y = y * _silu(z[:, cols])
            ss = jnp.sum(y * y, axis=-1, keepdims=True)
            sumsq = ss if sumsq is None else sumsq + ss
            ys.append((cols, y))
        inv = lax.rsqrt(sumsq / (pairs_per_group * LANES) + EPS)
        for cols, y in ys:
            o_ref[:, cols] = (y * inv * nw_ref[:, cols]).astype(o_ref.dtype)


def _mamba2_ssd(proj, layer, conv_w, conv_b, lane_params, norm_w):
    seq = proj.shape[0]
    tb = min(SSD_TB, seq)
    slab = lambda j: pl.BlockSpec((tb, GW), lambda i: (i, j))
    full = lambda shape: _layer_spec(shape, layer)
    return pl.pallas_call(
        _ssd_kernel,
        grid=(seq // tb,),
        in_specs=[
            slab(SLAB_SZ), slab(SLAB_SX), slab(SLAB_SBC),
            _prev_rows_spec(tb, GW, SLAB_SX), _prev_rows_spec(tb, GW, SLAB_SBC),
            pl.BlockSpec((tb, LANES), lambda i: (i, SLAB128_SMALL)),
            full((4, 2 * GW)), full((1, 2 * GW)), full((SUBLANES, LANES)), full((1, GW)),
        ],
        out_specs=pl.BlockSpec((tb, GW), lambda i: (i, 0)),
        out_shape=jax.ShapeDtypeStruct((seq, GW), BF16),
        scratch_shapes=[pltpu.VMEM((SSM_HEADS // 2, SSM_STATE, LANES), F32)],
        compiler_params=_cparams("arbitrary"),
        name="mamba2_ssd",
    )(proj, proj, proj, proj, proj, proj, conv_w, conv_b, lane_params, norm_w)


def _rotate_half(x, lane_in_head_low):
    width = x.shape[1]
    half = ATTN_HEAD_DIM // 2
    ahead = pltpu.roll(x, width - half, axis=1)
    behind = pltpu.roll(x, half, axis=1)
    return jnp.where(lane_in_head_low, ahead, behind)


SWA_TB = 256


def _swa_kernel(q_ref, k_ref, v_ref, cos_ref, sin_ref, sink_ref, o_ref, kprev_ref, vprev_ref):
    step = pl.program_id(0)
    tb = q_ref.shape[0]
    win = WINDOW
    hd = ATTN_HEAD_DIM

    @pl.when(step == 0)
    def _():
        kprev_ref[...] = jnp.zeros_like(kprev_ref)
        vprev_ref[...] = jnp.zeros_like(vprev_ref)

    cos = cos_ref[...]
    sin = sin_ref[...]
    lane = lax.broadcasted_iota(jnp.int32, (1, LANES), 1)
    low = (lane & (hd - 1)) < (hd // 2)
    first_half = lane < hd

    k_cur = k_ref[...]
    k_cur = k_cur * cos + _rotate_half(k_cur, low) * sin
    v_cur = v_ref[...]
    k_all = jnp.concatenate([kprev_ref[...], k_cur], axis=0)
    v_all = jnp.concatenate([vprev_ref[...], v_cur], axis=0)
    k_swap = pltpu.roll(k_all, hd, axis=1)
    v_swap = pltpu.roll(v_all, hd, axis=1)

    def placed(x, x_swap, kv_head, pos):
        src = x if kv_head == pos else x_swap
        keep = first_half if pos == 0 else ~first_half
        return jnp.where(keep, src, 0.0).astype(BF16)

    qi = lax.broadcasted_iota(jnp.int32, (win, 2 * win), 0)
    kj = lax.broadcasted_iota(jnp.int32, (win, 2 * win), 1)
    rel = qi + win - kj
    band = (rel >= 0) & (rel < win)
    band_first = band & ((kj >= win) | (step > 0))

    problems = [(j, h) for j in range(tb // win) for h in range(ATTN_Q_HEADS)]
    kv_of = lambda head: head // (ATTN_Q_HEADS // 2)
    pair_cols = lambda pair: slice(pair * LANES, (pair + 1) * LANES)
    k_at = {(g, pos): placed(k_all, k_swap, g, pos) for g in range(2) for pos in range(2)}
    v_at = {(g, pos): placed(v_all, v_swap, g, pos) for g in range(2) for pos in range(2)}
    q_pairs = []
    for pair in range(ATTN_Q_HEADS // 2):
        q_pair = q_ref[:, pair_cols(pair)]
        q_pairs.append((q_pair * cos + _rotate_half(q_pair, low) * sin).astype(BF16))
    sinks = [sink_ref[h:h + 1, 0:1] for _, h in problems]
    sc = [jnp.where(band_first if j == 0 else band,
                    _mm_nt(q_pairs[h // 2][j * win:(j + 1) * win],
                           k_at[kv_of(h), h % 2][j * win:(j + 2) * win]) * (hd ** -0.5),
                    -jnp.inf)
          for j, h in problems]
    mx = [jnp.maximum(jnp.max(s_, axis=-1, keepdims=True), sk) for s_, sk in zip(sc, sinks)]
    p = [jnp.exp(s_ - m_) for s_, m_ in zip(sc, mx)]
    denom = [jnp.sum(p_, axis=-1, keepdims=True) + jnp.exp(sk - m_) for p_, sk, m_ in zip(p, sinks, mx)]
    o = [_mm(p_, v_at[kv_of(h), h % 2][j * win:(j + 2) * win]) / d_
         for (j, h), p_, d_ in zip(problems, p, denom)]
    for idx in range(0, len(problems), 2):
        j, h = problems[idx]
        o_ref[j * win:(j + 1) * win, pair_cols(h // 2)] = (o[idx] + o[idx + 1]).astype(o_ref.dtype)

    kprev_ref[...] = k_cur[tb - win:]
    vprev_ref[...] = v_cur[tb - win:]


def _sliding_window_attention(proj, layer, cos, sin, sink_rows):
    seq = proj.shape[0]
    tb = min(SWA_TB, seq)
    return pl.pallas_call(
        _swa_kernel,
        grid=(seq // tb,),
        in_specs=[
            pl.BlockSpec((tb, GW), lambda i: (i, SLAB_AQ)),
            pl.BlockSpec((tb, LANES), lambda i: (i, SLAB128_AK)),
            pl.BlockSpec((tb, LANES), lambda i: (i, SLAB128_AV)),
            pl.BlockSpec((tb, LANES), lambda i: (i, 0)),
            pl.BlockSpec((tb, LANES), lambda i: (i, 0)),
            _layer_spec((ATTN_Q_HEADS, LANES), layer),
        ],
        out_specs=pl.BlockSpec((tb, GW), lambda i: (i, 0)),
        out_shape=jax.ShapeDtypeStruct((seq, GW), BF16),
        scratch_shapes=[pltpu.VMEM((WINDOW, LANES), F32), pltpu.VMEM((WINDOW, LANES), F32)],
        compiler_params=_cparams("arbitrary"),
        name="sliding_window_attn",
    )(proj, proj, proj, cos, sin, sink_rows)


SCONV_TB = 512


def _sconv_kernel(b_ref, c_ref, h_ref, cp_ref, hp_ref, w_ref, o_ref):
    keep_prev = jnp.where(pl.program_id(0) == 0, 0.0, 1.0)
    ch = c_ref[...] * h_ref[...]
    ch_prev = cp_ref[...] * hp_ref[...] * keep_prev
    o_ref[...] = (b_ref[...] * _causal_conv(ch, ch_prev, w_ref[...], 3)).astype(o_ref.dtype)


def _short_conv(proj, layer, conv_w):
    seq = proj.shape[0]
    tb = min(SCONV_TB, seq)
    slab = lambda j: pl.BlockSpec((tb, GW), lambda i: (i, j))
    return pl.pallas_call(
        _sconv_kernel,
        grid=(seq // tb,),
        in_specs=[
            slab(SLAB_CB), slab(SLAB_CC), slab(SLAB_CH),
            _prev_rows_spec(tb, GW, SLAB_CC), _prev_rows_spec(tb, GW, SLAB_CH),
            _layer_spec((3, GW), layer),
        ],
        out_specs=pl.BlockSpec((tb, GW), lambda i: (i, 0)),
        out_shape=jax.ShapeDtypeStruct((seq, GW), BF16),
        compiler_params=_cparams("parallel"),
        name="short_conv",
    )(proj, proj, proj, proj, proj, conv_w)


OUTPROJ_TM = 512


def _outproj_kernel(ya_ref, yb_ref, yc_ref, yd_ref, w_ref, x_ref, gate_ref, nw_ref, o_ref):
    y = jnp.concatenate([ya_ref[...], yb_ref[...], yc_ref[...], yd_ref[...]], axis=1)
    y = jnp.dot(y, w_ref[...], preferred_element_type=F32)
    y = y * lax.rsqrt(jnp.mean(y * y, axis=-1, keepdims=True) + EPS) * nw_ref[...]
    o_ref[...] = x_ref[...] + gate_ref[...] * y


def _out_projection(ys, layer, w_bf16, x, mod, norm_w):
    seq, d = x.shape
    tm = min(OUTPROJ_TM, seq)
    return pl.pallas_call(
        _outproj_kernel,
        grid=(seq // tm,),
        in_specs=[pl.BlockSpec((tm, GW), lambda i: (i, 0))] * 4 + [
            _layer_spec((4 * GW, d), layer),
            pl.BlockSpec((tm, d), lambda i: (i, 0)),
            _layer_spec((1, d), layer, 0, MOD_GATE_A),
            _layer_spec((1, d), layer),
        ],
        out_specs=pl.BlockSpec((tm, d), lambda i: (i, 0)),
        out_shape=jax.ShapeDtypeStruct((seq, d), F32),
        compiler_params=_cparams("parallel"),
        name="out_proj",
    )(*ys, w_bf16, x, mod, norm_w)


MLP_TM = 1024
MLP_TH = 512


def _mlp_kernel(x_ref, nw_ref, sc_ref, sh_ref, wu_ref, wd_ref, gate_ref, pw_ref, o_ref, h_ref):
    j = pl.program_id(1)

    @pl.when(j == 0)
    def _():
        _prenorm_to(x_ref, nw_ref, sc_ref, sh_ref, h_ref)

    hid = jnp.dot(h_ref[...], wu_ref[...], preferred_element_type=F32)
    hid = jnp.square(jnp.maximum(hid, 0.0)).astype(BF16)
    part = jnp.dot(hid, wd_ref[...], preferred_element_type=F32)

    @pl.when(j == 0)
    def _():
        o_ref[...] = part

    @pl.when(j > 0)
    def _():
        o_ref[...] += part

    @pl.when(j == pl.num_programs(1) - 1)
    def _():
        def body(r, carry):
            rows = pl.ds(pl.multiple_of(r * NORM_ROWS, NORM_ROWS), NORM_ROWS)
            y = o_ref[rows, :]
            y = y * lax.rsqrt(jnp.mean(y * y, axis=-1, keepdims=True) + EPS) * pw_ref[...]
            o_ref[rows, :] = x_ref[rows, :] + gate_ref[...] * y
            return carry

        lax.fori_loop(0, x_ref.shape[0] // NORM_ROWS, body, 0)


def _mlp(x, layer, norm_w, mod, wu_bf16, wd_bf16, post_w):
    seq, d = x.shape
    hidden = wu_bf16.shape[2]
    tm = min(MLP_TM, seq)
    return pl.pallas_call(
        _mlp_kernel,
        grid=(seq // tm, hidden // MLP_TH),
        in_specs=[
            pl.BlockSpec((tm, d), lambda i, j: (i, 0)),
            _layer_spec((1, d), layer),
            _layer_spec((1, d), layer, 0, MOD_SCALE_M),
            _layer_spec((1, d), layer, 0, MOD_SHIFT_M),
            _layer_spec((d, MLP_TH), layer, 0, lambda i, j: j),
            _layer_spec((MLP_TH, d), layer, lambda i, j: j, 0),
            _layer_spec((1, d), layer, 0, MOD_GATE_M),
            _layer_spec((1, d), layer),
        ],
        out_specs=pl.BlockSpec((tm, d), lambda i, j: (i, 0)),
        out_shape=jax.ShapeDtypeStruct((seq, d), F32),
        scratch_shapes=[pltpu.VMEM((tm, d), BF16)],
        compiler_params=_cparams("parallel", "arbitrary"),
        name="mlp",
    )(x, norm_w, mod, mod, wu_bf16, wd_bf16, mod, post_w)


def _permute_in_proj(w_in):
    sizes = (GW, GW, GW, GW, GDN_HEADS, GDN_HEADS,
             GW, GW + 4 * SSM_STATE, SSM_HEADS,
             GW, 2 * ATTN_HEAD_DIM, 2 * ATTN_HEAD_DIM,
             GW, GW, GW)
    offs = [0]
    for sz in sizes:
        offs.append(offs[-1] + sz)
    (gq, gk, gv, gz, gb, ga, sz_, sxbc, sdt, aq, ak, av, cb, cc, ch) = [
        w_in[..., offs[i]:offs[i + 1]].astype(BF16) for i in range(len(sizes))]
    pad = jnp.zeros(w_in.shape[:-1] + (2 * LANES - (2 * GDN_HEADS + SSM_HEADS),), BF16)
    out = jnp.concatenate([gq, gk, gv, gz, sz_, sxbc, aq, cb, cc, ch, ak, av, gb, ga, sdt, pad], axis=-1)
    assert out.shape[-1] == IN_COLS
    return out


def kernel(x, c, positions, ada_w, ada_b, norm_pre_mix, norm_post_mix, norm_pre_mlp, norm_post_mlp,
           w_in, w_out, gdn_conv_w, gdn_a_log, gdn_dt_bias, gdn_norm_w, ssm_conv_w, ssm_conv_b,
           ssm_a_log, ssm_dt_bias, ssm_d, ssm_norm_w, attn_sinks, sc_conv_w, w_up, w_down):
    batch, seq, d = x.shape
    assert batch == 1 and d == D_MODEL
    depth = ada_w.shape[0]
    xs = x.reshape(seq, d)

    mod = _modulation(c.reshape(d, 1), ada_w, ada_b)
    cos, sin = _rope_tables(positions.reshape(seq, 1))

    rows = lambda v: v.reshape(depth, 1, -1)
    w_in_bf16 = _permute_in_proj(w_in)
    w_out_bf16, w_up_bf16, w_down_bf16 = w_out.astype(BF16), w_up.astype(BF16), w_down.astype(BF16)
    gdn_lanes = _lane_params(LANE_ALPHA, gdn_a_log, gdn_dt_bias)
    ssm_lanes = _lane_params(LANE_DT, ssm_a_log, ssm_dt_bias, ssm_d)
    sink_rows = jnp.broadcast_to(attn_sinks.astype(F32)[:, :, None], (depth, ATTN_Q_HEADS, LANES))

    for i in range(depth):
        proj = _in_projection(xs, i, rows(norm_pre_mix), mod, w_in_bf16)
        y_a = _gated_deltanet(proj, i, gdn_conv_w, gdn_lanes, rows(gdn_norm_w))
        y_b = _mamba2_ssd(proj, i, ssm_conv_w, rows(ssm_conv_b), ssm_lanes, rows(ssm_norm_w))
        y_c = _sliding_window_attention(proj, i, cos, sin, sink_rows)
        y_d = _short_conv(proj, i, sc_conv_w)
        xs = _out_projection((y_a, y_b, y_c, y_d), i, w_out_bf16, xs, mod, rows(norm_post_mix))
        xs = _mlp(xs, i, rows(norm_pre_mlp), mod, w_up_bf16, w_down_bf16, rows(norm_post_mlp))
    return xs.reshape(batch, seq, d)
```

```python
import functools

import jax
import jax.numpy as jnp
from jax import lax
from jax.experimental import pallas as pl
from jax.experimental.pallas import tpu as pltpu

F32 = jnp.float32
BF16 = jnp.bfloat16

D_MODEL = 2048
GW = 512
GDN_HEADS, GDN_HEAD_DIM, GDN_CHUNK = 4, 128, 64
SSM_HEADS, SSM_HEAD_DIM, SSM_STATE, SSM_CHUNK = 8, 64, 128, 128
ATTN_HEAD_DIM, ATTN_Q_HEADS, WINDOW = 64, 8, 128
ROPE_THETA = 10000.0
MLP_HIDDEN = 4 * D_MODEL
EPS = 1e-6

SUBLANES = 8
LANES = 128

N_WIDE_SLABS = 11
IN_COLS = (N_WIDE_SLABS + 1) * GW
SLAB_GQ, SLAB_GK, SLAB_GV, SLAB_GZ, SLAB_SZ, SLAB_SX, SLAB_SBC, SLAB_AQ, SLAB_CB, SLAB_CC, SLAB_CH = range(11)
SLAB128_AK, SLAB128_AV, SLAB128_SMALL = 44, 45, 46
LANE_BETA, LANE_ALPHA, LANE_DT = 0, 4, 8

_IN_SEGMENTS = (("gq", GW), ("gk", GW), ("gv", GW), ("gz", GW), ("gb", GDN_HEADS), ("ga", GDN_HEADS),
                ("sz", GW), ("sx", GW), ("sbc", 4 * SSM_STATE), ("sdt", SSM_HEADS),
                ("aq", GW), ("ak", 2 * ATTN_HEAD_DIM), ("av", 2 * ATTN_HEAD_DIM),
                ("cb", GW), ("cc", GW), ("ch", GW))
IN_OFFSET = {}
_off = 0
for _name, _size in _IN_SEGMENTS:
    IN_OFFSET[_name] = (_off, _off + _size)
    _off += _size
IN_WIDTH = _off
_WIDE_ORDER = ("gq", "gk", "gv", "gz", "sz", "sx", "sbc", "aq", "cb", "cc", "ch")
WIDE_SLAB_SKIPS = []
_skipped = 0
for _k, _name in enumerate(_WIDE_ORDER):
    _extra = IN_OFFSET[_name][0] - _k * GW - _skipped
    assert _extra >= 0 and IN_OFFSET[_name][0] % SUBLANES == 0
    if _extra:
        WIDE_SLAB_SKIPS.append((_k, _extra))
        _skipped += _extra

VMEM_LIMIT = 56 * 1024 * 1024


def _cparams(*sem):
    return pltpu.CompilerParams(dimension_semantics=sem, vmem_limit_bytes=VMEM_LIMIT)


def _sigmoid(x):
    return jax.nn.sigmoid(x)


def _silu(x):
    return x * jax.nn.sigmoid(x)


def _mm(a, b):
    return jnp.dot(a.astype(BF16), b.astype(BF16), preferred_element_type=F32)


def _mm_nt(a, b):
    return lax.dot_general(a.astype(BF16), b.astype(BF16), (((1,), (1,)), ((), ())),
                           preferred_element_type=F32)


def _mm_tn(a, b):
    return lax.dot_general(a.astype(BF16), b.astype(BF16), (((0,), (0,)), ((), ())),
                           preferred_element_type=F32)


def _blk(idx, size):
    assert size & (size - 1) == 0
    return lax.shift_right_logical(idx, size.bit_length() - 1)


def _split3(x):
    x1 = x.astype(BF16)
    r1 = x - x1.astype(F32)
    x2 = r1.astype(BF16)
    r2 = r1 - x2.astype(F32)
    return x1, x2, r2.astype(BF16)


def _chunk_cumsum(g, chunk):
    rows = g.shape[0]
    r = lax.broadcasted_iota(jnp.int32, (2 * rows, rows), 0)
    s = lax.broadcasted_iota(jnp.int32, (2 * rows, rows), 1)
    rr = jnp.where(r >= rows, r - rows, r)
    same = _blk(rr, chunk) == _blk(s, chunk)
    sel = same & ((r >= rows) | (s <= rr))
    mat = jnp.where(sel, 1.0, 0.0).astype(BF16)
    acc = None
    for part in _split3(g):
        t = jnp.dot(mat, part, preferred_element_type=F32)
        acc = t if acc is None else acc + t
    return acc[:rows], acc[rows:]


def _causal_conv(x, prev, w, taps):
    rows = x.shape[0]
    xp = jnp.concatenate([prev, x], axis=0)
    acc = x * w[taps - 1:taps]
    for d in range(1, taps):
        acc = acc + xp[SUBLANES - d:SUBLANES - d + rows] * w[taps - 1 - d:taps - d]
    return acc


def _causal_conv_staged(x_ref, prev, w, taps, stage_ref):
    rows = x_ref.shape[0]
    x = x_ref[...]
    stage_ref[0:SUBLANES, :] = prev
    stage_ref[SUBLANES:, :] = x
    acc = x * w[taps - 1:taps]
    for d in range(1, taps):
        acc = acc + stage_ref[pl.ds(SUBLANES - d, rows), :] * w[taps - 1 - d:taps - d]
    return acc


MOD_TN = 1024
MOD_ROWS = 64


def _mod_kernel(c_ref, w_ref, b_ref, o_ref):
    d = c_ref.shape[0]

    def body(r, acc):
        rows = pl.ds(pl.multiple_of(r * MOD_ROWS, MOD_ROWS), MOD_ROWS)
        cc = _silu(c_ref[rows, :])
        prod = w_ref[0, rows, :] * cc
        return acc + jnp.sum(prod.reshape(MOD_ROWS // SUBLANES, SUBLANES, MOD_TN), axis=0)

    acc = lax.fori_loop(0, d // MOD_ROWS, body, jnp.zeros((SUBLANES, MOD_TN), F32))
    o_ref[0] = jnp.sum(acc, axis=0, keepdims=True) + b_ref[0]


def _modulation(c_col, ada_w, ada_b):
    depth, d, n = ada_w.shape
    out = pl.pallas_call(
        _mod_kernel,
        grid=(depth, n // MOD_TN),
        in_specs=[
            pl.BlockSpec((d, 1), lambda i, j: (0, 0)),
            pl.BlockSpec((1, d, MOD_TN), lambda i, j: (i, 0, j)),
            pl.BlockSpec((1, 1, MOD_TN), lambda i, j: (i, 0, j)),
        ],
        out_specs=pl.BlockSpec((1, 1, MOD_TN), lambda i, j: (i, 0, j)),
        out_shape=jax.ShapeDtypeStruct((depth, 1, n), F32),
        compiler_params=_cparams("parallel", "parallel"),
        name="adaln_mod",
    )(c_col, ada_w, ada_b.reshape(depth, 1, n))
    return out


ROPE_TB = 1024


def _rope_kernel(pos_ref, invf_ref, sign_ref, cos_ref, sin_ref):
    ang = pos_ref[...].astype(F32) * invf_ref[...]
    cos_ref[...] = jnp.cos(ang)
    sin_ref[...] = jnp.sin(ang) * sign_ref[...]


def _rope_tables(pos_col):
    seq = pos_col.shape[0]
    half = ATTN_HEAD_DIM // 2
    inv_freq = ROPE_THETA ** (-jnp.arange(0, ATTN_HEAD_DIM, 2, dtype=F32) / ATTN_HEAD_DIM)
    invf = jnp.tile(inv_freq, LANES // half).reshape(1, LANES)
    lane = jnp.arange(LANES)
    sign = jnp.where((lane % ATTN_HEAD_DIM) < half, -1.0, 1.0).astype(F32).reshape(1, LANES)
    tb = min(ROPE_TB, seq)
    return pl.pallas_call(
        _rope_kernel,
        grid=(seq // tb,),
        in_specs=[
            pl.BlockSpec((tb, 1), lambda i: (i, 0)),
            pl.BlockSpec((1, LANES), lambda i: (0, 0)),
            pl.BlockSpec((1, LANES), lambda i: (0, 0)),
        ],
        out_specs=[pl.BlockSpec((tb, LANES), lambda i: (i, 0))] * 2,
        out_shape=[jax.ShapeDtypeStruct((seq, LANES), F32)] * 2,
        compiler_params=_cparams("parallel"),
        name="rope_tables",
    )(pos_col, invf, sign)


NORM_ROWS = 128


def _prenorm_to(x_ref, nw_ref, sc_ref, sh_ref, h_ref):
    tm = x_ref.shape[0]

    def body(r, carry):
        rows = pl.ds(pl.multiple_of(r * NORM_ROWS, NORM_ROWS), NORM_ROWS)
        x = x_ref[rows, :]
        y = x * lax.rsqrt(jnp.mean(x * x, axis=-1, keepdims=True) + EPS) * nw_ref[...]
        h_ref[rows, :] = (y * (1.0 + sc_ref[...]) + sh_ref[...]).astype(BF16)
        return carry

    lax.fori_loop(0, tm // NORM_ROWS, body, 0)


INPROJ_TM = 1024

MOD_SHIFT_A, MOD_SCALE_A, MOD_GATE_A, MOD_SHIFT_M, MOD_SCALE_M, MOD_GATE_M = range(6)


def _layer_spec(shape, layer, *rest):
    rest = rest or (0,) * len(shape)

    def index_map(*grid_idx):
        return (layer,) + tuple(r(*grid_idx) if callable(r) else r for r in rest)

    return pl.BlockSpec((None,) + tuple(shape), index_map)


def _inproj_kernel(x_ref, nw_ref, sc_ref, sh_ref, wt_ref, tail_ref, o_ref, h_ref):
    j = pl.program_id(1)

    @pl.when(j == 0)
    def _():
        _prenorm_to(x_ref, nw_ref, sc_ref, sh_ref, h_ref)

    def project(w_ref):
        o_ref[...] = _mm_nt(h_ref[...], w_ref[...])

    @pl.when(j < N_WIDE_SLABS)
    def _():
        project(wt_ref)

    @pl.when(j == N_WIDE_SLABS)
    def _():
        project(tail_ref)


def _wide_slab_row(j):
    j = jnp.minimum(j, N_WIDE_SLABS - 1)
    group = j * (GW // SUBLANES)
    for first_slab, skipped in WIDE_SLAB_SKIPS:
        group = group + jnp.where(j >= first_slab, skipped // SUBLANES, 0)
    return group * SUBLANES


def _in_projection(x, layer, norm_w, mod, w_t, w_tail):
    seq, d = x.shape
    tm = min(INPROJ_TM, seq)
    return pl.pallas_call(
        _inproj_kernel,
        grid=(seq // tm, IN_COLS // GW),
        in_specs=[
            pl.BlockSpec((tm, d), lambda i, j: (i, 0)),
            _layer_spec((1, d), layer),
            _layer_spec((1, d), layer, 0, MOD_SCALE_A),
            _layer_spec((1, d), layer, 0, MOD_SHIFT_A),
            pl.BlockSpec((None, pl.Element(GW), pl.Element(d)), lambda i, j: (layer, _wide_slab_row(j), 0)),
            _layer_spec((GW, d), layer),
        ],
        out_specs=pl.BlockSpec((tm, GW), lambda i, j: (i, j)),
        out_shape=jax.ShapeDtypeStruct((seq, IN_COLS), F32),
        scratch_shapes=[pltpu.VMEM((tm, d), BF16)],
        compiler_params=_cparams("parallel", "arbitrary"),
        name="in_proj",
    )(x, norm_w, mod, mod, w_t, w_tail)


GDN_TB = 256


def _l2norm(x):
    return x * lax.rsqrt(jnp.sum(x * x, axis=-1, keepdims=True) + EPS)


GDN_BLK = 128


def _unit_lower_inverse_minus_identity(ms, r, s):
    same8 = _blk(r, 8) == _blk(s, 8)
    same16 = _blk(r, 16) == _blk(s, 16)
    same32 = _blk(r, 32) == _blk(s, 32)
    n1 = [jnp.where(same8, -m, 0.0) for m in ms]
    n2 = [_mm(a, a) for a in n1]
    n3 = [_mm(a, b) for a, b in zip(n1, n2)]
    n4 = [_mm(b, b) for b in n2]
    ys = [a + b + c for a, b, c in zip(n1, n2, n3)]
    ts = [_mm(y, d) for y, d in zip(ys, n4)]
    ys = [y + d + t for y, d, t in zip(ys, n4, ts)]
    for sel in (same16 & ~same8, same32 & ~same16, ~same32):
        blks = [jnp.where(sel, m, 0.0) for m in ms]
        cs = [b + _mm(y, b) for y, b in zip(ys, blks)]
        ts = [_mm(c, y) for c, y in zip(cs, ys)]
        ys = [y - c - t for y, c, t in zip(ys, cs, ts)]
    return ys


def _gdn_kernel(q_ref, k_ref, v_ref, z_ref, qp_ref, kp_ref, vp_ref, sm_ref, cw_ref, lp_ref, nw_ref,
                o_ref, state_ref, conv_ref):
    step = pl.program_id(0)
    tb = q_ref.shape[0]
    hd = GDN_HEAD_DIM
    nblk = tb // GDN_BLK
    chunks_per_blk = GDN_BLK // GDN_CHUNK

    @pl.when(step == 0)
    def _():
        state_ref[...] = jnp.zeros_like(state_ref)

    keep_prev = jnp.where(step == 0, 0.0, 1.0)
    cw = cw_ref[...]

    def conv_silu(x_ref, p_ref, idx):
        w = cw[:, idx * GW:(idx + 1) * GW]
        return _silu(_causal_conv_staged(x_ref, p_ref[...] * keep_prev, w, 4, conv_ref.at[idx]))

    q = conv_silu(q_ref, qp_ref, 0)
    k = conv_silu(k_ref, kp_ref, 1)
    v = conv_silu(v_ref, vp_ref, 2)
    z = z_ref[...]

    small = sm_ref[...]
    a_log = lp_ref[0:1, :]
    dt_bias = lp_ref[1:2, :]
    beta_all = _sigmoid(small)
    g_all = -jnp.exp(a_log) * jax.nn.softplus(small + dt_bias)
    gc_all, gtot_all = _chunk_cumsum(g_all, GDN_CHUNK)
    gc_t = gc_all.T

    r = lax.broadcasted_iota(jnp.int32, (GDN_BLK, GDN_BLK), 0)
    s = lax.broadcasted_iota(jnp.int32, (GDN_BLK, GDN_BLK), 1)
    same_chunk = _blk(r, GDN_CHUNK) == _blk(s, GDN_CHUNK)
    mask_incl = same_chunk & (s <= r)
    mask_strict = same_chunk & (s < r)

    problems = [(slice(b * GDN_BLK, (b + 1) * GDN_BLK), h) for b in range(nblk) for h in range(GDN_HEADS)]
    head_cols = lambda h: slice(h * hd, (h + 1) * hd)
    gate = lambda arr, rows, lane: arr[rows, lane:lane + 1]

    qn = [_l2norm(q[rows, head_cols(h)]) * (hd ** -0.5) for rows, h in problems]
    kn = [_l2norm(k[rows, head_cols(h)]) for rows, h in problems]
    beta = [gate(beta_all, rows, LANE_BETA + h) for rows, h in problems]
    gcol = [gate(gc_all, rows, LANE_ALPHA + h) for rows, h in problems]
    gtot = [gate(gtot_all, rows, LANE_ALPHA + h) for rows, h in problems]
    grow = [gc_t[LANE_ALPHA + h:LANE_ALPHA + h + 1, rows] for rows, h in problems]

    decay = [jnp.exp(jnp.where(mask_incl, gc - gr, -jnp.inf)) for gc, gr in zip(gcol, grow)]
    k_beta = [kk * b for kk, b in zip(kn, beta)]
    k16 = [kk.astype(BF16) for kk in kn]
    ms = [jnp.where(mask_strict, _mm_nt(kb, kk) * d, 0.0) for kb, kk, d in zip(k_beta, k16, decay)]
    attn = [_mm_nt(qq, kk) * d for qq, kk, d in zip(qn, k16, decay)]
    ys = _unit_lower_inverse_minus_identity(ms, r, s)

    e_gc = [jnp.exp(gc) for gc in gcol]
    rhs = [jnp.concatenate([v[rows, head_cols(h)] * b, kb * e], axis=1)
           for (rows, h), b, kb, e in zip(problems, beta, k_beta, e_gc)]
    uw = [x + _mm(y, x) for y, x in zip(ys, rhs)]
    q_dec = [(qq * e).astype(BF16) for qq, e in zip(qn, e_gc)]
    k_dec = [(kk * jnp.exp(gt - gc)).astype(BF16) for kk, gt, gc in zip(kn, gtot, gcol)]
    chunk_dec = [jnp.exp(gt) for gt in gtot]

    states = [state_ref[h] for h in range(GDN_HEADS)]
    v_new = [[] for _ in problems]
    o_inter = [[] for _ in problems]
    for b in range(nblk):
        for c in range(chunks_per_blk):
            rows = slice(c * GDN_CHUNK, (c + 1) * GDN_CHUNK)
            for h in range(GDN_HEADS):
                p = b * GDN_HEADS + h
                st16 = states[h].astype(BF16)
                vn = uw[p][rows, :hd] - jnp.dot(uw[p][rows, hd:].astype(BF16), st16, preferred_element_type=F32)
                o_inter[p].append(jnp.dot(q_dec[p][rows], st16, preferred_element_type=F32))
                states[h] = (states[h] * chunk_dec[p][c * GDN_CHUNK:c * GDN_CHUNK + 1, :]
                             + _mm_tn(k_dec[p][rows], vn))
                v_new[p].append(vn)
    for h in range(GDN_HEADS):
        state_ref[h] = states[h]

    for p, (rows, h) in enumerate(problems):
        o = jnp.concatenate(o_inter[p], axis=0) + _mm(attn[p], jnp.concatenate(v_new[p], axis=0))
        o = o * lax.rsqrt(jnp.mean(o * o, axis=-1, keepdims=True) + EPS) * nw_ref[...]
        o_ref[rows, head_cols(h)] = (o * _silu(z[rows, head_cols(h)])).astype(o_ref.dtype)


def _prev_rows_spec(tb, width, slab):
    blocks = tb // SUBLANES
    return pl.BlockSpec((SUBLANES, width), lambda i: (jnp.maximum(i * blocks - 1, 0), slab))


def _lane_params(lane0, *rows):
    stacked = jnp.stack([r.astype(F32) for r in rows], axis=1)
    nrows, heads = stacked.shape[1:]
    return jnp.pad(stacked, ((0, 0), (0, SUBLANES - nrows), (lane0, LANES - lane0 - heads)))


def _gated_deltanet(proj, layer, conv_w, lane_params, norm_w):
    seq = proj.shape[0]
    tb = min(GDN_TB, seq)
    slab = lambda j: pl.BlockSpec((tb, GW), lambda i: (i, j))
    full = lambda shape: _layer_spec(shape, layer)
    return pl.pallas_call(
        _gdn_kernel,
        grid=(seq // tb,),
        in_specs=[
            slab(SLAB_GQ), slab(SLAB_GK), slab(SLAB_GV), slab(SLAB_GZ),
            _prev_rows_spec(tb, GW, SLAB_GQ), _prev_rows_spec(tb, GW, SLAB_GK), _prev_rows_spec(tb, GW, SLAB_GV),
            pl.BlockSpec((tb, LANES), lambda i: (i, SLAB128_SMALL)),
            full((4, 3 * GW)), full((SUBLANES, LANES)), full((1, GDN_HEAD_DIM)),
        ],
        out_specs=pl.BlockSpec((tb, GW), lambda i: (i, 0)),
        out_shape=jax.ShapeDtypeStruct((seq, GW), BF16),
        scratch_shapes=[pltpu.VMEM((GDN_HEADS, GDN_HEAD_DIM, GDN_HEAD_DIM), F32),
                        pltpu.VMEM((3, tb + SUBLANES, GW), F32)],
        compiler_params=_cparams("arbitrary"),
        name="gated_deltanet",
    )(proj, proj, proj, proj, proj, proj, proj, proj, conv_w, lane_params, norm_w)


SSD_TB = 256


def _ssd_kernel(z_ref, x_ref, bc_ref, xp_ref, bcp_ref, sm_ref, cw_ref, cb_ref, lp_ref, nw_ref,
                o_ref, state_ref):
    step = pl.program_id(0)
    tb = z_ref.shape[0]
    nstate = SSM_STATE
    nchunks = tb // SSM_CHUNK
    pairs = SSM_HEADS // 2
    pairs_per_group = pairs // 2

    @pl.when(step == 0)
    def _():
        state_ref[...] = jnp.zeros_like(state_ref)

    keep_prev = jnp.where(step == 0, 0.0, 1.0)
    cw = cw_ref[...]
    cbias = cb_ref[...]
    xs = _silu(_causal_conv(x_ref[...], xp_ref[...] * keep_prev, cw[:, :GW], 4) + cbias[:, :GW])
    bc = _silu(_causal_conv(bc_ref[...], bcp_ref[...] * keep_prev, cw[:, GW:], 4) + cbias[:, GW:])
    z = z_ref[...]

    small = sm_ref[...]
    a_neg = -jnp.exp(lp_ref[0:1, :])
    dt_all = jax.nn.softplus(small + lp_ref[1:2, :])
    dskip_all = lp_ref[2:3, :]
    gc_all, gtot_all = _chunk_cumsum(dt_all * a_neg, SSM_CHUNK)
    gc_t = gc_all.T

    r = lax.broadcasted_iota(jnp.int32, (tb, tb), 0)
    s = lax.broadcasted_iota(jnp.int32, (tb, tb), 1)
    mask_incl = (_blk(r, SSM_CHUNK) == _blk(s, SSM_CHUNK)) & (s <= r)
    lane = lax.broadcasted_iota(jnp.int32, (1, LANES), 1)
    first_half = lane < SSM_HEAD_DIM

    for grp in range(2):
        b_g = bc[:, grp * nstate:(grp + 1) * nstate]
        c_g = bc[:, 2 * nstate + grp * nstate:2 * nstate + (grp + 1) * nstate]
        cb = _mm_nt(c_g, b_g)
        ys, sumsq = [], None
        for pp in range(pairs_per_group):
            pair = grp * pairs_per_group + pp
            cols = slice(pair * LANES, (pair + 1) * LANES)
            x_pair = xs[:, cols]
            heads = (2 * pair, 2 * pair + 1)
            col = [l for l in (LANE_DT + heads[0], LANE_DT + heads[1])]
            dt_lane = jnp.where(first_half, dt_all[:, col[0]:col[0] + 1], dt_all[:, col[1]:col[1] + 1])
            dskip_lane = jnp.where(first_half, dskip_all[:, col[0]:col[0] + 1], dskip_all[:, col[1]:col[1] + 1])
            xdt = x_pair * dt_lane
            xdt_half = (jnp.where(first_half, xdt, 0.0), jnp.where(first_half, 0.0, xdt))

            y_diag = None
            c_dec, b_dec, cd = [], [], []
            for hh in range(2):
                gcol = gc_all[:, col[hh]:col[hh] + 1]
                grow = gc_t[col[hh]:col[hh] + 1, :]
                gtot = gtot_all[:, col[hh]:col[hh] + 1]
                lmat = jnp.exp(jnp.where(mask_incl, gcol - grow, -jnp.inf))
                t = _mm(cb * lmat, xdt_half[hh])
                y_diag = t if y_diag is None else y_diag + t
                c_dec.append(c_g * jnp.exp(gcol))
                b_dec.append(b_g * jnp.exp(gtot - gcol))
                cd.append(jnp.exp(gtot))

            state = state_ref[pair]
            y_off = []
            for c in range(nchunks):
                rows = slice(c * SSM_CHUNK, (c + 1) * SSM_CHUNK)
                y_off.append(jnp.where(first_half, _mm(c_dec[0][rows], state), _mm(c_dec[1][rows], state)))
                cd_lane = jnp.where(first_half, cd[0][c * SSM_CHUNK:c * SSM_CHUNK + 1, :],
                                    cd[1][c * SSM_CHUNK:c * SSM_CHUNK + 1, :])
                upd = jnp.where(first_half, _mm_tn(b_dec[0][rows], xdt[rows]), _mm_tn(b_dec[1][rows], xdt[rows]))
                state = state * cd_lane + upd
            state_ref[pair] = state

            y = y_diag + jnp.concatenate(y_off, axis=0) + x_pair * dskip_lane
            y = y * _silu(z[:, cols])
            ss = jnp.sum(y * y, axis=-1, keepdims=True)
            sumsq = ss if sumsq is None else sumsq + ss
            ys.append((cols, y))
        inv = lax.rsqrt(sumsq / (pairs_per_group * LANES) + EPS)
        for cols, y in ys:
            o_ref[:, cols] = (y * inv * nw_ref[:, cols]).astype(o_ref.dtype)


def _mamba2_ssd(proj, layer, conv_w, conv_b, lane_params, norm_w):
    seq = proj.shape[0]
    tb = min(SSD_TB, seq)
    slab = lambda j: pl.BlockSpec((tb, GW), lambda i: (i, j))
    full = lambda shape: _layer_spec(shape, layer)
    return pl.pallas_call(
        _ssd_kernel,
        grid=(seq // tb,),
        in_specs=[
            slab(SLAB_SZ), slab(SLAB_SX), slab(SLAB_SBC),
            _prev_rows_spec(tb, GW, SLAB_SX), _prev_rows_spec(tb, GW, SLAB_SBC),
            pl.BlockSpec((tb, LANES), lambda i: (i, SLAB128_SMALL)),
            full((4, 2 * GW)), full((1, 2 * GW)), full((SUBLANES, LANES)), full((1, GW)),
        ],
        out_specs=pl.BlockSpec((tb, GW), lambda i: (i, 0)),
        out_shape=jax.ShapeDtypeStruct((seq, GW), BF16),
        scratch_shapes=[pltpu.VMEM((SSM_HEADS // 2, SSM_STATE, LANES), F32)],
        compiler_params=_cparams("arbitrary"),
        name="mamba2_ssd",
    )(proj, proj, proj, proj, proj, proj, conv_w, conv_b, lane_params, norm_w)


def _rotate_half(x, lane_in_head_low):
    width = x.shape[1]
    half = ATTN_HEAD_DIM // 2
    ahead = pltpu.roll(x, width - half, axis=1)
    behind = pltpu.roll(x, half, axis=1)
    return jnp.where(lane_in_head_low, ahead, behind)


SWA_TB = 256


def _swa_kernel(q_ref, k_ref, v_ref, cos_ref, sin_ref, sink_ref, o_ref, kprev_ref, vprev_ref):
    step = pl.program_id(0)
    tb = q_ref.shape[0]
    win = WINDOW
    hd = ATTN_HEAD_DIM

    @pl.when(step == 0)
    def _():
        kprev_ref[...] = jnp.zeros_like(kprev_ref)
        vprev_ref[...] = jnp.zeros_like(vprev_ref)

    cos = cos_ref[...]
    sin = sin_ref[...]
    lane = lax.broadcasted_iota(jnp.int32, (1, LANES), 1)
    low = (lane & (hd - 1)) < (hd // 2)
    first_half = lane < hd

    k_cur = k_ref[...]
    k_cur = k_cur * cos + _rotate_half(k_cur, low) * sin
    v_cur = v_ref[...]
    k_all = jnp.concatenate([kprev_ref[...], k_cur], axis=0)
    v_all = jnp.concatenate([vprev_ref[...], v_cur], axis=0)
    k_swap = pltpu.roll(k_all, hd, axis=1)
    v_swap = pltpu.roll(v_all, hd, axis=1)

    def placed(x, x_swap, kv_head, pos):
        src = x if kv_head == pos else x_swap
        keep = first_half if pos == 0 else ~first_half
        return jnp.where(keep, src, 0.0).astype(BF16)

    qi = lax.broadcasted_iota(jnp.int32, (win, 2 * win), 0)
    kj = lax.broadcasted_iota(jnp.int32, (win, 2 * win), 1)
    rel = qi + win - kj
    band = (rel >= 0) & (rel < win)
    band_first = band & ((kj >= win) | (step > 0))

    problems = [(j, h) for j in range(tb // win) for h in range(ATTN_Q_HEADS)]
    kv_of = lambda head: head // (ATTN_Q_HEADS // 2)
    pair_cols = lambda pair: slice(pair * LANES, (pair + 1) * LANES)
    k_at = {(g, pos): placed(k_all, k_swap, g, pos) for g in range(2) for pos in range(2)}
    v_at = {(g, pos): placed(v_all, v_swap, g, pos) for g in range(2) for pos in range(2)}
    q_pairs = []
    for pair in range(ATTN_Q_HEADS // 2):
        q_pair = q_ref[:, pair_cols(pair)]
        q_pairs.append((q_pair * cos + _rotate_half(q_pair, low) * sin).astype(BF16))
    sinks = [sink_ref[h:h + 1, 0:1] for _, h in problems]
    sc = [jnp.where(band_first if j == 0 else band,
                    _mm_nt(q_pairs[h // 2][j * win:(j + 1) * win],
                           k_at[kv_of(h), h % 2][j * win:(j + 2) * win]) * (hd ** -0.5),
                    -jnp.inf)
          for j, h in problems]
    mx = [jnp.maximum(jnp.max(s_, axis=-1, keepdims=True), sk) for s_, sk in zip(sc, sinks)]
    p = [jnp.exp(s_ - m_) for s_, m_ in zip(sc, mx)]
    denom = [jnp.sum(p_, axis=-1, keepdims=True) + jnp.exp(sk - m_) for p_, sk, m_ in zip(p, sinks, mx)]
    o = [_mm(p_, v_at[kv_of(h), h % 2][j * win:(j + 2) * win]) / d_
         for (j, h), p_, d_ in zip(problems, p, denom)]
    for idx in range(0, len(problems), 2):
        j, h = problems[idx]
        o_ref[j * win:(j + 1) * win, pair_cols(h // 2)] = (o[idx] + o[idx + 1]).astype(o_ref.dtype)

    kprev_ref[...] = k_cur[tb - win:]
    vprev_ref[...] = v_cur[tb - win:]


def _sliding_window_attention(proj, layer, cos, sin, sink_rows):
    seq = proj.shape[0]
    tb = min(SWA_TB, seq)
    return pl.pallas_call(
        _swa_kernel,
        grid=(seq // tb,),
        in_specs=[
            pl.BlockSpec((tb, GW), lambda i: (i, SLAB_AQ)),
            pl.BlockSpec((tb, LANES), lambda i: (i, SLAB128_AK)),
            pl.BlockSpec((tb, LANES), lambda i: (i, SLAB128_AV)),
            pl.BlockSpec((tb, LANES), lambda i: (i, 0)),
            pl.BlockSpec((tb, LANES), lambda i: (i, 0)),
            _layer_spec((ATTN_Q_HEADS, LANES), layer),
        ],
        out_specs=pl.BlockSpec((tb, GW), lambda i: (i, 0)),
        out_shape=jax.ShapeDtypeStruct((seq, GW), BF16),
        scratch_shapes=[pltpu.VMEM((WINDOW, LANES), F32), pltpu.VMEM((WINDOW, LANES), F32)],
        compiler_params=_cparams("arbitrary"),
        name="sliding_window_attn",
    )(proj, proj, proj, cos, sin, sink_rows)


SCONV_TB = 512


def _sconv_kernel(b_ref, c_ref, h_ref, cp_ref, hp_ref, w_ref, o_ref):
    keep_prev = jnp.where(pl.program_id(0) == 0, 0.0, 1.0)
    ch = c_ref[...] * h_ref[...]
    ch_prev = cp_ref[...] * hp_ref[...] * keep_prev
    o_ref[...] = (b_ref[...] * _causal_conv(ch, ch_prev, w_ref[...], 3)).astype(o_ref.dtype)


def _short_conv(proj, layer, conv_w):
    seq = proj.shape[0]
    tb = min(SCONV_TB, seq)
    slab = lambda j: pl.BlockSpec((tb, GW), lambda i: (i, j))
    return pl.pallas_call(
        _sconv_kernel,
        grid=(seq // tb,),
        in_specs=[
            slab(SLAB_CB), slab(SLAB_CC), slab(SLAB_CH),
            _prev_rows_spec(tb, GW, SLAB_CC), _prev_rows_spec(tb, GW, SLAB_CH),
            _layer_spec((3, GW), layer),
        ],
        out_specs=pl.BlockSpec((tb, GW), lambda i: (i, 0)),
        out_shape=jax.ShapeDtypeStruct((seq, GW), BF16),
        compiler_params=_cparams("parallel"),
        name="short_conv",
    )(proj, proj, proj, proj, proj, conv_w)


OUTPROJ_TM = 512


def _outproj_kernel(ya_ref, yb_ref, yc_ref, yd_ref, w_ref, x_ref, gate_ref, nw_ref, o_ref):
    y = jnp.concatenate([ya_ref[...], yb_ref[...], yc_ref[...], yd_ref[...]], axis=1)
    y = jnp.dot(y, w_ref[...], preferred_element_type=F32)
    y = y * lax.rsqrt(jnp.mean(y * y, axis=-1, keepdims=True) + EPS) * nw_ref[...]
    o_ref[...] = x_ref[...] + gate_ref[...] * y


def _out_projection(ys, layer, w_bf16, x, mod, norm_w):
    seq, d = x.shape
    tm = min(OUTPROJ_TM, seq)
    return pl.pallas_call(
        _outproj_kernel,
        grid=(seq // tm,),
        in_specs=[pl.BlockSpec((tm, GW), lambda i: (i, 0))] * 4 + [
            _layer_spec((4 * GW, d), layer),
            pl.BlockSpec((tm, d), lambda i: (i, 0)),
            _layer_spec((1, d), layer, 0, MOD_GATE_A),
            _layer_spec((1, d), layer),
        ],
        out_specs=pl.BlockSpec((tm, d), lambda i: (i, 0)),
        out_shape=jax.ShapeDtypeStruct((seq, d), F32),
        compiler_params=_cparams("parallel"),
        name="out_proj",
    )(*ys, w_bf16, x, mod, norm_w)


MLP_TM = 1024
MLP_TH = 512


def _mlp_kernel(x_ref, nw_ref, sc_ref, sh_ref, wu_ref, wd_ref, gate_ref, pw_ref, o_ref, h_ref):
    j = pl.program_id(1)

    @pl.when(j == 0)
    def _():
        _prenorm_to(x_ref, nw_ref, sc_ref, sh_ref, h_ref)
        o_ref[...] = jnp.zeros_like(o_ref)

    hid = jnp.dot(h_ref[...], wu_ref[...], preferred_element_type=F32)
    hid = jnp.square(jnp.maximum(hid, 0.0)).astype(BF16)
    o_ref[...] += jnp.dot(hid, wd_ref[...], preferred_element_type=F32)

    @pl.when(j == pl.num_programs(1) - 1)
    def _():
        def body(r, carry):
            rows = pl.ds(pl.multiple_of(r * NORM_ROWS, NORM_ROWS), NORM_ROWS)
            y = o_ref[rows, :]
            y = y * lax.rsqrt(jnp.mean(y * y, axis=-1, keepdims=True) + EPS) * pw_ref[...]
            o_ref[rows, :] = x_ref[rows, :] + gate_ref[...] * y
            return carry

        lax.fori_loop(0, x_ref.shape[0] // NORM_ROWS, body, 0)


def _mlp(x, layer, norm_w, mod, wu_bf16, wd_bf16, post_w):
    seq, d = x.shape
    hidden = wu_bf16.shape[2]
    tm = min(MLP_TM, seq)
    return pl.pallas_call(
        _mlp_kernel,
        grid=(seq // tm, hidden // MLP_TH),
        in_specs=[
            pl.BlockSpec((tm, d), lambda i, j: (i, 0)),
            _layer_spec((1, d), layer),
            _layer_spec((1, d), layer, 0, MOD_SCALE_M),
            _layer_spec((1, d), layer, 0, MOD_SHIFT_M),
            _layer_spec((d, MLP_TH), layer, 0, lambda i, j: j),
            _layer_spec((MLP_TH, d), layer, lambda i, j: j, 0),
            _layer_spec((1, d), layer, 0, MOD_GATE_M),
            _layer_spec((1, d), layer),
        ],
        out_specs=pl.BlockSpec((tm, d), lambda i, j: (i, 0)),
        out_shape=jax.ShapeDtypeStruct((seq, d), F32),
        scratch_shapes=[pltpu.VMEM((tm, d), BF16)],
        compiler_params=_cparams("parallel", "arbitrary"),
        name="mlp",
    )(x, norm_w, mod, mod, wu_bf16, wd_bf16, mod, post_w)


def _tail_slab(w_t):
    parts = [w_t[:, IN_OFFSET[name][0]:IN_OFFSET[name][1]] for name in ("ak", "av", "gb", "ga", "sdt")]
    used = sum(p.shape[1] for p in parts)
    parts.append(jnp.zeros((w_t.shape[0], GW - used, w_t.shape[2]), w_t.dtype))
    return jnp.concatenate(parts, axis=1)


def kernel(x, c, positions, ada_w, ada_b, norm_pre_mix, norm_post_mix, norm_pre_mlp, norm_post_mlp,
           w_in, w_out, gdn_conv_w, gdn_a_log, gdn_dt_bias, gdn_norm_w, ssm_conv_w, ssm_conv_b,
           ssm_a_log, ssm_dt_bias, ssm_d, ssm_norm_w, attn_sinks, sc_conv_w, w_up, w_down):
    batch, seq, d = x.shape
    assert batch == 1 and d == D_MODEL
    depth = ada_w.shape[0]
    xs = x.reshape(seq, d)

    mod = _modulation(c.reshape(d, 1), ada_w, ada_b)
    cos, sin = _rope_tables(positions.reshape(seq, 1))

    rows = lambda v: v.reshape(depth, 1, -1)
    w_in_t = jnp.swapaxes(w_in, 1, 2)
    w_in_tail = _tail_slab(w_in_t)
    w_out_bf16, w_up_bf16, w_down_bf16 = w_out.astype(BF16), w_up.astype(BF16), w_down.astype(BF16)
    gdn_lanes = _lane_params(LANE_ALPHA, gdn_a_log, gdn_dt_bias)
    ssm_lanes = _lane_params(LANE_DT, ssm_a_log, ssm_dt_bias, ssm_d)
    sink_rows = jnp.broadcast_to(attn_sinks.astype(F32)[:, :, None], (depth, ATTN_Q_HEADS, LANES))

    for i in range(depth):
        proj = _in_projection(xs, i, rows(norm_pre_mix), mod, w_in_t, w_in_tail)
        y_a = _gated_deltanet(proj, i, gdn_conv_w, gdn_lanes, rows(gdn_norm_w))
        y_b = _mamba2_ssd(proj, i, ssm_conv_w, rows(ssm_conv_b), ssm_lanes, rows(ssm_norm_w))
        y_c = _sliding_window_attention(proj, i, cos, sin, sink_rows)
        y_d = _short_conv(proj, i, sc_conv_w)
        xs = _out_projection((y_a, y_b, y_c, y_d), i, w_out_bf16, xs, mod, rows(norm_post_mix))
        xs = _mlp(xs, i, rows(norm_pre_mlp), mod, w_up_bf16, w_down_bf16, rows(norm_post_mlp))
    return xs.reshape(batch, seq, d)
```

```python
import functools

import jax
import jax.numpy as jnp
from jax import lax
from jax.experimental import pallas as pl
from jax.experimental.pallas import tpu as pltpu

F32 = jnp.float32
BF16 = jnp.bfloat16

D_MODEL = 2048
GW = 512
GDN_HEADS, GDN_HEAD_DIM, GDN_CHUNK = 4, 128, 64
SSM_HEADS, SSM_HEAD_DIM, SSM_STATE, SSM_CHUNK = 8, 64, 128, 128
ATTN_HEAD_DIM, ATTN_Q_HEADS, WINDOW = 64, 8, 128
ROPE_THETA = 10000.0
MLP_HIDDEN = 4 * D_MODEL
EPS = 1e-6

SUBLANES = 8
LANES = 128

N_WIDE_SLABS = 11
IN_COLS = (N_WIDE_SLABS + 1) * GW
SLAB_GQ, SLAB_GK, SLAB_GV, SLAB_GZ, SLAB_SZ, SLAB_SX, SLAB_SBC, SLAB_AQ, SLAB_CB, SLAB_CC, SLAB_CH = range(11)
SLAB128_AK, SLAB128_AV, SLAB128_SMALL = 44, 45, 46
LANE_BETA, LANE_ALPHA, LANE_DT = 0, 4, 8

_IN_SEGMENTS = (("gq", GW), ("gk", GW), ("gv", GW), ("gz", GW), ("gb", GDN_HEADS), ("ga", GDN_HEADS),
                ("sz", GW), ("sx", GW), ("sbc", 4 * SSM_STATE), ("sdt", SSM_HEADS),
                ("aq", GW), ("ak", 2 * ATTN_HEAD_DIM), ("av", 2 * ATTN_HEAD_DIM),
                ("cb", GW), ("cc", GW), ("ch", GW))
IN_OFFSET = {}
_off = 0
for _name, _size in _IN_SEGMENTS:
    IN_OFFSET[_name] = (_off, _off + _size)
    _off += _size
IN_WIDTH = _off
_WIDE_ORDER = ("gq", "gk", "gv", "gz", "sz", "sx", "sbc", "aq", "cb", "cc", "ch")
WIDE_SLAB_SKIPS = []
_skipped = 0
for _k, _name in enumerate(_WIDE_ORDER):
    _extra = IN_OFFSET[_name][0] - _k * GW - _skipped
    assert _extra >= 0 and IN_OFFSET[_name][0] % SUBLANES == 0
    if _extra:
        WIDE_SLAB_SKIPS.append((_k, _extra))
        _skipped += _extra

MIB = 1024 * 1024
V7X_VMEM_BYTES = 64 * MIB
VMEM_LIMIT = 56 * MIB


def _cparams(*sem, vmem_limit=VMEM_LIMIT):
    assert vmem_limit < V7X_VMEM_BYTES
    return pltpu.CompilerParams(dimension_semantics=sem, vmem_limit_bytes=vmem_limit)


def _sigmoid(x):
    return jax.nn.sigmoid(x)


def _silu(x):
    return x * jax.nn.sigmoid(x)


def _mm(a, b):
    return jnp.dot(a.astype(BF16), b.astype(BF16), preferred_element_type=F32)


def _mm_nt(a, b):
    return lax.dot_general(a.astype(BF16), b.astype(BF16), (((1,), (1,)), ((), ())),
                           preferred_element_type=F32)


def _mm_tn(a, b):
    return lax.dot_general(a.astype(BF16), b.astype(BF16), (((0,), (0,)), ((), ())),
                           preferred_element_type=F32)


def _blk(idx, size):
    assert size & (size - 1) == 0
    return lax.shift_right_logical(idx, size.bit_length() - 1)


def _split3(x):
    x1 = x.astype(BF16)
    r1 = x - x1.astype(F32)
    x2 = r1.astype(BF16)
    r2 = r1 - x2.astype(F32)
    return x1, x2, r2.astype(BF16)


def _chunk_cumsum(g, chunk):
    rows = g.shape[0]
    r = lax.broadcasted_iota(jnp.int32, (2 * rows, rows), 0)
    s = lax.broadcasted_iota(jnp.int32, (2 * rows, rows), 1)
    rr = jnp.where(r >= rows, r - rows, r)
    same = _blk(rr, chunk) == _blk(s, chunk)
    sel = same & ((r >= rows) | (s <= rr))
    mat = jnp.where(sel, 1.0, 0.0).astype(BF16)
    acc = None
    for part in _split3(g):
        t = jnp.dot(mat, part, preferred_element_type=F32)
        acc = t if acc is None else acc + t
    return acc[:rows], acc[rows:]


def _causal_conv(x, prev, w, taps):
    rows = x.shape[0]
    xp = jnp.concatenate([prev, x], axis=0)
    acc = x * w[taps - 1:taps]
    for d in range(1, taps):
        acc = acc + xp[SUBLANES - d:SUBLANES - d + rows] * w[taps - 1 - d:taps - d]
    return acc


def _causal_conv_staged(x_ref, prev, w, taps, stage_ref):
    rows = x_ref.shape[0]
    x = x_ref[...]
    stage_ref[0:SUBLANES, :] = prev
    stage_ref[SUBLANES:, :] = x
    acc = x * w[taps - 1:taps]
    for d in range(1, taps):
        acc = acc + stage_ref[pl.ds(SUBLANES - d, rows), :] * w[taps - 1 - d:taps - d]
    return acc


MOD_TN = 2048
MOD_ROWS = 64


def _mod_kernel(c_ref, w_ref, b_ref, o_ref):
    d = c_ref.shape[0]

    def body(r, acc):
        rows = pl.ds(pl.multiple_of(r * MOD_ROWS, MOD_ROWS), MOD_ROWS)
        cc = _silu(c_ref[rows, :])
        prod = w_ref[0, rows, :] * cc
        return acc + jnp.sum(prod.reshape(MOD_ROWS // SUBLANES, SUBLANES, MOD_TN), axis=0)

    acc = lax.fori_loop(0, d // MOD_ROWS, body, jnp.zeros((SUBLANES, MOD_TN), F32))
    o_ref[0] = jnp.sum(acc, axis=0, keepdims=True) + b_ref[0]


def _modulation(c_col, ada_w, ada_b):
    depth, d, n = ada_w.shape
    out = pl.pallas_call(
        _mod_kernel,
        grid=(depth, n // MOD_TN),
        in_specs=[
            pl.BlockSpec((d, 1), lambda i, j: (0, 0)),
            pl.BlockSpec((1, d, MOD_TN), lambda i, j: (i, 0, j)),
            pl.BlockSpec((1, 1, MOD_TN), lambda i, j: (i, 0, j)),
        ],
        out_specs=pl.BlockSpec((1, 1, MOD_TN), lambda i, j: (i, 0, j)),
        out_shape=jax.ShapeDtypeStruct((depth, 1, n), F32),
        compiler_params=_cparams("parallel", "parallel"),
        name="adaln_mod",
    )(c_col, ada_w, ada_b.reshape(depth, 1, n))
    return out


ROPE_TB = 1024


def _rope_kernel(pos_ref, invf_ref, sign_ref, cos_ref, sin_ref):
    ang = pos_ref[...].astype(F32) * invf_ref[...]
    cos_ref[...] = jnp.cos(ang)
    sin_ref[...] = jnp.sin(ang) * sign_ref[...]


def _rope_tables(pos_col):
    seq = pos_col.shape[0]
    half = ATTN_HEAD_DIM // 2
    inv_freq = ROPE_THETA ** (-jnp.arange(0, ATTN_HEAD_DIM, 2, dtype=F32) / ATTN_HEAD_DIM)
    invf = jnp.tile(inv_freq, LANES // half).reshape(1, LANES)
    lane = jnp.arange(LANES)
    sign = jnp.where((lane % ATTN_HEAD_DIM) < half, -1.0, 1.0).astype(F32).reshape(1, LANES)
    tb = min(ROPE_TB, seq)
    return pl.pallas_call(
        _rope_kernel,
        grid=(seq // tb,),
        in_specs=[
            pl.BlockSpec((tb, 1), lambda i: (i, 0)),
            pl.BlockSpec((1, LANES), lambda i: (0, 0)),
            pl.BlockSpec((1, LANES), lambda i: (0, 0)),
        ],
        out_specs=[pl.BlockSpec((tb, LANES), lambda i: (i, 0))] * 2,
        out_shape=[jax.ShapeDtypeStruct((seq, LANES), F32)] * 2,
        compiler_params=_cparams("parallel"),
        name="rope_tables",
    )(pos_col, invf, sign)


NORM_ROWS = 128


def _prenorm_to(x_ref, nw_ref, sc_ref, sh_ref, h_ref):
    tm = x_ref.shape[0]

    def body(r, carry):
        rows = pl.ds(pl.multiple_of(r * NORM_ROWS, NORM_ROWS), NORM_ROWS)
        x = x_ref[rows, :]
        y = x * lax.rsqrt(jnp.mean(x * x, axis=-1, keepdims=True) + EPS) * nw_ref[...]
        h_ref[rows, :] = (y * (1.0 + sc_ref[...]) + sh_ref[...]).astype(BF16)
        return carry

    lax.fori_loop(0, tm // NORM_ROWS, body, 0)


INPROJ_TM = 1024

MOD_SHIFT_A, MOD_SCALE_A, MOD_GATE_A, MOD_SHIFT_M, MOD_SCALE_M, MOD_GATE_M = range(6)


def _layer_spec(shape, layer, *rest):
    rest = rest or (0,) * len(shape)

    def index_map(*grid_idx):
        return (layer,) + tuple(r(*grid_idx) if callable(r) else r for r in rest)

    return pl.BlockSpec((None,) + tuple(shape), index_map)


def _inproj_kernel(x_ref, nw_ref, sc_ref, sh_ref, wt_ref, tail_ref, o_ref, h_ref):
    j = pl.program_id(1)

    @pl.when(j == 0)
    def _():
        _prenorm_to(x_ref, nw_ref, sc_ref, sh_ref, h_ref)

    def project(w_ref):
        o_ref[...] = _mm_nt(h_ref[...], w_ref[...])

    @pl.when(j < N_WIDE_SLABS)
    def _():
        project(wt_ref)

    @pl.when(j == N_WIDE_SLABS)
    def _():
        project(tail_ref)


def _wide_slab_row(j):
    j = jnp.minimum(j, N_WIDE_SLABS - 1)
    group = j * (GW // SUBLANES)
    for first_slab, skipped in WIDE_SLAB_SKIPS:
        group = group + jnp.where(j >= first_slab, skipped // SUBLANES, 0)
    return group * SUBLANES


def _in_projection(x, layer, norm_w, mod, w_t, w_tail):
    seq, d = x.shape
    tm = min(INPROJ_TM, seq)
    return pl.pallas_call(
        _inproj_kernel,
        grid=(seq // tm, IN_COLS // GW),
        in_specs=[
            pl.BlockSpec((tm, d), lambda i, j: (i, 0)),
            _layer_spec((1, d), layer),
            _layer_spec((1, d), layer, 0, MOD_SCALE_A),
            _layer_spec((1, d), layer, 0, MOD_SHIFT_A),
            pl.BlockSpec((None, pl.Element(GW), pl.Element(d)), lambda i, j: (layer, _wide_slab_row(j), 0)),
            _layer_spec((GW, d), layer),
        ],
        out_specs=pl.BlockSpec((tm, GW), lambda i, j: (i, j)),
        out_shape=jax.ShapeDtypeStruct((seq, IN_COLS), F32),
        scratch_shapes=[pltpu.VMEM((tm, d), BF16)],
        compiler_params=_cparams("parallel", "arbitrary"),
        name="in_proj",
    )(x, norm_w, mod, mod, w_t, w_tail)


GDN_TB = 256


def _l2norm(x):
    return x * lax.rsqrt(jnp.sum(x * x, axis=-1, keepdims=True) + EPS)


GDN_BLK = 128


def _unit_lower_inverse_minus_identity(ms, r, s):
    same8 = _blk(r, 8) == _blk(s, 8)
    same16 = _blk(r, 16) == _blk(s, 16)
    same32 = _blk(r, 32) == _blk(s, 32)
    n1 = [jnp.where(same8, -m, 0.0) for m in ms]
    n2 = [_mm(a, a) for a in n1]
    n3 = [_mm(a, b) for a, b in zip(n1, n2)]
    n4 = [_mm(b, b) for b in n2]
    ys = [a + b + c for a, b, c in zip(n1, n2, n3)]
    ts = [_mm(y, d) for y, d in zip(ys, n4)]
    ys = [y + d + t for y, d, t in zip(ys, n4, ts)]
    for sel in (same16 & ~same8, same32 & ~same16, ~same32):
        blks = [jnp.where(sel, m, 0.0) for m in ms]
        cs = [b + _mm(y, b) for y, b in zip(ys, blks)]
        ts = [_mm(c, y) for c, y in zip(cs, ys)]
        ys = [y - c - t for y, c, t in zip(ys, cs, ts)]
    return ys


def _gdn_kernel(q_ref, k_ref, v_ref, z_ref, qp_ref, kp_ref, vp_ref, sm_ref, cw_ref, lp_ref, nw_ref,
                o_ref, state_ref, conv_ref):
    step = pl.program_id(0)
    tb = q_ref.shape[0]
    hd = GDN_HEAD_DIM
    nblk = tb // GDN_BLK
    chunks_per_blk = GDN_BLK // GDN_CHUNK

    @pl.when(step == 0)
    def _():
        state_ref[...] = jnp.zeros_like(state_ref)

    keep_prev = jnp.where(step == 0, 0.0, 1.0)
    cw = cw_ref[...]

    def conv_silu(x_ref, p_ref, idx):
        w = cw[:, idx * GW:(idx + 1) * GW]
        return _silu(_causal_conv_staged(x_ref, p_ref[...] * keep_prev, w, 4, conv_ref.at[idx]))

    q = conv_silu(q_ref, qp_ref, 0)
    k = conv_silu(k_ref, kp_ref, 1)
    v = conv_silu(v_ref, vp_ref, 2)
    z = z_ref[...]

    small = sm_ref[...]
    a_log = lp_ref[0:1, :]
    dt_bias = lp_ref[1:2, :]
    beta_all = _sigmoid(small)
    g_all = -jnp.exp(a_log) * jax.nn.softplus(small + dt_bias)
    gc_all, gtot_all = _chunk_cumsum(g_all, GDN_CHUNK)
    gc_t = gc_all.T

    r = lax.broadcasted_iota(jnp.int32, (GDN_BLK, GDN_BLK), 0)
    s = lax.broadcasted_iota(jnp.int32, (GDN_BLK, GDN_BLK), 1)
    same_chunk = _blk(r, GDN_CHUNK) == _blk(s, GDN_CHUNK)
    mask_incl = same_chunk & (s <= r)
    mask_strict = same_chunk & (s < r)

    problems = [(slice(b * GDN_BLK, (b + 1) * GDN_BLK), h) for b in range(nblk) for h in range(GDN_HEADS)]
    head_cols = lambda h: slice(h * hd, (h + 1) * hd)
    gate = lambda arr, rows, lane: arr[rows, lane:lane + 1]

    qn = [_l2norm(q[rows, head_cols(h)]) * (hd ** -0.5) for rows, h in problems]
    kn = [_l2norm(k[rows, head_cols(h)]) for rows, h in problems]
    beta = [gate(beta_all, rows, LANE_BETA + h) for rows, h in problems]
    gcol = [gate(gc_all, rows, LANE_ALPHA + h) for rows, h in problems]
    gtot = [gate(gtot_all, rows, LANE_ALPHA + h) for rows, h in problems]
    grow = [gc_t[LANE_ALPHA + h:LANE_ALPHA + h + 1, rows] for rows, h in problems]

    decay = [jnp.exp(jnp.where(mask_incl, gc - gr, -jnp.inf)) for gc, gr in zip(gcol, grow)]
    k_beta = [kk * b for kk, b in zip(kn, beta)]
    k16 = [kk.astype(BF16) for kk in kn]
    ms = [jnp.where(mask_strict, _mm_nt(kb, kk) * d, 0.0) for kb, kk, d in zip(k_beta, k16, decay)]
    attn = [_mm_nt(qq, kk) * d for qq, kk, d in zip(qn, k16, decay)]
    ys = _unit_lower_inverse_minus_identity(ms, r, s)

    e_gc = [jnp.exp(gc) for gc in gcol]
    rhs = [jnp.concatenate([v[rows, head_cols(h)] * b, kb * e], axis=1)
           for (rows, h), b, kb, e in zip(problems, beta, k_beta, e_gc)]
    uw = [x + _mm(y, x) for y, x in zip(ys, rhs)]
    q_dec = [(qq * e).astype(BF16) for qq, e in zip(qn, e_gc)]
    k_dec = [(kk * jnp.exp(gt - gc)).astype(BF16) for kk, gt, gc in zip(kn, gtot, gcol)]
    chunk_dec = [jnp.exp(gt) for gt in gtot]

    states = [state_ref[h] for h in range(GDN_HEADS)]
    v_new = [[] for _ in problems]
    o_inter = [[] for _ in problems]
    for b in range(nblk):
        for c in range(chunks_per_blk):
            rows = slice(c * GDN_CHUNK, (c + 1) * GDN_CHUNK)
            for h in range(GDN_HEADS):
                p = b * GDN_HEADS + h
                st16 = states[h].astype(BF16)
                vn = uw[p][rows, :hd] - jnp.dot(uw[p][rows, hd:].astype(BF16), st16, preferred_element_type=F32)
                o_inter[p].append(jnp.dot(q_dec[p][rows], st16, preferred_element_type=F32))
                states[h] = (states[h] * chunk_dec[p][c * GDN_CHUNK:c * GDN_CHUNK + 1, :]
                             + _mm_tn(k_dec[p][rows], vn))
                v_new[p].append(vn)
    for h in range(GDN_HEADS):
        state_ref[h] = states[h]

    for p, (rows, h) in enumerate(problems):
        o = jnp.concatenate(o_inter[p], axis=0) + _mm(attn[p], jnp.concatenate(v_new[p], axis=0))
        o = o * lax.rsqrt(jnp.mean(o * o, axis=-1, keepdims=True) + EPS) * nw_ref[...]
        o_ref[rows, head_cols(h)] = (o * _silu(z[rows, head_cols(h)])).astype(o_ref.dtype)


def _prev_rows_spec(tb, width, slab):
    blocks = tb // SUBLANES
    return pl.BlockSpec((SUBLANES, width), lambda i: (jnp.maximum(i * blocks - 1, 0), slab))


def _lane_params(lane0, *rows):
    stacked = jnp.stack([r.astype(F32) for r in rows], axis=1)
    nrows, heads = stacked.shape[1:]
    return jnp.pad(stacked, ((0, 0), (0, SUBLANES - nrows), (lane0, LANES - lane0 - heads)))


def _gated_deltanet(proj, layer, conv_w, lane_params, norm_w):
    seq = proj.shape[0]
    tb = min(GDN_TB, seq)
    slab = lambda j: pl.BlockSpec((tb, GW), lambda i: (i, j))
    full = lambda shape: _layer_spec(shape, layer)
    return pl.pallas_call(
        _gdn_kernel,
        grid=(seq // tb,),
        in_specs=[
            slab(SLAB_GQ), slab(SLAB_GK), slab(SLAB_GV), slab(SLAB_GZ),
            _prev_rows_spec(tb, GW, SLAB_GQ), _prev_rows_spec(tb, GW, SLAB_GK), _prev_rows_spec(tb, GW, SLAB_GV),
            pl.BlockSpec((tb, LANES), lambda i: (i, SLAB128_SMALL)),
            full((4, 3 * GW)), full((SUBLANES, LANES)), full((1, GDN_HEAD_DIM)),
        ],
        out_specs=pl.BlockSpec((tb, GW), lambda i: (i, 0)),
        out_shape=jax.ShapeDtypeStruct((seq, GW), BF16),
        scratch_shapes=[pltpu.VMEM((GDN_HEADS, GDN_HEAD_DIM, GDN_HEAD_DIM), F32),
                        pltpu.VMEM((3, tb + SUBLANES, GW), F32)],
        compiler_params=_cparams("arbitrary"),
        name="gated_deltanet",
    )(proj, proj, proj, proj, proj, proj, proj, proj, conv_w, lane_params, norm_w)


SSD_TB = 256


def _ssd_kernel(z_ref, x_ref, bc_ref, xp_ref, bcp_ref, sm_ref, cw_ref, cb_ref, lp_ref, nw_ref,
                o_ref, state_ref):
    step = pl.program_id(0)
    tb = z_ref.shape[0]
    nstate = SSM_STATE
    chunk = SSM_CHUNK
    nchunks = tb // chunk
    n_pairs = SSM_HEADS // 2
    pairs_per_group = n_pairs // 2

    @pl.when(step == 0)
    def _():
        state_ref[...] = jnp.zeros_like(state_ref)

    keep_prev = jnp.where(step == 0, 0.0, 1.0)
    cw = cw_ref[...]
    cbias = cb_ref[...]
    xs = _silu(_causal_conv(x_ref[...], xp_ref[...] * keep_prev, cw[:, :GW], 4) + cbias[:, :GW])
    bc = _silu(_causal_conv(bc_ref[...], bcp_ref[...] * keep_prev, cw[:, GW:], 4) + cbias[:, GW:])
    z = z_ref[...]

    small = sm_ref[...]
    a_neg = -jnp.exp(lp_ref[0:1, :])
    dt_all = jax.nn.softplus(small + lp_ref[1:2, :])
    dskip_all = lp_ref[2:3, :]
    gc_all, gtot_all = _chunk_cumsum(dt_all * a_neg, chunk)
    gc_t = gc_all.T

    r = lax.broadcasted_iota(jnp.int32, (chunk, chunk), 0)
    s = lax.broadcasted_iota(jnp.int32, (chunk, chunk), 1)
    mask_incl = s <= r
    lane = lax.broadcasted_iota(jnp.int32, (1, LANES), 1)
    first_half = lane < SSM_HEAD_DIM
    by_half = lambda a, b: jnp.where(first_half, a, b)

    problems = [(c, p) for c in range(nchunks) for p in range(n_pairs)]
    rows_of = lambda c: slice(c * chunk, (c + 1) * chunk)
    cols_of = lambda p: slice(p * LANES, (p + 1) * LANES)
    lanes_of = lambda p: (LANE_DT + 2 * p, LANE_DT + 2 * p + 1)
    col = lambda arr, c, l: arr[rows_of(c), l:l + 1]

    b_mat = {(c, g): bc[rows_of(c), g * nstate:(g + 1) * nstate].astype(BF16)
             for c in range(nchunks) for g in range(2)}
    c_mat = {(c, g): bc[rows_of(c), (2 + g) * nstate:(3 + g) * nstate].astype(BF16)
             for c in range(nchunks) for g in range(2)}
    cb = {key: _mm_nt(c_mat[key], b_mat[key]) for key in b_mat}

    x_pair = [xs[rows_of(c), cols_of(p)] for c, p in problems]
    xdt = [x * by_half(col(dt_all, c, lanes_of(p)[0]), col(dt_all, c, lanes_of(p)[1]))
           for x, (c, p) in zip(x_pair, problems)]
    y_diag = []
    for (c, p), xd in zip(problems, xdt):
        halves = (by_half(xd, 0.0), by_half(0.0, xd))
        acc = None
        for l, half in zip(lanes_of(p), halves):
            grow = gc_t[l:l + 1, rows_of(c)]
            lmat = jnp.exp(jnp.where(mask_incl, col(gc_all, c, l) - grow, -jnp.inf))
            t = _mm(cb[c, p // pairs_per_group] * lmat, half)
            acc = t if acc is None else acc + t
        y_diag.append(acc)

    e_in = [by_half(jnp.exp(col(gc_all, c, lanes_of(p)[0])), jnp.exp(col(gc_all, c, lanes_of(p)[1])))
            for c, p in problems]
    e_out = [by_half(jnp.exp(col(gtot_all, c, lanes_of(p)[0]) - col(gc_all, c, lanes_of(p)[0])),
                     jnp.exp(col(gtot_all, c, lanes_of(p)[1]) - col(gc_all, c, lanes_of(p)[1])))
             for c, p in problems]
    states = [state_ref[p] for p in range(n_pairs)]
    y_off = []
    for idx, (c, p) in enumerate(problems):
        la, lb = lanes_of(p)
        g = p // pairs_per_group
        y_off.append(e_in[idx] * _mm(c_mat[c, g], states[p]))
        first = c * chunk
        cd_lane = by_half(jnp.exp(gtot_all[first:first + 1, la:la + 1]),
                          jnp.exp(gtot_all[first:first + 1, lb:lb + 1]))
        states[p] = states[p] * cd_lane + _mm_tn(b_mat[c, g], xdt[idx] * e_out[idx])
    for p in range(n_pairs):
        state_ref[p] = states[p]

    ys = []
    for idx, (c, p) in enumerate(problems):
        la, lb = lanes_of(p)
        y = y_diag[idx] + y_off[idx] + x_pair[idx] * by_half(dskip_all[:, la:la + 1], dskip_all[:, lb:lb + 1])
        ys.append(y * _silu(z[rows_of(c), cols_of(p)]))
    sumsq = [jnp.sum(y * y, axis=-1, keepdims=True) for y in ys]
    for c in range(nchunks):
        for g in range(2):
            members = [c * n_pairs + g * pairs_per_group + k for k in range(pairs_per_group)]
            total = sumsq[members[0]]
            for idx in members[1:]:
                total = total + sumsq[idx]
            inv = lax.rsqrt(total / (pairs_per_group * LANES) + EPS)
            for idx in members:
                cols = cols_of(problems[idx][1])
                o_ref[rows_of(c), cols] = (ys[idx] * inv * nw_ref[:, cols]).astype(o_ref.dtype)


def _mamba2_ssd(proj, layer, conv_w, conv_b, lane_params, norm_w):
    seq = proj.shape[0]
    tb = min(SSD_TB, seq)
    slab = lambda j: pl.BlockSpec((tb, GW), lambda i: (i, j))
    full = lambda shape: _layer_spec(shape, layer)
    return pl.pallas_call(
        _ssd_kernel,
        grid=(seq // tb,),
        in_specs=[
            slab(SLAB_SZ), slab(SLAB_SX), slab(SLAB_SBC),
            _prev_rows_spec(tb, GW, SLAB_SX), _prev_rows_spec(tb, GW, SLAB_SBC),
            pl.BlockSpec((tb, LANES), lambda i: (i, SLAB128_SMALL)),
            full((4, 2 * GW)), full((1, 2 * GW)), full((SUBLANES, LANES)), full((1, GW)),
        ],
        out_specs=pl.BlockSpec((tb, GW), lambda i: (i, 0)),
        out_shape=jax.ShapeDtypeStruct((seq, GW), BF16),
        scratch_shapes=[pltpu.VMEM((SSM_HEADS // 2, SSM_STATE, LANES), F32)],
        compiler_params=_cparams("arbitrary"),
        name="mamba2_ssd",
    )(proj, proj, proj, proj, proj, proj, conv_w, conv_b, lane_params, norm_w)


def _rotate_half(x, lane_in_head_low):
    width = x.shape[1]
    half = ATTN_HEAD_DIM // 2
    ahead = pltpu.roll(x, width - half, axis=1)
    behind = pltpu.roll(x, half, axis=1)
    return jnp.where(lane_in_head_low, ahead, behind)


SWA_TB = 256


def _swa_kernel(q_ref, k_ref, v_ref, cos_ref, sin_ref, sink_ref, o_ref, kprev_ref, vprev_ref):
    step = pl.program_id(0)
    tb = q_ref.shape[0]
    win = WINDOW
    hd = ATTN_HEAD_DIM

    @pl.when(step == 0)
    def _():
        kprev_ref[...] = jnp.zeros_like(kprev_ref)
        vprev_ref[...] = jnp.zeros_like(vprev_ref)

    cos = cos_ref[...]
    sin = sin_ref[...]
    lane = lax.broadcasted_iota(jnp.int32, (1, LANES), 1)
    low = (lane & (hd - 1)) < (hd // 2)
    first_half = lane < hd

    k_cur = k_ref[...]
    k_cur = k_cur * cos + _rotate_half(k_cur, low) * sin
    v_cur = v_ref[...]
    k_all = jnp.concatenate([kprev_ref[...], k_cur], axis=0)
    v_all = jnp.concatenate([vprev_ref[...], v_cur], axis=0)
    k_swap = pltpu.roll(k_all, hd, axis=1)
    v_swap = pltpu.roll(v_all, hd, axis=1)

    def placed(x, x_swap, kv_head, pos):
        src = x if kv_head == pos else x_swap
        keep = first_half if pos == 0 else ~first_half
        return jnp.where(keep, src, 0.0).astype(BF16)

    qi = lax.broadcasted_iota(jnp.int32, (win, 2 * win), 0)
    kj = lax.broadcasted_iota(jnp.int32, (win, 2 * win), 1)
    rel = qi + win - kj
    band = (rel >= 0) & (rel < win)
    band_first = band & ((kj >= win) | (step > 0))

    problems = [(j, h) for j in range(tb // win) for h in range(ATTN_Q_HEADS)]
    kv_of = lambda head: head // (ATTN_Q_HEADS // 2)
    pair_cols = lambda pair: slice(pair * LANES, (pair + 1) * LANES)
    k_at = {(g, pos): placed(k_all, k_swap, g, pos) for g in range(2) for pos in range(2)}
    v_at = {(g, pos): placed(v_all, v_swap, g, pos) for g in range(2) for pos in range(2)}
    q_pairs = []
    for pair in range(ATTN_Q_HEADS // 2):
        q_pair = q_ref[:, pair_cols(pair)]
        q_pairs.append((q_pair * cos + _rotate_half(q_pair, low) * sin).astype(BF16))
    sinks = [sink_ref[h:h + 1, 0:1] for _, h in problems]
    sc = [jnp.where(band_first if j == 0 else band,
                    _mm_nt(q_pairs[h // 2][j * win:(j + 1) * win],
                           k_at[kv_of(h), h % 2][j * win:(j + 2) * win]) * (hd ** -0.5),
                    -jnp.inf)
          for j, h in problems]
    mx = [jnp.maximum(jnp.max(s_, axis=-1, keepdims=True), sk) for s_, sk in zip(sc, sinks)]
    p = [jnp.exp(s_ - m_) for s_, m_ in zip(sc, mx)]
    denom = [jnp.sum(p_, axis=-1, keepdims=True) + jnp.exp(sk - m_) for p_, sk, m_ in zip(p, sinks, mx)]
    o = [_mm(p_, v_at[kv_of(h), h % 2][j * win:(j + 2) * win]) / d_
         for (j, h), p_, d_ in zip(problems, p, denom)]
    for idx in range(0, len(problems), 2):
        j, h = problems[idx]
        o_ref[j * win:(j + 1) * win, pair_cols(h // 2)] = (o[idx] + o[idx + 1]).astype(o_ref.dtype)

    kprev_ref[...] = k_cur[tb - win:]
    vprev_ref[...] = v_cur[tb - win:]


def _sliding_window_attention(proj, layer, cos, sin, sink_rows):
    seq = proj.shape[0]
    tb = min(SWA_TB, seq)
    return pl.pallas_call(
        _swa_kernel,
        grid=(seq // tb,),
        in_specs=[
            pl.BlockSpec((tb, GW), lambda i: (i, SLAB_AQ)),
            pl.BlockSpec((tb, LANES), lambda i: (i, SLAB128_AK)),
            pl.BlockSpec((tb, LANES), lambda i: (i, SLAB128_AV)),
            pl.BlockSpec((tb, LANES), lambda i: (i, 0)),
            pl.BlockSpec((tb, LANES), lambda i: (i, 0)),
            _layer_spec((ATTN_Q_HEADS, LANES), layer),
        ],
        out_specs=pl.BlockSpec((tb, GW), lambda i: (i, 0)),
        out_shape=jax.ShapeDtypeStruct((seq, GW), BF16),
        scratch_shapes=[pltpu.VMEM((WINDOW, LANES), F32), pltpu.VMEM((WINDOW, LANES), F32)],
        compiler_params=_cparams("arbitrary"),
        name="sliding_window_attn",
    )(proj, proj, proj, cos, sin, sink_rows)


SCONV_TB = 512


def _sconv_kernel(b_ref, c_ref, h_ref, cp_ref, hp_ref, w_ref, o_ref):
    keep_prev = jnp.where(pl.program_id(0) == 0, 0.0, 1.0)
    ch = c_ref[...] * h_ref[...]
    ch_prev = cp_ref[...] * hp_ref[...] * keep_prev
    o_ref[...] = (b_ref[...] * _causal_conv(ch, ch_prev, w_ref[...], 3)).astype(o_ref.dtype)


def _short_conv(proj, layer, conv_w):
    seq = proj.shape[0]
    tb = min(SCONV_TB, seq)
    slab = lambda j: pl.BlockSpec((tb, GW), lambda i: (i, j))
    return pl.pallas_call(
        _sconv_kernel,
        grid=(seq // tb,),
        in_specs=[
            slab(SLAB_CB), slab(SLAB_CC), slab(SLAB_CH),
            _prev_rows_spec(tb, GW, SLAB_CC), _prev_rows_spec(tb, GW, SLAB_CH),
            _layer_spec((3, GW), layer),
        ],
        out_specs=pl.BlockSpec((tb, GW), lambda i: (i, 0)),
        out_shape=jax.ShapeDtypeStruct((seq, GW), BF16),
        compiler_params=_cparams("parallel"),
        name="short_conv",
    )(proj, proj, proj, proj, proj, conv_w)


OUTPROJ_TM = 512


def _outproj_kernel(ya_ref, yb_ref, yc_ref, yd_ref, w_ref, x_ref, gate_ref, nw_ref, o_ref):
    y = jnp.concatenate([ya_ref[...], yb_ref[...], yc_ref[...], yd_ref[...]], axis=1)
    y = jnp.dot(y, w_ref[...], preferred_element_type=F32)
    y = y * lax.rsqrt(jnp.mean(y * y, axis=-1, keepdims=True) + EPS) * nw_ref[...]
    o_ref[...] = x_ref[...] + gate_ref[...] * y


def _out_projection(ys, layer, w_bf16, x, mod, norm_w):
    seq, d = x.shape
    tm = min(OUTPROJ_TM, seq)
    return pl.pallas_call(
        _outproj_kernel,
        grid=(seq // tm,),
        in_specs=[pl.BlockSpec((tm, GW), lambda i: (i, 0))] * 4 + [
            _layer_spec((4 * GW, d), layer),
            pl.BlockSpec((tm, d), lambda i: (i, 0)),
            _layer_spec((1, d), layer, 0, MOD_GATE_A),
            _layer_spec((1, d), layer),
        ],
        out_specs=pl.BlockSpec((tm, d), lambda i: (i, 0)),
        out_shape=jax.ShapeDtypeStruct((seq, d), F32),
        compiler_params=_cparams("parallel"),
        name="out_proj",
    )(*ys, w_bf16, x, mod, norm_w)


MLP_TM = 1024
MLP_TH = 512


def _mlp_kernel(x_ref, nw_ref, sc_ref, sh_ref, wu_ref, wd_ref, gate_ref, pw_ref, o_ref, h_ref):
    j = pl.program_id(1)

    @pl.when(j == 0)
    def _():
        _prenorm_to(x_ref, nw_ref, sc_ref, sh_ref, h_ref)
        o_ref[...] = jnp.zeros_like(o_ref)

    hid = _mm(h_ref[...], wu_ref[...])
    hid = jnp.square(jnp.maximum(hid, 0.0))
    o_ref[...] += _mm(hid, wd_ref[...])

    @pl.when(j == pl.num_programs(1) - 1)
    def _():
        def body(r, carry):
            rows = pl.ds(pl.multiple_of(r * NORM_ROWS, NORM_ROWS), NORM_ROWS)
            y = o_ref[rows, :]
            y = y * lax.rsqrt(jnp.mean(y * y, axis=-1, keepdims=True) + EPS) * pw_ref[...]
            o_ref[rows, :] = x_ref[rows, :] + gate_ref[...] * y
            return carry

        lax.fori_loop(0, x_ref.shape[0] // NORM_ROWS, body, 0)


def _mlp(x, layer, norm_w, mod, w_up, w_down, post_w):
    seq, d = x.shape
    hidden = w_up.shape[2]
    tm = min(MLP_TM, seq)
    vmem = (2 * 2 * tm * d * 4 + tm * d * 2 + 2 * 2 * d * MLP_TH * w_up.dtype.itemsize
            + 2 * d * MLP_TH * 2 + tm * MLP_TH * 6 + 2 * MIB)
    return pl.pallas_call(
        _mlp_kernel,
        grid=(seq // tm, hidden // MLP_TH),
        in_specs=[
            pl.BlockSpec((tm, d), lambda i, j: (i, 0)),
            _layer_spec((1, d), layer),
            _layer_spec((1, d), layer, 0, MOD_SCALE_M),
            _layer_spec((1, d), layer, 0, MOD_SHIFT_M),
            _layer_spec((d, MLP_TH), layer, 0, lambda i, j: j),
            _layer_spec((MLP_TH, d), layer, lambda i, j: j, 0),
            _layer_spec((1, d), layer, 0, MOD_GATE_M),
            _layer_spec((1, d), layer),
        ],
        out_specs=pl.BlockSpec((tm, d), lambda i, j: (i, 0)),
        out_shape=jax.ShapeDtypeStruct((seq, d), F32),
        scratch_shapes=[pltpu.VMEM((tm, d), BF16)],
        compiler_params=_cparams("parallel", "arbitrary", vmem_limit=vmem),
        name="mlp",
    )(x, norm_w, mod, mod, w_up, w_down, mod, post_w)


def _tail_slab(w_t):
    parts = [w_t[:, IN_OFFSET[name][0]:IN_OFFSET[name][1]] for name in ("ak", "av", "gb", "ga", "sdt")]
    used = sum(p.shape[1] for p in parts)
    parts.append(jnp.zeros((w_t.shape[0], GW - used, w_t.shape[2]), w_t.dtype))
    return jnp.concatenate(parts, axis=1)


def kernel(x, c, positions, ada_w, ada_b, norm_pre_mix, norm_post_mix, norm_pre_mlp, norm_post_mlp,
           w_in, w_out, gdn_conv_w, gdn_a_log, gdn_dt_bias, gdn_norm_w, ssm_conv_w, ssm_conv_b,
           ssm_a_log, ssm_dt_bias, ssm_d, ssm_norm_w, attn_sinks, sc_conv_w, w_up, w_down):
    batch, seq, d = x.shape
    assert batch == 1 and d == D_MODEL
    depth = ada_w.shape[0]
    xs = x.reshape(seq, d)

    mod = _modulation(c.reshape(d, 1), ada_w, ada_b)
    cos, sin = _rope_tables(positions.reshape(seq, 1))

    rows = lambda v: v.reshape(depth, 1, -1)
    w_in_t = jnp.swapaxes(w_in, 1, 2).astype(BF16)
    w_in_tail = _tail_slab(w_in_t)
    w_out_bf16 = w_out.astype(BF16)
    gdn_lanes = _lane_params(LANE_ALPHA, gdn_a_log, gdn_dt_bias)
    ssm_lanes = _lane_params(LANE_DT, ssm_a_log, ssm_dt_bias, ssm_d)
    sink_rows = jnp.broadcast_to(attn_sinks.astype(F32)[:, :, None], (depth, ATTN_Q_HEADS, LANES))

    for i in range(depth):
        proj = _in_projection(xs, i, rows(norm_pre_mix), mod, w_in_t, w_in_tail)
        y_a = _gated_deltanet(proj, i, gdn_conv_w, gdn_lanes, rows(gdn_norm_w))
        y_b = _mamba2_ssd(proj, i, ssm_conv_w, rows(ssm_conv_b), ssm_lanes, rows(ssm_norm_w))
        y_c = _sliding_window_attention(proj, i, cos, sin, sink_rows)
        y_d = _short_conv(proj, i, sc_conv_w)
        xs = _out_projection((y_a, y_b, y_c, y_d), i, w_out_bf16, xs, mod, rows(norm_post_mix))
        xs = _mlp(xs, i, rows(norm_pre_mlp), mod, w_up, w_down, rows(norm_post_mlp))
    return xs.reshape(batch, seq, d)
```

```python
import functools

import jax
import jax.numpy as jnp
from jax import lax
from jax.experimental import pallas as pl
from jax.experimental.pallas import tpu as pltpu

F32 = jnp.float32
BF16 = jnp.bfloat16

D_MODEL = 2048
GW = 512
GDN_HEADS, GDN_HEAD_DIM, GDN_CHUNK = 4, 128, 64
SSM_HEADS, SSM_HEAD_DIM, SSM_STATE, SSM_CHUNK = 8, 64, 128, 128
ATTN_HEAD_DIM, ATTN_Q_HEADS, WINDOW = 64, 8, 128
ROPE_THETA = 10000.0
MLP_HIDDEN = 4 * D_MODEL
EPS = 1e-6

SUBLANES = 8
LANES = 128

_IN_SEGMENTS = (("gq", GW), ("gk", GW), ("gv", GW), ("gz", GW), ("gb", GDN_HEADS), ("ga", GDN_HEADS),
                ("sz", GW), ("sx", GW), ("sbc", 4 * SSM_STATE), ("sdt", SSM_HEADS),
                ("aq", GW), ("ak", 2 * ATTN_HEAD_DIM), ("av", 2 * ATTN_HEAD_DIM),
                ("cb", GW), ("cc", GW), ("ch", GW))
IN_OFFSET = {}
_off = 0
for _name, _size in _IN_SEGMENTS:
    IN_OFFSET[_name] = (_off, _off + _size)
    _off += _size
IN_WIDTH = _off

IN_GROUP = 3 * GW
CONTIGUOUS_GROUPS = (("gq", "gk", "gv"), ("sz", "sx", "sbc"), ("cb", "cc", "ch"))
GATHERED_GROUP = ("gz", "aq", "ak", "av", "gb", "ga", "sdt")
IN_COLS = (len(CONTIGUOUS_GROUPS) + 1) * IN_GROUP
for _grp in CONTIGUOUS_GROUPS:
    assert IN_OFFSET[_grp[0]][0] % SUBLANES == 0
    assert all(IN_OFFSET[a][1] == IN_OFFSET[b][0] for a, b in zip(_grp, _grp[1:]))
(SLAB_GQ, SLAB_GK, SLAB_GV, SLAB_SZ, SLAB_SX, SLAB_SBC, SLAB_CB, SLAB_CC, SLAB_CH, SLAB_GZ, SLAB_AQ) = range(11)
SLAB128_AK, SLAB128_AV, SLAB128_SMALL = 44, 45, 46
LANE_BETA, LANE_ALPHA, LANE_DT = 0, 4, 8

MIB = 1024 * 1024
V7X_VMEM_BYTES = 64 * MIB
VMEM_LIMIT = 56 * MIB


def _cparams(*sem, vmem_limit=VMEM_LIMIT):
    assert vmem_limit < V7X_VMEM_BYTES
    return pltpu.CompilerParams(dimension_semantics=sem, vmem_limit_bytes=vmem_limit)


def _sigmoid(x):
    return jax.nn.sigmoid(x)


def _silu(x):
    return x * jax.nn.sigmoid(x)


def _mm(a, b):
    return jnp.dot(a.astype(BF16), b.astype(BF16), preferred_element_type=F32)


def _mm_nt(a, b):
    return lax.dot_general(a.astype(BF16), b.astype(BF16), (((1,), (1,)), ((), ())),
                           preferred_element_type=F32)


def _mm_tn(a, b):
    return lax.dot_general(a.astype(BF16), b.astype(BF16), (((0,), (0,)), ((), ())),
                           preferred_element_type=F32)


def _blk(idx, size):
    assert size & (size - 1) == 0
    return lax.shift_right_logical(idx, size.bit_length() - 1)


def _split3(x):
    x1 = x.astype(BF16)
    r1 = x - x1.astype(F32)
    x2 = r1.astype(BF16)
    r2 = r1 - x2.astype(F32)
    return x1, x2, r2.astype(BF16)


def _chunk_cumsum(g, chunk):
    rows = g.shape[0]
    r = lax.broadcasted_iota(jnp.int32, (2 * rows, rows), 0)
    s = lax.broadcasted_iota(jnp.int32, (2 * rows, rows), 1)
    rr = jnp.where(r >= rows, r - rows, r)
    same = _blk(rr, chunk) == _blk(s, chunk)
    sel = same & ((r >= rows) | (s <= rr))
    mat = jnp.where(sel, 1.0, 0.0).astype(BF16)
    acc = None
    for part in _split3(g):
        t = jnp.dot(mat, part, preferred_element_type=F32)
        acc = t if acc is None else acc + t
    return acc[:rows], acc[rows:]


def _causal_conv(x, prev, w, taps):
    rows = x.shape[0]
    xp = jnp.concatenate([prev, x], axis=0)
    acc = x * w[taps - 1:taps]
    for d in range(1, taps):
        acc = acc + xp[SUBLANES - d:SUBLANES - d + rows] * w[taps - 1 - d:taps - d]
    return acc


def _causal_conv_staged(x_ref, prev, w, taps, stage_ref):
    rows = x_ref.shape[0]
    x = x_ref[...]
    stage_ref[0:SUBLANES, :] = prev
    stage_ref[SUBLANES:, :] = x
    acc = x * w[taps - 1:taps]
    for d in range(1, taps):
        acc = acc + stage_ref[pl.ds(SUBLANES - d, rows), :] * w[taps - 1 - d:taps - d]
    return acc


MOD_TN = 2048
MOD_ROWS = 64


def _mod_kernel(c_ref, w_ref, b_ref, o_ref):
    d = c_ref.shape[0]

    def body(r, acc):
        rows = pl.ds(pl.multiple_of(r * MOD_ROWS, MOD_ROWS), MOD_ROWS)
        cc = _silu(c_ref[rows, :])
        prod = w_ref[0, rows, :] * cc
        return acc + jnp.sum(prod.reshape(MOD_ROWS // SUBLANES, SUBLANES, MOD_TN), axis=0)

    acc = lax.fori_loop(0, d // MOD_ROWS, body, jnp.zeros((SUBLANES, MOD_TN), F32))
    o_ref[0] = jnp.sum(acc, axis=0, keepdims=True) + b_ref[0]


def _modulation(c_col, ada_w, ada_b):
    depth, d, n = ada_w.shape
    out = pl.pallas_call(
        _mod_kernel,
        grid=(depth, n // MOD_TN),
        in_specs=[
            pl.BlockSpec((d, 1), lambda i, j: (0, 0)),
            pl.BlockSpec((1, d, MOD_TN), lambda i, j: (i, 0, j)),
            pl.BlockSpec((1, 1, MOD_TN), lambda i, j: (i, 0, j)),
        ],
        out_specs=pl.BlockSpec((1, 1, MOD_TN), lambda i, j: (i, 0, j)),
        out_shape=jax.ShapeDtypeStruct((depth, 1, n), F32),
        compiler_params=_cparams("parallel", "parallel"),
        name="adaln_mod",
    )(c_col, ada_w, ada_b.reshape(depth, 1, n))
    return out


ROPE_TB = 1024


def _rope_kernel(pos_ref, invf_ref, sign_ref, cos_ref, sin_ref):
    ang = pos_ref[...].astype(F32) * invf_ref[...]
    cos_ref[...] = jnp.cos(ang)
    sin_ref[...] = jnp.sin(ang) * sign_ref[...]


def _rope_tables(pos_col):
    seq = pos_col.shape[0]
    half = ATTN_HEAD_DIM // 2
    inv_freq = ROPE_THETA ** (-jnp.arange(0, ATTN_HEAD_DIM, 2, dtype=F32) / ATTN_HEAD_DIM)
    invf = jnp.tile(inv_freq, LANES // half).reshape(1, LANES)
    lane = jnp.arange(LANES)
    sign = jnp.where((lane % ATTN_HEAD_DIM) < half, -1.0, 1.0).astype(F32).reshape(1, LANES)
    tb = min(ROPE_TB, seq)
    return pl.pallas_call(
        _rope_kernel,
        grid=(seq // tb,),
        in_specs=[
            pl.BlockSpec((tb, 1), lambda i: (i, 0)),
            pl.BlockSpec((1, LANES), lambda i: (0, 0)),
            pl.BlockSpec((1, LANES), lambda i: (0, 0)),
        ],
        out_specs=[pl.BlockSpec((tb, LANES), lambda i: (i, 0))] * 2,
        out_shape=[jax.ShapeDtypeStruct((seq, LANES), F32)] * 2,
        compiler_params=_cparams("parallel"),
        name="rope_tables",
    )(pos_col, invf, sign)


NORM_ROWS = 128


def _prenorm_to(x_ref, nw_ref, sc_ref, sh_ref, h_ref):
    tm = x_ref.shape[0]

    def body(r, carry):
        rows = pl.ds(pl.multiple_of(r * NORM_ROWS, NORM_ROWS), NORM_ROWS)
        x = x_ref[rows, :]
        y = x * lax.rsqrt(jnp.mean(x * x, axis=-1, keepdims=True) + EPS) * nw_ref[...]
        h_ref[rows, :] = (y * (1.0 + sc_ref[...]) + sh_ref[...]).astype(BF16)
        return carry

    lax.fori_loop(0, tm // NORM_ROWS, body, 0)


INPROJ_TM = 1024

MOD_SHIFT_A, MOD_SCALE_A, MOD_GATE_A, MOD_SHIFT_M, MOD_SCALE_M, MOD_GATE_M = range(6)


def _layer_spec(shape, layer, *rest):
    rest = rest or (0,) * len(shape)

    def index_map(*grid_idx):
        return (layer,) + tuple(r(*grid_idx) if callable(r) else r for r in rest)

    return pl.BlockSpec((None,) + tuple(shape), index_map)


def _inproj_kernel(x_ref, nw_ref, sc_ref, sh_ref, wt_ref, tail_ref, o_ref, h_ref):
    j = pl.program_id(1)

    @pl.when(j == 0)
    def _():
        _prenorm_to(x_ref, nw_ref, sc_ref, sh_ref, h_ref)

    def project(w_ref):
        o_ref[...] = _mm_nt(h_ref[...], w_ref[...])

    @pl.when(j < len(CONTIGUOUS_GROUPS))
    def _():
        project(wt_ref)

    @pl.when(j == len(CONTIGUOUS_GROUPS))
    def _():
        project(tail_ref)


def _group_row(j):
    starts = [IN_OFFSET[grp[0]][0] // SUBLANES for grp in CONTIGUOUS_GROUPS]
    row8 = starts[-1]
    for k in range(len(starts) - 2, -1, -1):
        row8 = jnp.where(j <= k, starts[k], row8)
    return row8 * SUBLANES


def _in_projection(x, layer, norm_w, mod, w_t, w_gathered):
    seq, d = x.shape
    tm = min(INPROJ_TM, seq)
    return pl.pallas_call(
        _inproj_kernel,
        grid=(seq // tm, IN_COLS // IN_GROUP),
        in_specs=[
            pl.BlockSpec((tm, d), lambda i, j: (i, 0)),
            _layer_spec((1, d), layer),
            _layer_spec((1, d), layer, 0, MOD_SCALE_A),
            _layer_spec((1, d), layer, 0, MOD_SHIFT_A),
            pl.BlockSpec((None, pl.Element(IN_GROUP), pl.Element(d)), lambda i, j: (layer, _group_row(j), 0)),
            pl.BlockSpec((None, IN_GROUP, d), lambda i, j: (layer, 0, 0), pipeline_mode=pl.Buffered(1)),
        ],
        out_specs=pl.BlockSpec((tm, IN_GROUP), lambda i, j: (i, j)),
        out_shape=jax.ShapeDtypeStruct((seq, IN_COLS), F32),
        scratch_shapes=[pltpu.VMEM((tm, d), BF16)],
        compiler_params=_cparams("parallel", "arbitrary"),
        name="in_proj",
    )(x, norm_w, mod, mod, w_t, w_gathered)


GDN_TB = 512


def _l2norm(x):
    return x * lax.rsqrt(jnp.sum(x * x, axis=-1, keepdims=True) + EPS)


GDN_BLK = 128
GDN_PREP_STAGES_PER_CHUNK = 4


def _unit_lower_inverse_minus_identity(ms, r, s):
    same8 = _blk(r, 8) == _blk(s, 8)
    same16 = _blk(r, 16) == _blk(s, 16)
    same32 = _blk(r, 32) == _blk(s, 32)
    n1 = [jnp.where(same8, -m, 0.0) for m in ms]
    n2 = [_mm(a, a) for a in n1]
    yield
    n3 = [_mm(a, b) for a, b in zip(n1, n2)]
    n4 = [_mm(b, b) for b in n2]
    yield
    ys = [a + b + c for a, b, c in zip(n1, n2, n3)]
    ts = [_mm(y, d) for y, d in zip(ys, n4)]
    yield
    ys = [y + d + t for y, d, t in zip(ys, n4, ts)]
    for sel in (same16 & ~same8, same32 & ~same16, ~same32):
        blks = [jnp.where(sel, m, 0.0) for m in ms]
        cs = [b + _mm(y, b) for y, b in zip(ys, blks)]
        yield
        ts = [_mm(c, y) for c, y in zip(cs, ys)]
        yield
        ys = [y - c - t for y, c, t in zip(ys, cs, ts)]
    return ys


def _run_interleaved(main, side, main_stages_per_side_stage):
    side_live = True
    done = 0
    while True:
        try:
            next(main)
        except StopIteration as stop:
            result = stop.value
            break
        done += 1
        if side_live and done % main_stages_per_side_stage == 0:
            side_live = next(side, StopIteration) is not StopIteration
    while side_live:
        side_live = next(side, StopIteration) is not StopIteration
    return result


def _gdn_prepare(q_ref, k_ref, v_ref, qp_ref, kp_ref, vp_ref, sm_ref, cw_ref, lp_ref, conv_ref, first_block):
    tb = q_ref.shape[0]
    hd = GDN_HEAD_DIM
    nblk = tb // GDN_BLK

    keep_prev = jnp.where(first_block, 0.0, 1.0)
    cw = cw_ref[...]

    def conv_silu(x_ref, p_ref, idx):
        w = cw[:, idx * GW:(idx + 1) * GW]
        return _silu(_causal_conv_staged(x_ref, p_ref[...] * keep_prev, w, 4, conv_ref.at[idx]))

    q = conv_silu(q_ref, qp_ref, 0)
    yield
    k = conv_silu(k_ref, kp_ref, 1)
    yield
    v = conv_silu(v_ref, vp_ref, 2)
    yield

    small = sm_ref[...]
    a_log = lp_ref[0:1, :]
    dt_bias = lp_ref[1:2, :]
    beta_all = _sigmoid(small)
    g_all = -jnp.exp(a_log) * jax.nn.softplus(small + dt_bias)
    gc_all, gtot_all = _chunk_cumsum(g_all, GDN_CHUNK)
    gc_t = gc_all.T
    yield

    r = lax.broadcasted_iota(jnp.int32, (GDN_BLK, GDN_BLK), 0)
    s = lax.broadcasted_iota(jnp.int32, (GDN_BLK, GDN_BLK), 1)
    same_chunk = _blk(r, GDN_CHUNK) == _blk(s, GDN_CHUNK)
    mask_incl = same_chunk & (s <= r)
    mask_strict = same_chunk & (s < r)

    problems = [(slice(b * GDN_BLK, (b + 1) * GDN_BLK), h) for b in range(nblk) for h in range(GDN_HEADS)]
    head_cols = lambda h: slice(h * hd, (h + 1) * hd)
    gate = lambda arr, rows, lane: arr[rows, lane:lane + 1]

    qn = [_l2norm(q[rows, head_cols(h)]) * (hd ** -0.5) for rows, h in problems]
    kn = [_l2norm(k[rows, head_cols(h)]) for rows, h in problems]
    beta = [gate(beta_all, rows, LANE_BETA + h) for rows, h in problems]
    gcol = [gate(gc_all, rows, LANE_ALPHA + h) for rows, h in problems]
    gtot = [gate(gtot_all, rows, LANE_ALPHA + h) for rows, h in problems]
    grow = [gc_t[LANE_ALPHA + h:LANE_ALPHA + h + 1, rows] for rows, h in problems]
    yield

    decay = [jnp.exp(jnp.where(mask_incl, gc - gr, -jnp.inf)) for gc, gr in zip(gcol, grow)]
    k_beta = [kk * b for kk, b in zip(kn, beta)]
    k16 = [kk.astype(BF16) for kk in kn]
    yield
    ms = [jnp.where(mask_strict, _mm_nt(kb, kk) * d, 0.0) for kb, kk, d in zip(k_beta, k16, decay)]
    yield
    attn = [(_mm_nt(qq, kk) * d).astype(BF16) for qq, kk, d in zip(qn, k16, decay)]
    ys = yield from _unit_lower_inverse_minus_identity(ms, r, s)
    yield

    e_gc = [jnp.exp(gc) for gc in gcol]
    rhs = [jnp.concatenate([v[rows, head_cols(h)] * b, kb * e], axis=1)
           for (rows, h), b, kb, e in zip(problems, beta, k_beta, e_gc)]
    uw = [x + _mm(y, x) for y, x in zip(ys, rhs)]
    yield
    u = [x[:, :hd] for x in uw]
    w = [x[:, hd:].astype(BF16) for x in uw]
    q_dec = [(qq * e).astype(BF16) for qq, e in zip(qn, e_gc)]
    k_dec = [(kk * jnp.exp(gt - gc)).astype(BF16) for kk, gt, gc in zip(kn, gtot, gcol)]
    return u, w, q_dec, k_dec, attn, gtot_all


def _gdn_recurrence(u, w, q_dec, k_dec, attn, gtot_all, state_ref, z, nw_ref, o_ref):
    hd = GDN_HEAD_DIM
    nblk = o_ref.shape[0] // GDN_BLK
    chunks_per_blk = GDN_BLK // GDN_CHUNK
    head_cols = lambda h: slice(h * hd, (h + 1) * hd)

    states = [state_ref[h] for h in range(GDN_HEADS)]
    v_new = [[] for _ in range(nblk * GDN_HEADS)]
    o_inter = [[] for _ in range(nblk * GDN_HEADS)]
    for b in range(nblk):
        for c in range(chunks_per_blk):
            rows = slice(c * GDN_CHUNK, (c + 1) * GDN_CHUNK)
            first = b * GDN_BLK + c * GDN_CHUNK
            for h in range(GDN_HEADS):
                p = b * GDN_HEADS + h
                st16 = states[h].astype(BF16)
                vn = u(p, rows) - jnp.dot(w(p, rows), st16, preferred_element_type=F32)
                o_inter[p].append(jnp.dot(q_dec(p, rows), st16, preferred_element_type=F32))
                chunk_dec = jnp.exp(gtot_all[first:first + 1, LANE_ALPHA + h:LANE_ALPHA + h + 1])
                states[h] = states[h] * chunk_dec + _mm_tn(k_dec(p, rows), vn)
                v_new[p].append(vn)
            yield
        rows = slice(b * GDN_BLK, (b + 1) * GDN_BLK)
        for h in range(GDN_HEADS):
            p = b * GDN_HEADS + h
            o = jnp.concatenate(o_inter[p], axis=0) + _mm(attn(p, slice(None)), jnp.concatenate(v_new[p], axis=0))
            o = o * lax.rsqrt(jnp.mean(o * o, axis=-1, keepdims=True) + EPS) * nw_ref[...]
            o_ref[rows, head_cols(h)] = (o * _silu(z[rows, head_cols(h)])).astype(o_ref.dtype)
    for h in range(GDN_HEADS):
        state_ref[h] = states[h]


def _gdn_kernel(q_ref, k_ref, v_ref, z_ref, qp_ref, kp_ref, vp_ref, sm_ref, cw_ref, lp_ref, nw_ref,
                o_ref, state_ref, conv_ref, u_ref, w_ref, qd_ref, kd_ref, at_ref, gt_ref):
    step = pl.program_id(0)
    write_slot = step % 2
    read_slot = 1 - write_slot
    n_problems = u_ref.shape[1]

    @pl.when(step == 0)
    def _():
        state_ref[...] = jnp.zeros_like(state_ref)
        for ref in (u_ref, w_ref, qd_ref, kd_ref, at_ref, gt_ref):
            ref[1] = jnp.zeros(ref.shape[1:], ref.dtype)

    recurrence = _gdn_recurrence(*[(lambda p, rows, ref=ref: ref[read_slot, p, rows])
                                   for ref in (u_ref, w_ref, qd_ref, kd_ref, at_ref)],
                                 gt_ref[read_slot], state_ref, z_ref[...], nw_ref, o_ref)
    prepare = _gdn_prepare(q_ref, k_ref, v_ref, qp_ref, kp_ref, vp_ref, sm_ref, cw_ref, lp_ref, conv_ref, step == 0)
    u, w, q_dec, k_dec, attn, gtot_all = _run_interleaved(prepare, recurrence, GDN_PREP_STAGES_PER_CHUNK)
    for p in range(n_problems):
        u_ref[write_slot, p] = u[p]
        w_ref[write_slot, p] = w[p]
        qd_ref[write_slot, p] = q_dec[p]
        kd_ref[write_slot, p] = k_dec[p]
        at_ref[write_slot, p] = attn[p]
    gt_ref[write_slot] = gtot_all


def _prev_rows_spec(tb, width, slab):
    blocks = tb // SUBLANES
    return pl.BlockSpec((SUBLANES, width), lambda i: (jnp.maximum(i * blocks - 1, 0), slab))


def _lane_params(lane0, *rows):
    stacked = jnp.stack([r.astype(F32) for r in rows], axis=1)
    nrows, heads = stacked.shape[1:]
    return jnp.pad(stacked, ((0, 0), (0, SUBLANES - nrows), (lane0, LANES - lane0 - heads)))


def _gated_deltanet(proj, layer, conv_w, lane_params, norm_w):
    seq = proj.shape[0]
    tb = min(GDN_TB, seq)
    nsteps = seq // tb
    n_problems = (tb // GDN_BLK) * GDN_HEADS
    prep = lambda t: jnp.minimum(t, nsteps - 1)
    done = lambda t: jnp.maximum(t - 1, 0)
    slab = lambda j: pl.BlockSpec((tb, GW), lambda t: (prep(t), j))
    prev = lambda j: pl.BlockSpec((SUBLANES, GW),
                                  lambda t: (jnp.maximum(prep(t) * (tb // SUBLANES) - 1, 0), j))
    full = lambda shape: _layer_spec(shape, layer)
    factor = lambda dtype: pltpu.VMEM((2, n_problems, GDN_BLK, GDN_HEAD_DIM), dtype)
    return pl.pallas_call(
        _gdn_kernel,
        grid=(nsteps + 1,),
        in_specs=[
            slab(SLAB_GQ), slab(SLAB_GK), slab(SLAB_GV),
            pl.BlockSpec((tb, GW), lambda t: (done(t), SLAB_GZ)),
            prev(SLAB_GQ), prev(SLAB_GK), prev(SLAB_GV),
            pl.BlockSpec((tb, LANES), lambda t: (prep(t), SLAB128_SMALL)),
            full((4, 3 * GW)), full((SUBLANES, LANES)), full((1, GDN_HEAD_DIM)),
        ],
        out_specs=pl.BlockSpec((tb, GW), lambda t: (done(t), 0)),
        out_shape=jax.ShapeDtypeStruct((seq, GW), BF16),
        scratch_shapes=[pltpu.VMEM((GDN_HEADS, GDN_HEAD_DIM, GDN_HEAD_DIM), F32),
                        pltpu.VMEM((3, tb + SUBLANES, GW), F32),
                        factor(F32), factor(BF16), factor(BF16), factor(BF16), factor(BF16),
                        pltpu.VMEM((2, tb, LANES), F32)],
        compiler_params=_cparams("arbitrary"),
        name="gated_deltanet",
    )(proj, proj, proj, proj, proj, proj, proj, proj, conv_w, lane_params, norm_w)


SSD_TB = 256


def _ssd_kernel(z_ref, x_ref, bc_ref, xp_ref, bcp_ref, sm_ref, cw_ref, cb_ref, lp_ref, nw_ref,
                o_ref, state_ref):
    step = pl.program_id(0)
    tb = z_ref.shape[0]
    nstate = SSM_STATE
    chunk = SSM_CHUNK
    nchunks = tb // chunk
    n_pairs = SSM_HEADS // 2
    pairs_per_group = n_pairs // 2

    @pl.when(step == 0)
    def _():
        state_ref[...] = jnp.zeros_like(state_ref)

    keep_prev = jnp.where(step == 0, 0.0, 1.0)
    cw = cw_ref[...]
    cbias = cb_ref[...]
    xs = _silu(_causal_conv(x_ref[...], xp_ref[...] * keep_prev, cw[:, :GW], 4) + cbias[:, :GW])
    bc = _silu(_causal_conv(bc_ref[...], bcp_ref[...] * keep_prev, cw[:, GW:], 4) + cbias[:, GW:])
    z = z_ref[...]

    small = sm_ref[...]
    a_neg = -jnp.exp(lp_ref[0:1, :])
    dt_all = jax.nn.softplus(small + lp_ref[1:2, :])
    dskip_all = lp_ref[2:3, :]
    gc_all, gtot_all = _chunk_cumsum(dt_all * a_neg, chunk)
    gc_t = gc_all.T

    r = lax.broadcasted_iota(jnp.int32, (chunk, chunk), 0)
    s = lax.broadcasted_iota(jnp.int32, (chunk, chunk), 1)
    mask_incl = s <= r
    lane = lax.broadcasted_iota(jnp.int32, (1, LANES), 1)
    first_half = lane < SSM_HEAD_DIM
    by_half = lambda a, b: jnp.where(first_half, a, b)

    problems = [(c, p) for c in range(nchunks) for p in range(n_pairs)]
    rows_of = lambda c: slice(c * chunk, (c + 1) * chunk)
    cols_of = lambda p: slice(p * LANES, (p + 1) * LANES)
    lanes_of = lambda p: (LANE_DT + 2 * p, LANE_DT + 2 * p + 1)
    col = lambda arr, c, l: arr[rows_of(c), l:l + 1]

    b_mat = {(c, g): bc[rows_of(c), g * nstate:(g + 1) * nstate].astype(BF16)
             for c in range(nchunks) for g in range(2)}
    c_mat = {(c, g): bc[rows_of(c), (2 + g) * nstate:(3 + g) * nstate].astype(BF16)
             for c in range(nchunks) for g in range(2)}
    cb = {key: _mm_nt(c_mat[key], b_mat[key]) for key in b_mat}

    x_pair = [xs[rows_of(c), cols_of(p)] for c, p in problems]
    xdt = [x * by_half(col(dt_all, c, lanes_of(p)[0]), col(dt_all, c, lanes_of(p)[1]))
           for x, (c, p) in zip(x_pair, problems)]
    y_diag = []
    for (c, p), xd in zip(problems, xdt):
        halves = (by_half(xd, 0.0), by_half(0.0, xd))
        acc = None
        for l, half in zip(lanes_of(p), halves):
            grow = gc_t[l:l + 1, rows_of(c)]
            lmat = jnp.exp(jnp.where(mask_incl, col(gc_all, c, l) - grow, -jnp.inf))
            t = _mm(cb[c, p // pairs_per_group] * lmat, half)
            acc = t if acc is None else acc + t
        y_diag.append(acc)

    e_in = [by_half(jnp.exp(col(gc_all, c, lanes_of(p)[0])), jnp.exp(col(gc_all, c, lanes_of(p)[1])))
            for c, p in problems]
    e_out = [by_half(jnp.exp(col(gtot_all, c, lanes_of(p)[0]) - col(gc_all, c, lanes_of(p)[0])),
                     jnp.exp(col(gtot_all, c, lanes_of(p)[1]) - col(gc_all, c, lanes_of(p)[1])))
             for c, p in problems]
    states = [state_ref[p] for p in range(n_pairs)]
    y_off = []
    for idx, (c, p) in enumerate(problems):
        la, lb = lanes_of(p)
        g = p // pairs_per_group
        y_off.append(e_in[idx] * _mm(c_mat[c, g], states[p]))
        first = c * chunk
        cd_lane = by_half(jnp.exp(gtot_all[first:first + 1, la:la + 1]),
                          jnp.exp(gtot_all[first:first + 1, lb:lb + 1]))
        states[p] = states[p] * cd_lane + _mm_tn(b_mat[c, g], xdt[idx] * e_out[idx])
    for p in range(n_pairs):
        state_ref[p] = states[p]

    ys = []
    for idx, (c, p) in enumerate(problems):
        la, lb = lanes_of(p)
        y = y_diag[idx] + y_off[idx] + x_pair[idx] * by_half(dskip_all[:, la:la + 1], dskip_all[:, lb:lb + 1])
        ys.append(y * _silu(z[rows_of(c), cols_of(p)]))
    sumsq = [jnp.sum(y * y, axis=-1, keepdims=True) for y in ys]
    for c in range(nchunks):
        for g in range(2):
            members = [c * n_pairs + g * pairs_per_group + k for k in range(pairs_per_group)]
            total = sumsq[members[0]]
            for idx in members[1:]:
                total = total + sumsq[idx]
            inv = lax.rsqrt(total / (pairs_per_group * LANES) + EPS)
            for idx in members:
                cols = cols_of(problems[idx][1])
                o_ref[rows_of(c), cols] = (ys[idx] * inv * nw_ref[:, cols]).astype(o_ref.dtype)


def _mamba2_ssd(proj, layer, conv_w, conv_b, lane_params, norm_w):
    seq = proj.shape[0]
    tb = min(SSD_TB, seq)
    slab = lambda j: pl.BlockSpec((tb, GW), lambda i: (i, j))
    full = lambda shape: _layer_spec(shape, layer)
    return pl.pallas_call(
        _ssd_kernel,
        grid=(seq // tb,),
        in_specs=[
            slab(SLAB_SZ), slab(SLAB_SX), slab(SLAB_SBC),
            _prev_rows_spec(tb, GW, SLAB_SX), _prev_rows_spec(tb, GW, SLAB_SBC),
            pl.BlockSpec((tb, LANES), lambda i: (i, SLAB128_SMALL)),
            full((4, 2 * GW)), full((1, 2 * GW)), full((SUBLANES, LANES)), full((1, GW)),
        ],
        out_specs=pl.BlockSpec((tb, GW), lambda i: (i, 0)),
        out_shape=jax.ShapeDtypeStruct((seq, GW), BF16),
        scratch_shapes=[pltpu.VMEM((SSM_HEADS // 2, SSM_STATE, LANES), F32)],
        compiler_params=_cparams("arbitrary"),
        name="mamba2_ssd",
    )(proj, proj, proj, proj, proj, proj, conv_w, conv_b, lane_params, norm_w)


def _rotate_half(x, lane_in_head_low):
    width = x.shape[1]
    half = ATTN_HEAD_DIM // 2
    ahead = pltpu.roll(x, width - half, axis=1)
    behind = pltpu.roll(x, half, axis=1)
    return jnp.where(lane_in_head_low, ahead, behind)


SWA_TB = 256


def _swa_kernel(q_ref, k_ref, v_ref, cos_ref, sin_ref, sink_ref, o_ref, kprev_ref, vprev_ref):
    step = pl.program_id(0)
    tb = q_ref.shape[0]
    win = WINDOW
    hd = ATTN_HEAD_DIM

    @pl.when(step == 0)
    def _():
        kprev_ref[...] = jnp.zeros_like(kprev_ref)
        vprev_ref[...] = jnp.zeros_like(vprev_ref)

    cos = cos_ref[...]
    sin = sin_ref[...]
    lane = lax.broadcasted_iota(jnp.int32, (1, LANES), 1)
    low = (lane & (hd - 1)) < (hd // 2)
    first_half = lane < hd

    k_cur = k_ref[...]
    k_cur = k_cur * cos + _rotate_half(k_cur, low) * sin
    v_cur = v_ref[...]
    k_all = jnp.concatenate([kprev_ref[...], k_cur], axis=0)
    v_all = jnp.concatenate([vprev_ref[...], v_cur], axis=0)
    k_swap = pltpu.roll(k_all, hd, axis=1)
    v_swap = pltpu.roll(v_all, hd, axis=1)

    def placed(x, x_swap, kv_head, pos):
        src = x if kv_head == pos else x_swap
        keep = first_half if pos == 0 else ~first_half
        return jnp.where(keep, src, 0.0).astype(BF16)

    qi = lax.broadcasted_iota(jnp.int32, (win, 2 * win), 0)
    kj = lax.broadcasted_iota(jnp.int32, (win, 2 * win), 1)
    rel = qi + win - kj
    band = (rel >= 0) & (rel < win)
    band_first = band & ((kj >= win) | (step > 0))

    problems = [(j, h) for j in range(tb // win) for h in range(ATTN_Q_HEADS)]
    kv_of = lambda head: head // (ATTN_Q_HEADS // 2)
    pair_cols = lambda pair: slice(pair * LANES, (pair + 1) * LANES)
    k_at = {(g, pos): placed(k_all, k_swap, g, pos) for g in range(2) for pos in range(2)}
    v_at = {(g, pos): placed(v_all, v_swap, g, pos) for g in range(2) for pos in range(2)}
    q_pairs = []
    for pair in range(ATTN_Q_HEADS // 2):
        q_pair = q_ref[:, pair_cols(pair)]
        q_pairs.append((q_pair * cos + _rotate_half(q_pair, low) * sin).astype(BF16))
    sinks = [sink_ref[h:h + 1, 0:1] for _, h in problems]
    sc = [jnp.where(band_first if j == 0 else band,
                    _mm_nt(q_pairs[h // 2][j * win:(j + 1) * win],
                           k_at[kv_of(h), h % 2][j * win:(j + 2) * win]) * (hd ** -0.5),
                    -jnp.inf)
          for j, h in problems]
    mx = [jnp.maximum(jnp.max(s_, axis=-1, keepdims=True), sk) for s_, sk in zip(sc, sinks)]
    p = [jnp.exp(s_ - m_) for s_, m_ in zip(sc, mx)]
    denom = [jnp.sum(p_, axis=-1, keepdims=True) + jnp.exp(sk - m_) for p_, sk, m_ in zip(p, sinks, mx)]
    o = [_mm(p_, v_at[kv_of(h), h % 2][j * win:(j + 2) * win]) / d_
         for (j, h), p_, d_ in zip(problems, p, denom)]
    for idx in range(0, len(problems), 2):
        j, h = problems[idx]
        o_ref[j * win:(j + 1) * win, pair_cols(h // 2)] = (o[idx] + o[idx + 1]).astype(o_ref.dtype)

    kprev_ref[...] = k_cur[tb - win:]
    vprev_ref[...] = v_cur[tb - win:]


def _sliding_window_attention(proj, layer, cos, sin, sink_rows):
    seq = proj.shape[0]
    tb = min(SWA_TB, seq)
    return pl.pallas_call(
        _swa_kernel,
        grid=(seq // tb,),
        in_specs=[
            pl.BlockSpec((tb, GW), lambda i: (i, SLAB_AQ)),
            pl.BlockSpec((tb, LANES), lambda i: (i, SLAB128_AK)),
            pl.BlockSpec((tb, LANES), lambda i: (i, SLAB128_AV)),
            pl.BlockSpec((tb, LANES), lambda i: (i, 0)),
            pl.BlockSpec((tb, LANES), lambda i: (i, 0)),
            _layer_spec((ATTN_Q_HEADS, LANES), layer),
        ],
        out_specs=pl.BlockSpec((tb, GW), lambda i: (i, 0)),
        out_shape=jax.ShapeDtypeStruct((seq, GW), BF16),
        scratch_shapes=[pltpu.VMEM((WINDOW, LANES), F32), pltpu.VMEM((WINDOW, LANES), F32)],
        compiler_params=_cparams("arbitrary"),
        name="sliding_window_attn",
    )(proj, proj, proj, cos, sin, sink_rows)


SCONV_TB = 512


def _sconv_kernel(b_ref, c_ref, h_ref, cp_ref, hp_ref, w_ref, o_ref):
    keep_prev = jnp.where(pl.program_id(0) == 0, 0.0, 1.0)
    ch = c_ref[...] * h_ref[...]
    ch_prev = cp_ref[...] * hp_ref[...] * keep_prev
    o_ref[...] = (b_ref[...] * _causal_conv(ch, ch_prev, w_ref[...], 3)).astype(o_ref.dtype)


def _short_conv(proj, layer, conv_w):
    seq = proj.shape[0]
    tb = min(SCONV_TB, seq)
    slab = lambda j: pl.BlockSpec((tb, GW), lambda i: (i, j))
    return pl.pallas_call(
        _sconv_kernel,
        grid=(seq // tb,),
        in_specs=[
            slab(SLAB_CB), slab(SLAB_CC), slab(SLAB_CH),
            _prev_rows_spec(tb, GW, SLAB_CC), _prev_rows_spec(tb, GW, SLAB_CH),
            _layer_spec((3, GW), layer),
        ],
        out_specs=pl.BlockSpec((tb, GW), lambda i: (i, 0)),
        out_shape=jax.ShapeDtypeStruct((seq, GW), BF16),
        compiler_params=_cparams("parallel"),
        name="short_conv",
    )(proj, proj, proj, proj, proj, conv_w)


OUTPROJ_TM = 512


def _outproj_kernel(ya_ref, yb_ref, yc_ref, yd_ref, w_ref, x_ref, gate_ref, nw_ref, o_ref):
    y = jnp.concatenate([ya_ref[...], yb_ref[...], yc_ref[...], yd_ref[...]], axis=1)
    y = jnp.dot(y, w_ref[...], preferred_element_type=F32)
    y = y * lax.rsqrt(jnp.mean(y * y, axis=-1, keepdims=True) + EPS) * nw_ref[...]
    o_ref[...] = x_ref[...] + gate_ref[...] * y


def _out_projection(ys, layer, w_bf16, x, mod, norm_w):
    seq, d = x.shape
    tm = min(OUTPROJ_TM, seq)
    return pl.pallas_call(
        _outproj_kernel,
        grid=(seq // tm,),
        in_specs=[pl.BlockSpec((tm, GW), lambda i: (i, 0))] * 4 + [
            _layer_spec((4 * GW, d), layer),
            pl.BlockSpec((tm, d), lambda i: (i, 0)),
            _layer_spec((1, d), layer, 0, MOD_GATE_A),
            _layer_spec((1, d), layer),
        ],
        out_specs=pl.BlockSpec((tm, d), lambda i: (i, 0)),
        out_shape=jax.ShapeDtypeStruct((seq, d), F32),
        compiler_params=_cparams("parallel"),
        name="out_proj",
    )(*ys, w_bf16, x, mod, norm_w)


MLP_TM = 1024
MLP_TH = 512


def _mlp_kernel(x_ref, nw_ref, sc_ref, sh_ref, wu_ref, wd_ref, gate_ref, pw_ref, o_ref, h_ref):
    j = pl.program_id(1)

    @pl.when(j == 0)
    def _():
        _prenorm_to(x_ref, nw_ref, sc_ref, sh_ref, h_ref)
        o_ref[...] = jnp.zeros_like(o_ref)

    hid = _mm(h_ref[...], wu_ref[...])
    hid = jnp.square(jnp.maximum(hid, 0.0))
    o_ref[...] += _mm(hid, wd_ref[...])

    @pl.when(j == pl.num_programs(1) - 1)
    def _():
        def body(r, carry):
            rows = pl.ds(pl.multiple_of(r * NORM_ROWS, NORM_ROWS), NORM_ROWS)
            y = o_ref[rows, :]
            y = y * lax.rsqrt(jnp.mean(y * y, axis=-1, keepdims=True) + EPS) * pw_ref[...]
            o_ref[rows, :] = x_ref[rows, :] + gate_ref[...] * y
            return carry

        lax.fori_loop(0, x_ref.shape[0] // NORM_ROWS, body, 0)


def _mlp(x, layer, norm_w, mod, w_up, w_down, post_w):
    seq, d = x.shape
    hidden = w_up.shape[2]
    tm = min(MLP_TM, seq)
    vmem = (2 * 2 * tm * d * 4 + tm * d * 2 + 2 * 2 * d * MLP_TH * w_up.dtype.itemsize
            + 2 * d * MLP_TH * 2 + tm * MLP_TH * 6 + 2 * MIB)
    return pl.pallas_call(
        _mlp_kernel,
        grid=(seq // tm, hidden // MLP_TH),
        in_specs=[
            pl.BlockSpec((tm, d), lambda i, j: (i, 0)),
            _layer_spec((1, d), layer),
            _layer_spec((1, d), layer, 0, MOD_SCALE_M),
            _layer_spec((1, d), layer, 0, MOD_SHIFT_M),
            _layer_spec((d, MLP_TH), layer, 0, lambda i, j: j),
            _layer_spec((MLP_TH, d), layer, lambda i, j: j, 0),
            _layer_spec((1, d), layer, 0, MOD_GATE_M),
            _layer_spec((1, d), layer),
        ],
        out_specs=pl.BlockSpec((tm, d), lambda i, j: (i, 0)),
        out_shape=jax.ShapeDtypeStruct((seq, d), F32),
        scratch_shapes=[pltpu.VMEM((tm, d), BF16)],
        compiler_params=_cparams("parallel", "arbitrary", vmem_limit=vmem),
        name="mlp",
    )(x, norm_w, mod, mod, w_up, w_down, mod, post_w)


def _gathered_group(w_t):
    parts = [w_t[:, IN_OFFSET[name][0]:IN_OFFSET[name][1]] for name in GATHERED_GROUP]
    used = sum(p.shape[1] for p in parts)
    parts.append(jnp.zeros((w_t.shape[0], IN_GROUP - used, w_t.shape[2]), w_t.dtype))
    return jnp.concatenate(parts, axis=1)


def kernel(x, c, positions, ada_w, ada_b, norm_pre_mix, norm_post_mix, norm_pre_mlp, norm_post_mlp,
           w_in, w_out, gdn_conv_w, gdn_a_log, gdn_dt_bias, gdn_norm_w, ssm_conv_w, ssm_conv_b,
           ssm_a_log, ssm_dt_bias, ssm_d, ssm_norm_w, attn_sinks, sc_conv_w, w_up, w_down):
    batch, seq, d = x.shape
    assert batch == 1 and d == D_MODEL
    depth = ada_w.shape[0]
    xs = x.reshape(seq, d)

    mod = _modulation(c.reshape(d, 1), ada_w, ada_b)
    cos, sin = _rope_tables(positions.reshape(seq, 1))

    rows = lambda v: v.reshape(depth, 1, -1)
    w_in_t = jnp.swapaxes(w_in, 1, 2).astype(BF16)
    w_in_tail = _gathered_group(w_in_t)
    w_out_bf16 = w_out.astype(BF16)
    gdn_lanes = _lane_params(LANE_ALPHA, gdn_a_log, gdn_dt_bias)
    ssm_lanes = _lane_params(LANE_DT, ssm_a_log, ssm_dt_bias, ssm_d)
    sink_rows = jnp.broadcast_to(attn_sinks.astype(F32)[:, :, None], (depth, ATTN_Q_HEADS, LANES))

    for i in range(depth):
        proj = _in_projection(xs, i, rows(norm_pre_mix), mod, w_in_t, w_in_tail)
        y_a = _gated_deltanet(proj, i, gdn_conv_w, gdn_lanes, rows(gdn_norm_w))
        y_b = _mamba2_ssd(proj, i, ssm_conv_w, rows(ssm_conv_b), ssm_lanes, rows(ssm_norm_w))
        y_c = _sliding_window_attention(proj, i, cos, sin, sink_rows)
        y_d = _short_conv(proj, i, sc_conv_w)
        xs = _out_projection((y_a, y_b, y_c, y_d), i, w_out_bf16, xs, mod, rows(norm_post_mix))
        xs = _mlp(xs, i, rows(norm_pre_mlp), mod, w_up, w_down, rows(norm_post_mlp))
    return xs.reshape(batch, seq, d)
```

```python
import functools

import jax
import jax.numpy as jnp
from jax import lax
from jax.experimental import pallas as pl
from jax.experimental.pallas import tpu as pltpu

F32 = jnp.float32
BF16 = jnp.bfloat16

D_MODEL = 2048
GW = 512
GDN_HEADS, GDN_HEAD_DIM, GDN_CHUNK = 4, 128, 64
SSM_HEADS, SSM_HEAD_DIM, SSM_STATE, SSM_CHUNK = 8, 64, 128, 128
ATTN_HEAD_DIM, ATTN_Q_HEADS, WINDOW = 64, 8, 128
ROPE_THETA = 10000.0
MLP_HIDDEN = 4 * D_MODEL
EPS = 1e-6

SUBLANES = 8
LANES = 128

_IN_SEGMENTS = (("gq", GW), ("gk", GW), ("gv", GW), ("gz", GW), ("gb", GDN_HEADS), ("ga", GDN_HEADS),
                ("sz", GW), ("sx", GW), ("sbc", 4 * SSM_STATE), ("sdt", SSM_HEADS),
                ("aq", GW), ("ak", 2 * ATTN_HEAD_DIM), ("av", 2 * ATTN_HEAD_DIM),
                ("cb", GW), ("cc", GW), ("ch", GW))
IN_OFFSET = {}
_off = 0
for _name, _size in _IN_SEGMENTS:
    IN_OFFSET[_name] = (_off, _off + _size)
    _off += _size
IN_WIDTH = _off

IN_GROUP = 3 * GW
CONTIGUOUS_GROUPS = (("gq", "gk", "gv"), ("sz", "sx", "sbc"), ("cb", "cc", "ch"))
GATHERED_GROUP = ("gz", "aq", "ak", "av", "gb", "ga", "sdt")
IN_COLS = (len(CONTIGUOUS_GROUPS) + 1) * IN_GROUP
for _grp in CONTIGUOUS_GROUPS:
    assert IN_OFFSET[_grp[0]][0] % SUBLANES == 0
    assert all(IN_OFFSET[a][1] == IN_OFFSET[b][0] for a, b in zip(_grp, _grp[1:]))
(SLAB_GQ, SLAB_GK, SLAB_GV, SLAB_SZ, SLAB_SX, SLAB_SBC, SLAB_CB, SLAB_CC, SLAB_CH, SLAB_GZ, SLAB_AQ) = range(11)
SLAB128_AK, SLAB128_AV, SLAB128_SMALL = 44, 45, 46
LANE_BETA, LANE_ALPHA, LANE_DT = 0, 4, 8

MIB = 1024 * 1024
V7X_VMEM_BYTES = 64 * MIB
VMEM_LIMIT = 56 * MIB


def _cparams(*sem, vmem_limit=VMEM_LIMIT):
    assert vmem_limit < V7X_VMEM_BYTES
    return pltpu.CompilerParams(dimension_semantics=sem, vmem_limit_bytes=vmem_limit)


def _sigmoid(x):
    return jax.nn.sigmoid(x)


def _silu(x):
    return x * jax.nn.sigmoid(x)


def _mm(a, b):
    return jnp.dot(a.astype(BF16), b.astype(BF16), preferred_element_type=F32)


def _mm_nt(a, b):
    return lax.dot_general(a.astype(BF16), b.astype(BF16), (((1,), (1,)), ((), ())),
                           preferred_element_type=F32)


def _mm_tn(a, b):
    return lax.dot_general(a.astype(BF16), b.astype(BF16), (((0,), (0,)), ((), ())),
                           preferred_element_type=F32)


def _blk(idx, size):
    assert size & (size - 1) == 0
    return lax.shift_right_logical(idx, size.bit_length() - 1)


def _split3(x):
    x1 = x.astype(BF16)
    r1 = x - x1.astype(F32)
    x2 = r1.astype(BF16)
    r2 = r1 - x2.astype(F32)
    return x1, x2, r2.astype(BF16)


def _chunk_cumsum(g, chunk):
    rows = g.shape[0]
    r = lax.broadcasted_iota(jnp.int32, (2 * rows, rows), 0)
    s = lax.broadcasted_iota(jnp.int32, (2 * rows, rows), 1)
    rr = jnp.where(r >= rows, r - rows, r)
    same = _blk(rr, chunk) == _blk(s, chunk)
    sel = same & ((r >= rows) | (s <= rr))
    mat = jnp.where(sel, 1.0, 0.0).astype(BF16)
    acc = None
    for part in _split3(g):
        t = jnp.dot(mat, part, preferred_element_type=F32)
        acc = t if acc is None else acc + t
    return acc[:rows], acc[rows:]


def _causal_conv(x, prev, w, taps):
    rows = x.shape[0]
    xp = jnp.concatenate([prev, x], axis=0)
    acc = x * w[taps - 1:taps]
    for d in range(1, taps):
        acc = acc + xp[SUBLANES - d:SUBLANES - d + rows] * w[taps - 1 - d:taps - d]
    return acc


def _causal_conv_staged(x_ref, prev, w, taps, stage_ref):
    rows = x_ref.shape[0]
    x = x_ref[...]
    stage_ref[0:SUBLANES, :] = prev
    stage_ref[SUBLANES:, :] = x
    acc = x * w[taps - 1:taps]
    for d in range(1, taps):
        acc = acc + stage_ref[pl.ds(SUBLANES - d, rows), :] * w[taps - 1 - d:taps - d]
    return acc


MOD_TN = 2048
MOD_ROWS = 64


def _mod_kernel(c_ref, w_ref, b_ref, o_ref):
    d = c_ref.shape[0]

    def body(r, acc):
        rows = pl.ds(pl.multiple_of(r * MOD_ROWS, MOD_ROWS), MOD_ROWS)
        cc = _silu(c_ref[rows, :])
        prod = w_ref[0, rows, :] * cc
        return acc + jnp.sum(prod.reshape(MOD_ROWS // SUBLANES, SUBLANES, MOD_TN), axis=0)

    acc = lax.fori_loop(0, d // MOD_ROWS, body, jnp.zeros((SUBLANES, MOD_TN), F32))
    o_ref[0] = jnp.sum(acc, axis=0, keepdims=True) + b_ref[0]


def _modulation(c_col, ada_w, ada_b):
    depth, d, n = ada_w.shape
    out = pl.pallas_call(
        _mod_kernel,
        grid=(depth, n // MOD_TN),
        in_specs=[
            pl.BlockSpec((d, 1), lambda i, j: (0, 0)),
            pl.BlockSpec((1, d, MOD_TN), lambda i, j: (i, 0, j)),
            pl.BlockSpec((1, 1, MOD_TN), lambda i, j: (i, 0, j)),
        ],
        out_specs=pl.BlockSpec((1, 1, MOD_TN), lambda i, j: (i, 0, j)),
        out_shape=jax.ShapeDtypeStruct((depth, 1, n), F32),
        compiler_params=_cparams("parallel", "parallel"),
        name="adaln_mod",
    )(c_col, ada_w, ada_b.reshape(depth, 1, n))
    return out


ROPE_TB = 1024


def _rope_kernel(pos_ref, invf_ref, sign_ref, cos_ref, sin_ref):
    ang = pos_ref[...].astype(F32) * invf_ref[...]
    cos_ref[...] = jnp.cos(ang)
    sin_ref[...] = jnp.sin(ang) * sign_ref[...]


def _rope_tables(pos_col):
    seq = pos_col.shape[0]
    half = ATTN_HEAD_DIM // 2
    inv_freq = ROPE_THETA ** (-jnp.arange(0, ATTN_HEAD_DIM, 2, dtype=F32) / ATTN_HEAD_DIM)
    invf = jnp.tile(inv_freq, LANES // half).reshape(1, LANES)
    lane = jnp.arange(LANES)
    sign = jnp.where((lane % ATTN_HEAD_DIM) < half, -1.0, 1.0).astype(F32).reshape(1, LANES)
    tb = min(ROPE_TB, seq)
    return pl.pallas_call(
        _rope_kernel,
        grid=(seq // tb,),
        in_specs=[
            pl.BlockSpec((tb, 1), lambda i: (i, 0)),
            pl.BlockSpec((1, LANES), lambda i: (0, 0)),
            pl.BlockSpec((1, LANES), lambda i: (0, 0)),
        ],
        out_specs=[pl.BlockSpec((tb, LANES), lambda i: (i, 0))] * 2,
        out_shape=[jax.ShapeDtypeStruct((seq, LANES), F32)] * 2,
        compiler_params=_cparams("parallel"),
        name="rope_tables",
    )(pos_col, invf, sign)


NORM_ROWS = 128


def _prenorm_to(x_ref, nw_ref, sc_ref, sh_ref, h_ref):
    tm = x_ref.shape[0]

    def body(r, carry):
        rows = pl.ds(pl.multiple_of(r * NORM_ROWS, NORM_ROWS), NORM_ROWS)
        x = x_ref[rows, :]
        y = x * lax.rsqrt(jnp.mean(x * x, axis=-1, keepdims=True) + EPS) * nw_ref[...]
        h_ref[rows, :] = (y * (1.0 + sc_ref[...]) + sh_ref[...]).astype(BF16)
        return carry

    lax.fori_loop(0, tm // NORM_ROWS, body, 0)


INPROJ_TM = 1024

MOD_SHIFT_A, MOD_SCALE_A, MOD_GATE_A, MOD_SHIFT_M, MOD_SCALE_M, MOD_GATE_M = range(6)


def _layer_spec(shape, layer, *rest):
    rest = rest or (0,) * len(shape)

    def index_map(*grid_idx):
        return (layer,) + tuple(r(*grid_idx) if callable(r) else r for r in rest)

    return pl.BlockSpec((None,) + tuple(shape), index_map)


def _inproj_kernel(x_ref, nw_ref, sc_ref, sh_ref, wt_ref, tail_ref, o_ref, h_ref):
    j = pl.program_id(1)

    @pl.when(j == 0)
    def _():
        _prenorm_to(x_ref, nw_ref, sc_ref, sh_ref, h_ref)

    def project(w_ref):
        o_ref[...] = _mm_nt(h_ref[...], w_ref[...])

    @pl.when(j < len(CONTIGUOUS_GROUPS))
    def _():
        project(wt_ref)

    @pl.when(j == len(CONTIGUOUS_GROUPS))
    def _():
        project(tail_ref)


def _group_row(j):
    starts = [IN_OFFSET[grp[0]][0] // SUBLANES for grp in CONTIGUOUS_GROUPS]
    row8 = starts[-1]
    for k in range(len(starts) - 2, -1, -1):
        row8 = jnp.where(j <= k, starts[k], row8)
    return row8 * SUBLANES


def _in_projection(x, layer, norm_w, mod, w_t, w_gathered):
    seq, d = x.shape
    tm = min(INPROJ_TM, seq)
    return pl.pallas_call(
        _inproj_kernel,
        grid=(seq // tm, IN_COLS // IN_GROUP),
        in_specs=[
            pl.BlockSpec((tm, d), lambda i, j: (i, 0)),
            _layer_spec((1, d), layer),
            _layer_spec((1, d), layer, 0, MOD_SCALE_A),
            _layer_spec((1, d), layer, 0, MOD_SHIFT_A),
            pl.BlockSpec((None, pl.Element(IN_GROUP), pl.Element(d)), lambda i, j: (layer, _group_row(j), 0)),
            pl.BlockSpec((None, IN_GROUP, d), lambda i, j: (layer, 0, 0), pipeline_mode=pl.Buffered(1)),
        ],
        out_specs=pl.BlockSpec((tm, IN_GROUP), lambda i, j: (i, j)),
        out_shape=jax.ShapeDtypeStruct((seq, IN_COLS), F32),
        scratch_shapes=[pltpu.VMEM((tm, d), BF16)],
        compiler_params=_cparams("parallel", "arbitrary"),
        name="in_proj",
    )(x, norm_w, mod, mod, w_t, w_gathered)


GDN_TB = 512


def _l2norm(x):
    return x * lax.rsqrt(jnp.sum(x * x, axis=-1, keepdims=True) + EPS)


GDN_BLK = 128
GDN_PREP_STAGES_PER_CHUNK = 4


def _unit_lower_inverse_minus_identity(ms, r, s):
    same8 = _blk(r, 8) == _blk(s, 8)
    same16 = _blk(r, 16) == _blk(s, 16)
    same32 = _blk(r, 32) == _blk(s, 32)
    n1 = [jnp.where(same8, -m, 0.0) for m in ms]
    n2 = [_mm(a, a) for a in n1]
    yield
    n3 = [_mm(a, b) for a, b in zip(n1, n2)]
    n4 = [_mm(b, b) for b in n2]
    yield
    ys = [a + b + c for a, b, c in zip(n1, n2, n3)]
    ts = [_mm(y, d) for y, d in zip(ys, n4)]
    yield
    ys = [y + d + t for y, d, t in zip(ys, n4, ts)]
    for sel in (same16 & ~same8, same32 & ~same16, ~same32):
        blks = [jnp.where(sel, m, 0.0) for m in ms]
        cs = [b + _mm(y, b) for y, b in zip(ys, blks)]
        yield
        ts = [_mm(c, y) for c, y in zip(cs, ys)]
        yield
        ys = [y - c - t for y, c, t in zip(ys, cs, ts)]
    return ys


def _run_interleaved(main, side, main_stages_per_side_stage):
    side_live = True
    done = 0
    while True:
        try:
            next(main)
        except StopIteration as stop:
            result = stop.value
            break
        done += 1
        if side_live and done % main_stages_per_side_stage == 0:
            side_live = next(side, StopIteration) is not StopIteration
    while side_live:
        side_live = next(side, StopIteration) is not StopIteration
    return result


def _gdn_prepare(q_ref, k_ref, v_ref, qp_ref, kp_ref, vp_ref, sm_ref, cw_ref, lp_ref, conv_ref, first_block):
    tb = q_ref.shape[0]
    hd = GDN_HEAD_DIM
    nblk = tb // GDN_BLK

    keep_prev = jnp.where(first_block, 0.0, 1.0)
    cw = cw_ref[...]

    def conv_silu(x_ref, p_ref, idx):
        w = cw[:, idx * GW:(idx + 1) * GW]
        return _silu(_causal_conv_staged(x_ref, p_ref[...] * keep_prev, w, 4, conv_ref.at[idx]))

    q = conv_silu(q_ref, qp_ref, 0)
    yield
    k = conv_silu(k_ref, kp_ref, 1)
    yield
    v = conv_silu(v_ref, vp_ref, 2)
    yield

    small = sm_ref[...]
    a_log = lp_ref[0:1, :]
    dt_bias = lp_ref[1:2, :]
    beta_all = _sigmoid(small)
    g_all = -jnp.exp(a_log) * jax.nn.softplus(small + dt_bias)
    gc_all, gtot_all = _chunk_cumsum(g_all, GDN_CHUNK)
    gc_t = gc_all.T
    yield

    r = lax.broadcasted_iota(jnp.int32, (GDN_BLK, GDN_BLK), 0)
    s = lax.broadcasted_iota(jnp.int32, (GDN_BLK, GDN_BLK), 1)
    same_chunk = _blk(r, GDN_CHUNK) == _blk(s, GDN_CHUNK)
    mask_incl = same_chunk & (s <= r)
    mask_strict = same_chunk & (s < r)

    problems = [(slice(b * GDN_BLK, (b + 1) * GDN_BLK), h) for b in range(nblk) for h in range(GDN_HEADS)]
    head_cols = lambda h: slice(h * hd, (h + 1) * hd)
    gate = lambda arr, rows, lane: arr[rows, lane:lane + 1]

    qn = [_l2norm(q[rows, head_cols(h)]) * (hd ** -0.5) for rows, h in problems]
    kn = [_l2norm(k[rows, head_cols(h)]) for rows, h in problems]
    beta = [gate(beta_all, rows, LANE_BETA + h) for rows, h in problems]
    gcol = [gate(gc_all, rows, LANE_ALPHA + h) for rows, h in problems]
    gtot = [gate(gtot_all, rows, LANE_ALPHA + h) for rows, h in problems]
    grow = [gc_t[LANE_ALPHA + h:LANE_ALPHA + h + 1, rows] for rows, h in problems]
    yield

    decay = [jnp.exp(jnp.where(mask_incl, gc - gr, -jnp.inf)) for gc, gr in zip(gcol, grow)]
    k_beta = [kk * b for kk, b in zip(kn, beta)]
    k16 = [kk.astype(BF16) for kk in kn]
    yield
    ms = [jnp.where(mask_strict, _mm_nt(kb, kk) * d, 0.0) for kb, kk, d in zip(k_beta, k16, decay)]
    yield
    attn = [(_mm_nt(qq, kk) * d).astype(BF16) for qq, kk, d in zip(qn, k16, decay)]
    ys = yield from _unit_lower_inverse_minus_identity(ms, r, s)
    yield

    e_gc = [jnp.exp(gc) for gc in gcol]
    rhs = [jnp.concatenate([v[rows, head_cols(h)] * b, kb * e], axis=1)
           for (rows, h), b, kb, e in zip(problems, beta, k_beta, e_gc)]
    uw = [x + _mm(y, x) for y, x in zip(ys, rhs)]
    yield
    u = [x[:, :hd] for x in uw]
    w = [x[:, hd:].astype(BF16) for x in uw]
    q_dec = [(qq * e).astype(BF16) for qq, e in zip(qn, e_gc)]
    k_dec = [(kk * jnp.exp(gt - gc)).astype(BF16) for kk, gt, gc in zip(kn, gtot, gcol)]
    return u, w, q_dec, k_dec, attn, gtot_all


def _gdn_recurrence(u, w, q_dec, k_dec, attn, gtot_all, state_ref, z, nw_ref, o_ref):
    hd = GDN_HEAD_DIM
    nblk = o_ref.shape[0] // GDN_BLK
    chunks_per_blk = GDN_BLK // GDN_CHUNK
    head_cols = lambda h: slice(h * hd, (h + 1) * hd)

    states = [state_ref[h] for h in range(GDN_HEADS)]
    v_new = [[] for _ in range(nblk * GDN_HEADS)]
    o_inter = [[] for _ in range(nblk * GDN_HEADS)]
    for b in range(nblk):
        for c in range(chunks_per_blk):
            rows = slice(c * GDN_CHUNK, (c + 1) * GDN_CHUNK)
            first = b * GDN_BLK + c * GDN_CHUNK
            for h in range(GDN_HEADS):
                p = b * GDN_HEADS + h
                st16 = states[h].astype(BF16)
                vn = u(p, rows) - jnp.dot(w(p, rows), st16, preferred_element_type=F32)
                o_inter[p].append(jnp.dot(q_dec(p, rows), st16, preferred_element_type=F32))
                chunk_dec = jnp.exp(gtot_all[first:first + 1, LANE_ALPHA + h:LANE_ALPHA + h + 1])
                states[h] = states[h] * chunk_dec + _mm_tn(k_dec(p, rows), vn)
                v_new[p].append(vn)
            yield
        rows = slice(b * GDN_BLK, (b + 1) * GDN_BLK)
        for h in range(GDN_HEADS):
            p = b * GDN_HEADS + h
            o = jnp.concatenate(o_inter[p], axis=0) + _mm(attn(p, slice(None)), jnp.concatenate(v_new[p], axis=0))
            o = o * lax.rsqrt(jnp.mean(o * o, axis=-1, keepdims=True) + EPS) * nw_ref[...]
            o_ref[rows, head_cols(h)] = (o * _silu(z[rows, head_cols(h)])).astype(o_ref.dtype)
    for h in range(GDN_HEADS):
        state_ref[h] = states[h]


def _gdn_kernel(q_ref, k_ref, v_ref, z_ref, qp_ref, kp_ref, vp_ref, sm_ref, cw_ref, lp_ref, nw_ref,
                o_ref, state_ref, conv_ref, u_ref, w_ref, qd_ref, kd_ref, at_ref, gt_ref):
    step = pl.program_id(0)
    write_slot = step % 2
    read_slot = 1 - write_slot
    n_problems = u_ref.shape[1]

    @pl.when(step == 0)
    def _():
        state_ref[...] = jnp.zeros_like(state_ref)
        for ref in (u_ref, w_ref, qd_ref, kd_ref, at_ref, gt_ref):
            ref[1] = jnp.zeros(ref.shape[1:], ref.dtype)

    recurrence = _gdn_recurrence(*[(lambda p, rows, ref=ref: ref[read_slot, p, rows])
                                   for ref in (u_ref, w_ref, qd_ref, kd_ref, at_ref)],
                                 gt_ref[read_slot], state_ref, z_ref[...], nw_ref, o_ref)
    prepare = _gdn_prepare(q_ref, k_ref, v_ref, qp_ref, kp_ref, vp_ref, sm_ref, cw_ref, lp_ref, conv_ref, step == 0)
    u, w, q_dec, k_dec, attn, gtot_all = _run_interleaved(prepare, recurrence, GDN_PREP_STAGES_PER_CHUNK)
    for p in range(n_problems):
        u_ref[write_slot, p] = u[p]
        w_ref[write_slot, p] = w[p]
        qd_ref[write_slot, p] = q_dec[p]
        kd_ref[write_slot, p] = k_dec[p]
        at_ref[write_slot, p] = attn[p]
    gt_ref[write_slot] = gtot_all


def _prev_rows_spec(tb, width, slab):
    blocks = tb // SUBLANES
    return pl.BlockSpec((SUBLANES, width), lambda i: (jnp.maximum(i * blocks - 1, 0), slab))


def _lane_params(lane0, *rows):
    stacked = jnp.stack([r.astype(F32) for r in rows], axis=1)
    nrows, heads = stacked.shape[1:]
    return jnp.pad(stacked, ((0, 0), (0, SUBLANES - nrows), (lane0, LANES - lane0 - heads)))


def _gated_deltanet(proj, layer, conv_w, lane_params, norm_w):
    seq = proj.shape[0]
    tb = min(GDN_TB, seq)
    nsteps = seq // tb
    n_problems = (tb // GDN_BLK) * GDN_HEADS
    prep = lambda t: jnp.minimum(t, nsteps - 1)
    done = lambda t: jnp.maximum(t - 1, 0)
    slab = lambda j: pl.BlockSpec((tb, GW), lambda t: (prep(t), j))
    prev = lambda j: pl.BlockSpec((SUBLANES, GW),
                                  lambda t: (jnp.maximum(prep(t) * (tb // SUBLANES) - 1, 0), j))
    full = lambda shape: _layer_spec(shape, layer)
    factor = lambda dtype: pltpu.VMEM((2, n_problems, GDN_BLK, GDN_HEAD_DIM), dtype)
    return pl.pallas_call(
        _gdn_kernel,
        grid=(nsteps + 1,),
        in_specs=[
            slab(SLAB_GQ), slab(SLAB_GK), slab(SLAB_GV),
            pl.BlockSpec((tb, GW), lambda t: (done(t), SLAB_GZ)),
            prev(SLAB_GQ), prev(SLAB_GK), prev(SLAB_GV),
            pl.BlockSpec((tb, LANES), lambda t: (prep(t), SLAB128_SMALL)),
            full((4, 3 * GW)), full((SUBLANES, LANES)), full((1, GDN_HEAD_DIM)),
        ],
        out_specs=pl.BlockSpec((tb, GW), lambda t: (done(t), 0)),
        out_shape=jax.ShapeDtypeStruct((seq, GW), BF16),
        scratch_shapes=[pltpu.VMEM((GDN_HEADS, GDN_HEAD_DIM, GDN_HEAD_DIM), F32),
                        pltpu.VMEM((3, tb + SUBLANES, GW), F32),
                        factor(F32), factor(BF16), factor(BF16), factor(BF16), factor(BF16),
                        pltpu.VMEM((2, tb, LANES), F32)],
        compiler_params=_cparams("arbitrary"),
        name="gated_deltanet",
    )(proj, proj, proj, proj, proj, proj, proj, proj, conv_w, lane_params, norm_w)


SSD_TB = 256


def _ssd_kernel(z_ref, x_ref, bc_ref, xp_ref, bcp_ref, sm_ref, cw_ref, cb_ref, lp_ref, nw_ref,
                o_ref, state_ref):
    step = pl.program_id(0)
    tb = z_ref.shape[0]
    nstate = SSM_STATE
    chunk = SSM_CHUNK
    nchunks = tb // chunk
    n_pairs = SSM_HEADS // 2
    pairs_per_group = n_pairs // 2

    @pl.when(step == 0)
    def _():
        state_ref[...] = jnp.zeros_like(state_ref)

    keep_prev = jnp.where(step == 0, 0.0, 1.0)
    cw = cw_ref[...]
    cbias = cb_ref[...]
    xs = _silu(_causal_conv(x_ref[...], xp_ref[...] * keep_prev, cw[:, :GW], 4) + cbias[:, :GW])
    bc = _silu(_causal_conv(bc_ref[...], bcp_ref[...] * keep_prev, cw[:, GW:], 4) + cbias[:, GW:])
    z = z_ref[...]

    small = sm_ref[...]
    a_neg = -jnp.exp(lp_ref[0:1, :])
    dt_all = jax.nn.softplus(small + lp_ref[1:2, :])
    dskip_all = lp_ref[2:3, :]
    gc_all, gtot_all = _chunk_cumsum(dt_all * a_neg, chunk)
    gc_t = gc_all.T

    r = lax.broadcasted_iota(jnp.int32, (chunk, chunk), 0)
    s = lax.broadcasted_iota(jnp.int32, (chunk, chunk), 1)
    mask_incl = s <= r
    lane = lax.broadcasted_iota(jnp.int32, (1, LANES), 1)
    first_half = lane < SSM_HEAD_DIM
    by_half = lambda a, b: jnp.where(first_half, a, b)

    problems = [(c, p) for c in range(nchunks) for p in range(n_pairs)]
    rows_of = lambda c: slice(c * chunk, (c + 1) * chunk)
    cols_of = lambda p: slice(p * LANES, (p + 1) * LANES)
    lanes_of = lambda p: (LANE_DT + 2 * p, LANE_DT + 2 * p + 1)
    col = lambda arr, c, l: arr[rows_of(c), l:l + 1]

    b_mat = {(c, g): bc[rows_of(c), g * nstate:(g + 1) * nstate].astype(BF16)
             for c in range(nchunks) for g in range(2)}
    c_mat = {(c, g): bc[rows_of(c), (2 + g) * nstate:(3 + g) * nstate].astype(BF16)
             for c in range(nchunks) for g in range(2)}
    cb = {key: _mm_nt(c_mat[key], b_mat[key]) for key in b_mat}

    x_pair = [xs[rows_of(c), cols_of(p)] for c, p in problems]
    xdt = [x * by_half(col(dt_all, c, lanes_of(p)[0]), col(dt_all, c, lanes_of(p)[1]))
           for x, (c, p) in zip(x_pair, problems)]
    y_diag = []
    for (c, p), xd in zip(problems, xdt):
        halves = (by_half(xd, 0.0), by_half(0.0, xd))
        acc = None
        for l, half in zip(lanes_of(p), halves):
            grow = gc_t[l:l + 1, rows_of(c)]
            lmat = jnp.exp(jnp.where(mask_incl, col(gc_all, c, l) - grow, -jnp.inf))
            t = _mm(cb[c, p // pairs_per_group] * lmat, half)
            acc = t if acc is None else acc + t
        y_diag.append(acc)

    e_in = [by_half(jnp.exp(col(gc_all, c, lanes_of(p)[0])), jnp.exp(col(gc_all, c, lanes_of(p)[1])))
            for c, p in problems]
    e_out = [by_half(jnp.exp(col(gtot_all, c, lanes_of(p)[0]) - col(gc_all, c, lanes_of(p)[0])),
                     jnp.exp(col(gtot_all, c, lanes_of(p)[1]) - col(gc_all, c, lanes_of(p)[1])))
             for c, p in problems]
    states = [state_ref[p] for p in range(n_pairs)]
    y_off = []
    for idx, (c, p) in enumerate(problems):
        la, lb = lanes_of(p)
        g = p // pairs_per_group
        y_off.append(e_in[idx] * _mm(c_mat[c, g], states[p]))
        first = c * chunk
        cd_lane = by_half(jnp.exp(gtot_all[first:first + 1, la:la + 1]),
                          jnp.exp(gtot_all[first:first + 1, lb:lb + 1]))
        states[p] = states[p] * cd_lane + _mm_tn(b_mat[c, g], xdt[idx] * e_out[idx])
    for p in range(n_pairs):
        state_ref[p] = states[p]

    ys = []
    for idx, (c, p) in enumerate(problems):
        la, lb = lanes_of(p)
        y = y_diag[idx] + y_off[idx] + x_pair[idx] * by_half(dskip_all[:, la:la + 1], dskip_all[:, lb:lb + 1])
        ys.append(y * _silu(z[rows_of(c), cols_of(p)]))
    sumsq = [jnp.sum(y * y, axis=-1, keepdims=True) for y in ys]
    for c in range(nchunks):
        for g in range(2):
            members = [c * n_pairs + g * pairs_per_group + k for k in range(pairs_per_group)]
            total = sumsq[members[0]]
            for idx in members[1:]:
                total = total + sumsq[idx]
            inv = lax.rsqrt(total / (pairs_per_group * LANES) + EPS)
            for idx in members:
                cols = cols_of(problems[idx][1])
                o_ref[rows_of(c), cols] = (ys[idx] * inv * nw_ref[:, cols]).astype(o_ref.dtype)


def _mamba2_ssd(proj, layer, conv_w, conv_b, lane_params, norm_w):
    seq = proj.shape[0]
    tb = min(SSD_TB, seq)
    slab = lambda j: pl.BlockSpec((tb, GW), lambda i: (i, j))
    full = lambda shape: _layer_spec(shape, layer)
    return pl.pallas_call(
        _ssd_kernel,
        grid=(seq // tb,),
        in_specs=[
            slab(SLAB_SZ), slab(SLAB_SX), slab(SLAB_SBC),
            _prev_rows_spec(tb, GW, SLAB_SX), _prev_rows_spec(tb, GW, SLAB_SBC),
            pl.BlockSpec((tb, LANES), lambda i: (i, SLAB128_SMALL)),
            full((4, 2 * GW)), full((1, 2 * GW)), full((SUBLANES, LANES)), full((1, GW)),
        ],
        out_specs=pl.BlockSpec((tb, GW), lambda i: (i, 0)),
        out_shape=jax.ShapeDtypeStruct((seq, GW), BF16),
        scratch_shapes=[pltpu.VMEM((SSM_HEADS // 2, SSM_STATE, LANES), F32)],
        compiler_params=_cparams("arbitrary"),
        name="mamba2_ssd",
    )(proj, proj, proj, proj, proj, proj, conv_w, conv_b, lane_params, norm_w)


def _rotate_half(x, lane_in_head_low):
    width = x.shape[1]
    half = ATTN_HEAD_DIM // 2
    ahead = pltpu.roll(x, width - half, axis=1)
    behind = pltpu.roll(x, half, axis=1)
    return jnp.where(lane_in_head_low, ahead, behind)


SWA_TB = 512


def _swa_kernel(q_ref, k_ref, v_ref, cos_ref, sin_ref, sink_ref, o_ref, kprev_ref, vprev_ref):
    step = pl.program_id(0)
    tb = q_ref.shape[0]
    win = WINDOW
    hd = ATTN_HEAD_DIM

    @pl.when(step == 0)
    def _():
        kprev_ref[...] = jnp.zeros_like(kprev_ref)
        vprev_ref[...] = jnp.zeros_like(vprev_ref)

    cos = cos_ref[...]
    sin = sin_ref[...]
    lane = lax.broadcasted_iota(jnp.int32, (1, LANES), 1)
    low = (lane & (hd - 1)) < (hd // 2)
    first_half = lane < hd

    k_cur = k_ref[...]
    k_cur = k_cur * cos + _rotate_half(k_cur, low) * sin
    v_cur = v_ref[...]
    k_all = jnp.concatenate([kprev_ref[...], k_cur], axis=0)
    v_all = jnp.concatenate([vprev_ref[...], v_cur], axis=0)
    k_swap = pltpu.roll(k_all, hd, axis=1)
    v_swap = pltpu.roll(v_all, hd, axis=1)

    def placed(x, x_swap, kv_head, pos):
        src = x if kv_head == pos else x_swap
        keep = first_half if pos == 0 else ~first_half
        return jnp.where(keep, src, 0.0).astype(BF16)

    qi = lax.broadcasted_iota(jnp.int32, (win, 2 * win), 0)
    kj = lax.broadcasted_iota(jnp.int32, (win, 2 * win), 1)
    rel = qi + win - kj
    band = (rel >= 0) & (rel < win)
    band_first = band & ((kj >= win) | (step > 0))

    problems = [(j, h) for j in range(tb // win) for h in range(ATTN_Q_HEADS)]
    kv_of = lambda head: head // (ATTN_Q_HEADS // 2)
    pair_cols = lambda pair: slice(pair * LANES, (pair + 1) * LANES)
    k_at = {(g, pos): placed(k_all, k_swap, g, pos) for g in range(2) for pos in range(2)}
    v_at = {(g, pos): placed(v_all, v_swap, g, pos) for g in range(2) for pos in range(2)}
    q_pairs = []
    for pair in range(ATTN_Q_HEADS // 2):
        q_pair = q_ref[:, pair_cols(pair)]
        q_pairs.append((q_pair * cos + _rotate_half(q_pair, low) * sin).astype(BF16))
    sinks = [sink_ref[h:h + 1, 0:1] for _, h in problems]
    sc = [jnp.where(band_first if j == 0 else band,
                    _mm_nt(q_pairs[h // 2][j * win:(j + 1) * win],
                           k_at[kv_of(h), h % 2][j * win:(j + 2) * win]) * (hd ** -0.5),
                    -jnp.inf)
          for j, h in problems]
    mx = [jnp.maximum(jnp.max(s_, axis=-1, keepdims=True), sk) for s_, sk in zip(sc, sinks)]
    p = [jnp.exp(s_ - m_) for s_, m_ in zip(sc, mx)]
    denom = [jnp.sum(p_, axis=-1, keepdims=True) + jnp.exp(sk - m_) for p_, sk, m_ in zip(p, sinks, mx)]
    o = [_mm(p_, v_at[kv_of(h), h % 2][j * win:(j + 2) * win]) / d_
         for (j, h), p_, d_ in zip(problems, p, denom)]
    for idx in range(0, len(problems), 2):
        j, h = problems[idx]
        o_ref[j * win:(j + 1) * win, pair_cols(h // 2)] = (o[idx] + o[idx + 1]).astype(o_ref.dtype)

    kprev_ref[...] = k_cur[tb - win:]
    vprev_ref[...] = v_cur[tb - win:]


def _sliding_window_attention(proj, layer, cos, sin, sink_rows):
    seq = proj.shape[0]
    tb = min(SWA_TB, seq)
    return pl.pallas_call(
        _swa_kernel,
        grid=(seq // tb,),
        in_specs=[
            pl.BlockSpec((tb, GW), lambda i: (i, SLAB_AQ)),
            pl.BlockSpec((tb, LANES), lambda i: (i, SLAB128_AK)),
            pl.BlockSpec((tb, LANES), lambda i: (i, SLAB128_AV)),
            pl.BlockSpec((tb, LANES), lambda i: (i, 0)),
            pl.BlockSpec((tb, LANES), lambda i: (i, 0)),
            _layer_spec((ATTN_Q_HEADS, LANES), layer),
        ],
        out_specs=pl.BlockSpec((tb, GW), lambda i: (i, 0)),
        out_shape=jax.ShapeDtypeStruct((seq, GW), BF16),
        scratch_shapes=[pltpu.VMEM((WINDOW, LANES), F32), pltpu.VMEM((WINDOW, LANES), F32)],
        compiler_params=_cparams("arbitrary"),
        name="sliding_window_attn",
    )(proj, proj, proj, cos, sin, sink_rows)


SCONV_TB = 1024


def _sconv_kernel(b_ref, c_ref, h_ref, cp_ref, hp_ref, w_ref, o_ref):
    keep_prev = jnp.where(pl.program_id(0) == 0, 0.0, 1.0)
    ch = c_ref[...] * h_ref[...]
    ch_prev = cp_ref[...] * hp_ref[...] * keep_prev
    o_ref[...] = (b_ref[...] * _causal_conv(ch, ch_prev, w_ref[...], 3)).astype(o_ref.dtype)


def _short_conv(proj, layer, conv_w):
    seq = proj.shape[0]
    tb = min(SCONV_TB, seq)
    slab = lambda j: pl.BlockSpec((tb, GW), lambda i: (i, j))
    return pl.pallas_call(
        _sconv_kernel,
        grid=(seq // tb,),
        in_specs=[
            slab(SLAB_CB), slab(SLAB_CC), slab(SLAB_CH),
            _prev_rows_spec(tb, GW, SLAB_CC), _prev_rows_spec(tb, GW, SLAB_CH),
            _layer_spec((3, GW), layer),
        ],
        out_specs=pl.BlockSpec((tb, GW), lambda i: (i, 0)),
        out_shape=jax.ShapeDtypeStruct((seq, GW), BF16),
        compiler_params=_cparams("parallel"),
        name="short_conv",
    )(proj, proj, proj, proj, proj, conv_w)


OUTPROJ_TM = 1024


def _outproj_kernel(ya_ref, yb_ref, yc_ref, yd_ref, w_ref, x_ref, gate_ref, nw_ref, o_ref):
    acc = None
    for g, y_ref in enumerate((ya_ref, yb_ref, yc_ref, yd_ref)):
        part = jnp.dot(y_ref[...], w_ref[g * GW:(g + 1) * GW, :], preferred_element_type=F32)
        acc = part if acc is None else acc + part
    o_ref[...] = acc

    def body(r, carry):
        rows = pl.ds(pl.multiple_of(r * NORM_ROWS, NORM_ROWS), NORM_ROWS)
        y = o_ref[rows, :]
        y = y * lax.rsqrt(jnp.mean(y * y, axis=-1, keepdims=True) + EPS) * nw_ref[...]
        o_ref[rows, :] = x_ref[rows, :] + gate_ref[...] * y
        return carry

    lax.fori_loop(0, o_ref.shape[0] // NORM_ROWS, body, 0)


def _out_projection(ys, layer, w_bf16, x, mod, norm_w):
    seq, d = x.shape
    tm = min(OUTPROJ_TM, seq)
    vmem = 2 * 4 * tm * GW * 2 + 2 * 2 * tm * d * 4 + 4 * GW * d * 2 + tm * d * 4 + 2 * MIB
    return pl.pallas_call(
        _outproj_kernel,
        grid=(seq // tm,),
        in_specs=[pl.BlockSpec((tm, GW), lambda i: (i, 0))] * 4 + [
            pl.BlockSpec((None, 4 * GW, d), lambda i: (layer, 0, 0), pipeline_mode=pl.Buffered(1)),
            pl.BlockSpec((tm, d), lambda i: (i, 0)),
            _layer_spec((1, d), layer, 0, MOD_GATE_A),
            _layer_spec((1, d), layer),
        ],
        out_specs=pl.BlockSpec((tm, d), lambda i: (i, 0)),
        out_shape=jax.ShapeDtypeStruct((seq, d), F32),
        compiler_params=_cparams("parallel", vmem_limit=vmem),
        name="out_proj",
    )(*ys, w_bf16, x, mod, norm_w)


MLP_TM = 1024
MLP_TH = 512


def _mlp_kernel(x_ref, nw_ref, sc_ref, sh_ref, wu_ref, wd_ref, gate_ref, pw_ref, o_ref, h_ref):
    j = pl.program_id(1)

    @pl.when(j == 0)
    def _():
        _prenorm_to(x_ref, nw_ref, sc_ref, sh_ref, h_ref)
        o_ref[...] = jnp.zeros_like(o_ref)

    hid = _mm(h_ref[...], wu_ref[...])
    hid = jnp.square(jnp.maximum(hid, 0.0))
    o_ref[...] += _mm(hid, wd_ref[...])

    @pl.when(j == pl.num_programs(1) - 1)
    def _():
        def body(r, carry):
            rows = pl.ds(pl.multiple_of(r * NORM_ROWS, NORM_ROWS), NORM_ROWS)
            y = o_ref[rows, :]
            y = y * lax.rsqrt(jnp.mean(y * y, axis=-1, keepdims=True) + EPS) * pw_ref[...]
            o_ref[rows, :] = x_ref[rows, :] + gate_ref[...] * y
            return carry

        lax.fori_loop(0, x_ref.shape[0] // NORM_ROWS, body, 0)


def _mlp(x, layer, norm_w, mod, w_up, w_down, post_w):
    seq, d = x.shape
    hidden = w_up.shape[2]
    tm = min(MLP_TM, seq)
    vmem = (2 * 2 * tm * d * 4 + tm * d * 2 + 2 * 2 * d * MLP_TH * w_up.dtype.itemsize
            + 2 * d * MLP_TH * 2 + tm * MLP_TH * 6 + 2 * MIB)
    return pl.pallas_call(
        _mlp_kernel,
        grid=(seq // tm, hidden // MLP_TH),
        in_specs=[
            pl.BlockSpec((tm, d), lambda i, j: (i, 0)),
            _layer_spec((1, d), layer),
            _layer_spec((1, d), layer, 0, MOD_SCALE_M),
            _layer_spec((1, d), layer, 0, MOD_SHIFT_M),
            _layer_spec((d, MLP_TH), layer, 0, lambda i, j: j),
            _layer_spec((MLP_TH, d), layer, lambda i, j: j, 0),
            _layer_spec((1, d), layer, 0, MOD_GATE_M),
            _layer_spec((1, d), layer),
        ],
        out_specs=pl.BlockSpec((tm, d), lambda i, j: (i, 0)),
        out_shape=jax.ShapeDtypeStruct((seq, d), F32),
        scratch_shapes=[pltpu.VMEM((tm, d), BF16)],
        compiler_params=_cparams("parallel", "arbitrary", vmem_limit=vmem),
        name="mlp",
    )(x, norm_w, mod, mod, w_up, w_down, mod, post_w)


def _gathered_group(w_t):
    parts = [w_t[:, IN_OFFSET[name][0]:IN_OFFSET[name][1]] for name in GATHERED_GROUP]
    used = sum(p.shape[1] for p in parts)
    parts.append(jnp.zeros((w_t.shape[0], IN_GROUP - used, w_t.shape[2]), w_t.dtype))
    return jnp.concatenate(parts, axis=1)


def kernel(x, c, positions, ada_w, ada_b, norm_pre_mix, norm_post_mix, norm_pre_mlp, norm_post_mlp,
           w_in, w_out, gdn_conv_w, gdn_a_log, gdn_dt_bias, gdn_norm_w, ssm_conv_w, ssm_conv_b,
           ssm_a_log, ssm_dt_bias, ssm_d, ssm_norm_w, attn_sinks, sc_conv_w, w_up, w_down):
    batch, seq, d = x.shape
    assert batch == 1 and d == D_MODEL
    depth = ada_w.shape[0]
    xs = x.reshape(seq, d)

    mod = _modulation(c.reshape(d, 1), ada_w, ada_b)
    cos, sin = _rope_tables(positions.reshape(seq, 1))

    rows = lambda v: v.reshape(depth, 1, -1)
    w_in_t = jnp.swapaxes(w_in, 1, 2).astype(BF16)
    w_in_tail = _gathered_group(w_in_t)
    w_out_bf16 = w_out.astype(BF16)
    gdn_lanes = _lane_params(LANE_ALPHA, gdn_a_log, gdn_dt_bias)
    ssm_lanes = _lane_params(LANE_DT, ssm_a_log, ssm_dt_bias, ssm_d)
    sink_rows = jnp.broadcast_to(attn_sinks.astype(F32)[:, :, None], (depth, ATTN_Q_HEADS, LANES))

    for i in range(depth):
        proj = _in_projection(xs, i, rows(norm_pre_mix), mod, w_in_t, w_in_tail)
        y_a = _gated_deltanet(proj, i, gdn_conv_w, gdn_lanes, rows(gdn_norm_w))
        y_b = _mamba2_ssd(proj, i, ssm_conv_w, rows(ssm_conv_b), ssm_lanes, rows(ssm_norm_w))
        y_c = _sliding_window_attention(proj, i, cos, sin, sink_rows)
        y_d = _short_conv(proj, i, sc_conv_w)
        xs = _out_projection((y_a, y_b, y_c, y_d), i, w_out_bf16, xs, mod, rows(norm_post_mix))
        xs = _mlp(xs, i, rows(norm_pre_mlp), mod, w_up, w_down, rows(norm_post_mlp))
    return xs.reshape(batch, seq, d)
```

```python
import functools

import jax
import jax.numpy as jnp
from jax import lax
from jax.experimental import pallas as pl
from jax.experimental.pallas import tpu as pltpu

F32 = jnp.float32
BF16 = jnp.bfloat16

D_MODEL = 2048
GW = 512
GDN_HEADS, GDN_HEAD_DIM, GDN_CHUNK = 4, 128, 64
SSM_HEADS, SSM_HEAD_DIM, SSM_STATE, SSM_CHUNK = 8, 64, 128, 128
ATTN_HEAD_DIM, ATTN_Q_HEADS, WINDOW = 64, 8, 128
ROPE_THETA = 10000.0
MLP_HIDDEN = 4 * D_MODEL
EPS = 1e-6

SUBLANES = 8
LANES = 128

_IN_SEGMENTS = (("gq", GW), ("gk", GW), ("gv", GW), ("gz", GW), ("gb", GDN_HEADS), ("ga", GDN_HEADS),
                ("sz", GW), ("sx", GW), ("sbc", 4 * SSM_STATE), ("sdt", SSM_HEADS),
                ("aq", GW), ("ak", 2 * ATTN_HEAD_DIM), ("av", 2 * ATTN_HEAD_DIM),
                ("cb", GW), ("cc", GW), ("ch", GW))
IN_OFFSET = {}
_off = 0
for _name, _size in _IN_SEGMENTS:
    IN_OFFSET[_name] = (_off, _off + _size)
    _off += _size
IN_WIDTH = _off

IN_GROUP = 3 * GW
CONTIGUOUS_GROUPS = (("gq", "gk", "gv"), ("sz", "sx", "sbc"), ("cb", "cc", "ch"))
GATHERED_GROUP = ("gz", "aq", "ak", "av", "gb", "ga", "sdt")
IN_COLS = (len(CONTIGUOUS_GROUPS) + 1) * IN_GROUP
for _grp in CONTIGUOUS_GROUPS:
    assert IN_OFFSET[_grp[0]][0] % SUBLANES == 0
    assert all(IN_OFFSET[a][1] == IN_OFFSET[b][0] for a, b in zip(_grp, _grp[1:]))
(SLAB_GQ, SLAB_GK, SLAB_GV, SLAB_SZ, SLAB_SX, SLAB_SBC, SLAB_CB, SLAB_CC, SLAB_CH, SLAB_GZ, SLAB_AQ) = range(11)
SLAB128_AK, SLAB128_AV, SLAB128_SMALL = 44, 45, 46
LANE_BETA, LANE_ALPHA, LANE_DT = 0, 4, 8

MIB = 1024 * 1024
V7X_VMEM_BYTES = 64 * MIB
VMEM_LIMIT = 56 * MIB


def _cparams(*sem, vmem_limit=VMEM_LIMIT):
    assert vmem_limit < V7X_VMEM_BYTES
    return pltpu.CompilerParams(dimension_semantics=sem, vmem_limit_bytes=vmem_limit)


def _sigmoid(x):
    return jax.nn.sigmoid(x)


def _silu(x):
    return x * jax.nn.sigmoid(x)


def _mm(a, b):
    return jnp.dot(a.astype(BF16), b.astype(BF16), preferred_element_type=F32)


def _mm_nt(a, b):
    return lax.dot_general(a.astype(BF16), b.astype(BF16), (((1,), (1,)), ((), ())),
                           preferred_element_type=F32)


def _mm_tn(a, b):
    return lax.dot_general(a.astype(BF16), b.astype(BF16), (((0,), (0,)), ((), ())),
                           preferred_element_type=F32)


def _blk(idx, size):
    assert size & (size - 1) == 0
    return lax.shift_right_logical(idx, size.bit_length() - 1)


def _split3(x):
    x1 = x.astype(BF16)
    r1 = x - x1.astype(F32)
    x2 = r1.astype(BF16)
    r2 = r1 - x2.astype(F32)
    return x1, x2, r2.astype(BF16)


def _chunk_cumsum(g, chunk):
    rows = g.shape[0]
    r = lax.broadcasted_iota(jnp.int32, (2 * rows, rows), 0)
    s = lax.broadcasted_iota(jnp.int32, (2 * rows, rows), 1)
    rr = jnp.where(r >= rows, r - rows, r)
    same = _blk(rr, chunk) == _blk(s, chunk)
    sel = same & ((r >= rows) | (s <= rr))
    mat = jnp.where(sel, 1.0, 0.0).astype(BF16)
    acc = None
    for part in _split3(g):
        t = jnp.dot(mat, part, preferred_element_type=F32)
        acc = t if acc is None else acc + t
    return acc[:rows], acc[rows:]


def _causal_conv(x, prev, w, taps):
    rows = x.shape[0]
    xp = jnp.concatenate([prev, x], axis=0)
    acc = x * w[taps - 1:taps]
    for d in range(1, taps):
        acc = acc + xp[SUBLANES - d:SUBLANES - d + rows] * w[taps - 1 - d:taps - d]
    return acc


def _causal_conv_staged(x_ref, prev, w, taps, stage_ref):
    rows = x_ref.shape[0]
    x = x_ref[...]
    stage_ref[0:SUBLANES, :] = prev
    stage_ref[SUBLANES:, :] = x
    acc = x * w[taps - 1:taps]
    for d in range(1, taps):
        acc = acc + stage_ref[pl.ds(SUBLANES - d, rows), :] * w[taps - 1 - d:taps - d]
    return acc


MOD_TN = 2048
MOD_ROWS = 64


def _mod_kernel(c_ref, w_ref, b_ref, o_ref):
    d = c_ref.shape[0]

    def body(r, acc):
        rows = pl.ds(pl.multiple_of(r * MOD_ROWS, MOD_ROWS), MOD_ROWS)
        cc = _silu(c_ref[rows, :])
        prod = w_ref[0, rows, :] * cc
        return acc + jnp.sum(prod.reshape(MOD_ROWS // SUBLANES, SUBLANES, MOD_TN), axis=0)

    acc = lax.fori_loop(0, d // MOD_ROWS, body, jnp.zeros((SUBLANES, MOD_TN), F32))
    o_ref[0] = jnp.sum(acc, axis=0, keepdims=True) + b_ref[0]


def _modulation(c_col, ada_w, ada_b, layer):
    depth, d, n = ada_w.shape
    return pl.pallas_call(
        _mod_kernel,
        grid=(n // MOD_TN,),
        in_specs=[
            pl.BlockSpec((d, 1), lambda j: (0, 0)),
            pl.BlockSpec((1, d, MOD_TN), lambda j: (layer, 0, j)),
            pl.BlockSpec((1, 1, MOD_TN), lambda j: (layer, 0, j)),
        ],
        out_specs=pl.BlockSpec((1, 1, MOD_TN), lambda j: (0, 0, j)),
        out_shape=jax.ShapeDtypeStruct((1, 1, n), F32),
        compiler_params=_cparams("parallel"),
        name="adaln_mod",
    )(c_col, ada_w, ada_b)


def _mod_spec(k):
    return pl.BlockSpec((None, 1, D_MODEL), lambda *grid_idx: (0, 0, k))


ROPE_TB = 1024


def _rope_kernel(pos_ref, invf_ref, sign_ref, cos_ref, sin_ref):
    ang = pos_ref[...].astype(F32) * invf_ref[...]
    cos_ref[...] = jnp.cos(ang)
    sin_ref[...] = jnp.sin(ang) * sign_ref[...]


def _rope_tables(pos_col):
    seq = pos_col.shape[0]
    half = ATTN_HEAD_DIM // 2
    inv_freq = ROPE_THETA ** (-jnp.arange(0, ATTN_HEAD_DIM, 2, dtype=F32) / ATTN_HEAD_DIM)
    invf = jnp.tile(inv_freq, LANES // half).reshape(1, LANES)
    lane = jnp.arange(LANES)
    sign = jnp.where((lane % ATTN_HEAD_DIM) < half, -1.0, 1.0).astype(F32).reshape(1, LANES)
    tb = min(ROPE_TB, seq)
    return pl.pallas_call(
        _rope_kernel,
        grid=(seq // tb,),
        in_specs=[
            pl.BlockSpec((tb, 1), lambda i: (i, 0)),
            pl.BlockSpec((1, LANES), lambda i: (0, 0)),
            pl.BlockSpec((1, LANES), lambda i: (0, 0)),
        ],
        out_specs=[pl.BlockSpec((tb, LANES), lambda i: (i, 0))] * 2,
        out_shape=[jax.ShapeDtypeStruct((seq, LANES), F32)] * 2,
        compiler_params=_cparams("parallel"),
        name="rope_tables",
    )(pos_col, invf, sign)


NORM_ROWS = 128


def _prenorm_to(x_ref, nw_ref, sc_ref, sh_ref, h_ref):
    tm = x_ref.shape[0]

    def body(r, carry):
        rows = pl.ds(pl.multiple_of(r * NORM_ROWS, NORM_ROWS), NORM_ROWS)
        x = x_ref[rows, :]
        y = x * lax.rsqrt(jnp.mean(x * x, axis=-1, keepdims=True) + EPS) * nw_ref[...]
        h_ref[rows, :] = (y * (1.0 + sc_ref[...]) + sh_ref[...]).astype(BF16)
        return carry

    lax.fori_loop(0, tm // NORM_ROWS, body, 0)


INPROJ_TM = 1024

MOD_SHIFT_A, MOD_SCALE_A, MOD_GATE_A, MOD_SHIFT_M, MOD_SCALE_M, MOD_GATE_M = range(6)


def _layer_spec(shape, layer, *rest):
    rest = rest or (0,) * len(shape)

    def index_map(*grid_idx):
        return (layer,) + tuple(r(*grid_idx) if callable(r) else r for r in rest)

    return pl.BlockSpec((None,) + tuple(shape), index_map)


def _inproj_kernel(x_ref, nw_ref, sc_ref, sh_ref, wt_ref, tail_ref, o_ref, h_ref):
    j = pl.program_id(1)

    @pl.when(j == 0)
    def _():
        _prenorm_to(x_ref, nw_ref, sc_ref, sh_ref, h_ref)

    def project(w_ref):
        o_ref[...] = _mm_nt(h_ref[...], w_ref[...])

    @pl.when(j < len(CONTIGUOUS_GROUPS))
    def _():
        project(wt_ref)

    @pl.when(j == len(CONTIGUOUS_GROUPS))
    def _():
        project(tail_ref)


def _group_row(j):
    starts = [IN_OFFSET[grp[0]][0] // SUBLANES for grp in CONTIGUOUS_GROUPS]
    row8 = starts[-1]
    for k in range(len(starts) - 2, -1, -1):
        row8 = jnp.where(j <= k, starts[k], row8)
    return row8 * SUBLANES


def _in_projection(x, layer, norm_w, mod, w_t, w_gathered):
    seq, d = x.shape
    tm = min(INPROJ_TM, seq)
    return pl.pallas_call(
        _inproj_kernel,
        grid=(seq // tm, IN_COLS // IN_GROUP),
        in_specs=[
            pl.BlockSpec((tm, d), lambda i, j: (i, 0)),
            _layer_spec((1, d), layer),
            _mod_spec(MOD_SCALE_A),
            _mod_spec(MOD_SHIFT_A),
            pl.BlockSpec((None, pl.Element(IN_GROUP), pl.Element(d)), lambda i, j: (layer, _group_row(j), 0)),
            pl.BlockSpec((None, IN_GROUP, d), lambda i, j: (layer, 0, 0), pipeline_mode=pl.Buffered(1)),
        ],
        out_specs=pl.BlockSpec((tm, IN_GROUP), lambda i, j: (i, j)),
        out_shape=jax.ShapeDtypeStruct((seq, IN_COLS), F32),
        scratch_shapes=[pltpu.VMEM((tm, d), BF16)],
        compiler_params=_cparams("parallel", "arbitrary"),
        name="in_proj",
    )(x, norm_w, mod, mod, w_t, w_gathered)


GDN_TB = 512


def _l2norm(x):
    return x * lax.rsqrt(jnp.sum(x * x, axis=-1, keepdims=True) + EPS)


GDN_BLK = 128
GDN_PREP_STAGES_PER_CHUNK = 4


def _unit_lower_inverse_minus_identity(ms, r, s):
    same8 = _blk(r, 8) == _blk(s, 8)
    same16 = _blk(r, 16) == _blk(s, 16)
    same32 = _blk(r, 32) == _blk(s, 32)
    n1 = [jnp.where(same8, -m, 0.0) for m in ms]
    n2 = [_mm(a, a) for a in n1]
    yield
    n3 = [_mm(a, b) for a, b in zip(n1, n2)]
    n4 = [_mm(b, b) for b in n2]
    yield
    ys = [a + b + c for a, b, c in zip(n1, n2, n3)]
    ts = [_mm(y, d) for y, d in zip(ys, n4)]
    yield
    ys = [y + d + t for y, d, t in zip(ys, n4, ts)]
    for sel in (same16 & ~same8, same32 & ~same16, ~same32):
        blks = [jnp.where(sel, m, 0.0) for m in ms]
        cs = [b + _mm(y, b) for y, b in zip(ys, blks)]
        yield
        ts = [_mm(c, y) for c, y in zip(cs, ys)]
        yield
        ys = [y - c - t for y, c, t in zip(ys, cs, ts)]
    return ys


def _run_interleaved(main, side, main_stages_per_side_stage):
    side_live = True
    done = 0
    while True:
        try:
            next(main)
        except StopIteration as stop:
            result = stop.value
            break
        done += 1
        if side_live and done % main_stages_per_side_stage == 0:
            side_live = next(side, StopIteration) is not StopIteration
    while side_live:
        side_live = next(side, StopIteration) is not StopIteration
    return result


def _gdn_prepare(q_ref, k_ref, v_ref, qp_ref, kp_ref, vp_ref, sm_ref, cw_ref, lp_ref, conv_ref, first_block):
    tb = q_ref.shape[0]
    hd = GDN_HEAD_DIM
    nblk = tb // GDN_BLK

    keep_prev = jnp.where(first_block, 0.0, 1.0)
    cw = cw_ref[...]

    def conv_silu(x_ref, p_ref, idx):
        w = cw[:, idx * GW:(idx + 1) * GW]
        return _silu(_causal_conv_staged(x_ref, p_ref[...] * keep_prev, w, 4, conv_ref.at[idx]))

    q = conv_silu(q_ref, qp_ref, 0)
    yield
    k = conv_silu(k_ref, kp_ref, 1)
    yield
    v = conv_silu(v_ref, vp_ref, 2)
    yield

    small = sm_ref[...]
    a_log = lp_ref[0:1, :]
    dt_bias = lp_ref[1:2, :]
    beta_all = _sigmoid(small)
    g_all = -jnp.exp(a_log) * jax.nn.softplus(small + dt_bias)
    gc_all, gtot_all = _chunk_cumsum(g_all, GDN_CHUNK)
    gc_t = gc_all.T
    yield

    r = lax.broadcasted_iota(jnp.int32, (GDN_BLK, GDN_BLK), 0)
    s = lax.broadcasted_iota(jnp.int32, (GDN_BLK, GDN_BLK), 1)
    same_chunk = _blk(r, GDN_CHUNK) == _blk(s, GDN_CHUNK)
    mask_incl = same_chunk & (s <= r)
    mask_strict = same_chunk & (s < r)

    problems = [(slice(b * GDN_BLK, (b + 1) * GDN_BLK), h) for b in range(nblk) for h in range(GDN_HEADS)]
    head_cols = lambda h: slice(h * hd, (h + 1) * hd)
    gate = lambda arr, rows, lane: arr[rows, lane:lane + 1]

    qn = [_l2norm(q[rows, head_cols(h)]) * (hd ** -0.5) for rows, h in problems]
    kn = [_l2norm(k[rows, head_cols(h)]) for rows, h in problems]
    beta = [gate(beta_all, rows, LANE_BETA + h) for rows, h in problems]
    gcol = [gate(gc_all, rows, LANE_ALPHA + h) for rows, h in problems]
    gtot = [gate(gtot_all, rows, LANE_ALPHA + h) for rows, h in problems]
    grow = [gc_t[LANE_ALPHA + h:LANE_ALPHA + h + 1, rows] for rows, h in problems]
    yield

    decay = [jnp.exp(jnp.where(mask_incl, gc - gr, -jnp.inf)) for gc, gr in zip(gcol, grow)]
    k_beta = [kk * b for kk, b in zip(kn, beta)]
    k16 = [kk.astype(BF16) for kk in kn]
    yield
    ms = [jnp.where(mask_strict, _mm_nt(kb, kk) * d, 0.0) for kb, kk, d in zip(k_beta, k16, decay)]
    yield
    attn = [(_mm_nt(qq, kk) * d).astype(BF16) for qq, kk, d in zip(qn, k16, decay)]
    ys = yield from _unit_lower_inverse_minus_identity(ms, r, s)
    yield

    e_gc = [jnp.exp(gc) for gc in gcol]
    rhs = [jnp.concatenate([v[rows, head_cols(h)] * b, kb * e], axis=1)
           for (rows, h), b, kb, e in zip(problems, beta, k_beta, e_gc)]
    uw = [x + _mm(y, x) for y, x in zip(ys, rhs)]
    yield
    u = [x[:, :hd] for x in uw]
    w = [x[:, hd:].astype(BF16) for x in uw]
    q_dec = [(qq * e).astype(BF16) for qq, e in zip(qn, e_gc)]
    k_dec = [(kk * jnp.exp(gt - gc)).astype(BF16) for kk, gt, gc in zip(kn, gtot, gcol)]
    return u, w, q_dec, k_dec, attn, gtot_all


def _gdn_recurrence(u, w, q_dec, k_dec, attn, gtot_all, state_ref, z, nw_ref, o_ref):
    hd = GDN_HEAD_DIM
    nblk = o_ref.shape[0] // GDN_BLK
    chunks_per_blk = GDN_BLK // GDN_CHUNK
    head_cols = lambda h: slice(h * hd, (h + 1) * hd)

    states = [state_ref[h] for h in range(GDN_HEADS)]
    v_new = [[] for _ in range(nblk * GDN_HEADS)]
    o_inter = [[] for _ in range(nblk * GDN_HEADS)]
    for b in range(nblk):
        for c in range(chunks_per_blk):
            rows = slice(c * GDN_CHUNK, (c + 1) * GDN_CHUNK)
            first = b * GDN_BLK + c * GDN_CHUNK
            for h in range(GDN_HEADS):
                p = b * GDN_HEADS + h
                st16 = states[h].astype(BF16)
                vn = u(p, rows) - jnp.dot(w(p, rows), st16, preferred_element_type=F32)
                o_inter[p].append(jnp.dot(q_dec(p, rows), st16, preferred_element_type=F32))
                chunk_dec = jnp.exp(gtot_all[first:first + 1, LANE_ALPHA + h:LANE_ALPHA + h + 1])
                states[h] = states[h] * chunk_dec + _mm_tn(k_dec(p, rows), vn)
                v_new[p].append(vn)
            yield
        rows = slice(b * GDN_BLK, (b + 1) * GDN_BLK)
        for h in range(GDN_HEADS):
            p = b * GDN_HEADS + h
            o = jnp.concatenate(o_inter[p], axis=0) + _mm(attn(p, slice(None)), jnp.concatenate(v_new[p], axis=0))
            o = o * lax.rsqrt(jnp.mean(o * o, axis=-1, keepdims=True) + EPS) * nw_ref[...]
            o_ref[rows, head_cols(h)] = (o * _silu(z[rows, head_cols(h)])).astype(o_ref.dtype)
    for h in range(GDN_HEADS):
        state_ref[h] = states[h]


def _gdn_kernel(q_ref, k_ref, v_ref, z_ref, qp_ref, kp_ref, vp_ref, sm_ref, cw_ref, lp_ref, nw_ref,
                o_ref, state_ref, conv_ref, u_ref, w_ref, qd_ref, kd_ref, at_ref, gt_ref):
    step = pl.program_id(0)
    write_slot = step % 2
    read_slot = 1 - write_slot
    n_problems = u_ref.shape[1]

    @pl.when(step == 0)
    def _():
        state_ref[...] = jnp.zeros_like(state_ref)
        for ref in (u_ref, w_ref, qd_ref, kd_ref, at_ref, gt_ref):
            ref[1] = jnp.zeros(ref.shape[1:], ref.dtype)

    recurrence = _gdn_recurrence(*[(lambda p, rows, ref=ref: ref[read_slot, p, rows])
                                   for ref in (u_ref, w_ref, qd_ref, kd_ref, at_ref)],
                                 gt_ref[read_slot], state_ref, z_ref[...], nw_ref, o_ref)
    prepare = _gdn_prepare(q_ref, k_ref, v_ref, qp_ref, kp_ref, vp_ref, sm_ref, cw_ref, lp_ref, conv_ref, step == 0)
    u, w, q_dec, k_dec, attn, gtot_all = _run_interleaved(prepare, recurrence, GDN_PREP_STAGES_PER_CHUNK)
    for p in range(n_problems):
        u_ref[write_slot, p] = u[p]
        w_ref[write_slot, p] = w[p]
        qd_ref[write_slot, p] = q_dec[p]
        kd_ref[write_slot, p] = k_dec[p]
        at_ref[write_slot, p] = attn[p]
    gt_ref[write_slot] = gtot_all


def _prev_rows_spec(tb, width, slab):
    blocks = tb // SUBLANES
    return pl.BlockSpec((SUBLANES, width), lambda i: (jnp.maximum(i * blocks - 1, 0), slab))


def _lane_params(lane0, *rows):
    stacked = jnp.stack([r.astype(F32) for r in rows], axis=1)
    nrows, heads = stacked.shape[1:]
    return jnp.pad(stacked, ((0, 0), (0, SUBLANES - nrows), (lane0, LANES - lane0 - heads)))


def _gated_deltanet(proj, layer, conv_w, lane_params, norm_w):
    seq = proj.shape[0]
    tb = min(GDN_TB, seq)
    nsteps = seq // tb
    n_problems = (tb // GDN_BLK) * GDN_HEADS
    prep = lambda t: jnp.minimum(t, nsteps - 1)
    done = lambda t: jnp.maximum(t - 1, 0)
    slab = lambda j: pl.BlockSpec((tb, GW), lambda t: (prep(t), j))
    prev = lambda j: pl.BlockSpec((SUBLANES, GW),
                                  lambda t: (jnp.maximum(prep(t) * (tb // SUBLANES) - 1, 0), j))
    full = lambda shape: _layer_spec(shape, layer)
    factor = lambda dtype: pltpu.VMEM((2, n_problems, GDN_BLK, GDN_HEAD_DIM), dtype)
    return pl.pallas_call(
        _gdn_kernel,
        grid=(nsteps + 1,),
        in_specs=[
            slab(SLAB_GQ), slab(SLAB_GK), slab(SLAB_GV),
            pl.BlockSpec((tb, GW), lambda t: (done(t), SLAB_GZ)),
            prev(SLAB_GQ), prev(SLAB_GK), prev(SLAB_GV),
            pl.BlockSpec((tb, LANES), lambda t: (prep(t), SLAB128_SMALL)),
            full((4, 3 * GW)), full((SUBLANES, LANES)), full((1, GDN_HEAD_DIM)),
        ],
        out_specs=pl.BlockSpec((tb, GW), lambda t: (done(t), 0)),
        out_shape=jax.ShapeDtypeStruct((seq, GW), BF16),
        scratch_shapes=[pltpu.VMEM((GDN_HEADS, GDN_HEAD_DIM, GDN_HEAD_DIM), F32),
                        pltpu.VMEM((3, tb + SUBLANES, GW), F32),
                        factor(F32), factor(BF16), factor(BF16), factor(BF16), factor(BF16),
                        pltpu.VMEM((2, tb, LANES), F32)],
        compiler_params=_cparams("arbitrary"),
        name="gated_deltanet",
    )(proj, proj, proj, proj, proj, proj, proj, proj, conv_w, lane_params, norm_w)


SSD_TB = 256


def _ssd_kernel(z_ref, x_ref, bc_ref, xp_ref, bcp_ref, sm_ref, cw_ref, cb_ref, lp_ref, nw_ref,
                o_ref, state_ref):
    step = pl.program_id(0)
    tb = z_ref.shape[0]
    nstate = SSM_STATE
    chunk = SSM_CHUNK
    nchunks = tb // chunk
    n_pairs = SSM_HEADS // 2
    pairs_per_group = n_pairs // 2

    @pl.when(step == 0)
    def _():
        state_ref[...] = jnp.zeros_like(state_ref)

    keep_prev = jnp.where(step == 0, 0.0, 1.0)
    cw = cw_ref[...]
    cbias = cb_ref[...]
    xs = _silu(_causal_conv(x_ref[...], xp_ref[...] * keep_prev, cw[:, :GW], 4) + cbias[:, :GW])
    bc = _silu(_causal_conv(bc_ref[...], bcp_ref[...] * keep_prev, cw[:, GW:], 4) + cbias[:, GW:])
    z = z_ref[...]

    small = sm_ref[...]
    a_neg = -jnp.exp(lp_ref[0:1, :])
    dt_all = jax.nn.softplus(small + lp_ref[1:2, :])
    dskip_all = lp_ref[2:3, :]
    gc_all, gtot_all = _chunk_cumsum(dt_all * a_neg, chunk)
    gc_t = gc_all.T

    r = lax.broadcasted_iota(jnp.int32, (chunk, chunk), 0)
    s = lax.broadcasted_iota(jnp.int32, (chunk, chunk), 1)
    mask_incl = s <= r
    lane = lax.broadcasted_iota(jnp.int32, (1, LANES), 1)
    first_half = lane < SSM_HEAD_DIM
    by_half = lambda a, b: jnp.where(first_half, a, b)

    problems = [(c, p) for c in range(nchunks) for p in range(n_pairs)]
    rows_of = lambda c: slice(c * chunk, (c + 1) * chunk)
    cols_of = lambda p: slice(p * LANES, (p + 1) * LANES)
    lanes_of = lambda p: (LANE_DT + 2 * p, LANE_DT + 2 * p + 1)
    col = lambda arr, c, l: arr[rows_of(c), l:l + 1]

    b_mat = {(c, g): bc[rows_of(c), g * nstate:(g + 1) * nstate].astype(BF16)
             for c in range(nchunks) for g in range(2)}
    c_mat = {(c, g): bc[rows_of(c), (2 + g) * nstate:(3 + g) * nstate].astype(BF16)
             for c in range(nchunks) for g in range(2)}
    cb = {key: _mm_nt(c_mat[key], b_mat[key]) for key in b_mat}

    x_pair = [xs[rows_of(c), cols_of(p)] for c, p in problems]
    xdt = [x * by_half(col(dt_all, c, lanes_of(p)[0]), col(dt_all, c, lanes_of(p)[1]))
           for x, (c, p) in zip(x_pair, problems)]
    y_diag = []
    for (c, p), xd in zip(problems, xdt):
        halves = (by_half(xd, 0.0), by_half(0.0, xd))
        acc = None
        for l, half in zip(lanes_of(p), halves):
            grow = gc_t[l:l + 1, rows_of(c)]
            lmat = jnp.exp(jnp.where(mask_incl, col(gc_all, c, l) - grow, -jnp.inf))
            t = _mm(cb[c, p // pairs_per_group] * lmat, half)
            acc = t if acc is None else acc + t
        y_diag.append(acc)

    e_in = [by_half(jnp.exp(col(gc_all, c, lanes_of(p)[0])), jnp.exp(col(gc_all, c, lanes_of(p)[1])))
            for c, p in problems]
    e_out = [by_half(jnp.exp(col(gtot_all, c, lanes_of(p)[0]) - col(gc_all, c, lanes_of(p)[0])),
                     jnp.exp(col(gtot_all, c, lanes_of(p)[1]) - col(gc_all, c, lanes_of(p)[1])))
             for c, p in problems]
    states = [state_ref[p] for p in range(n_pairs)]
    y_off = []
    for idx, (c, p) in enumerate(problems):
        la, lb = lanes_of(p)
        g = p // pairs_per_group
        y_off.append(e_in[idx] * _mm(c_mat[c, g], states[p]))
        first = c * chunk
        cd_lane = by_half(jnp.exp(gtot_all[first:first + 1, la:la + 1]),
                          jnp.exp(gtot_all[first:first + 1, lb:lb + 1]))
        states[p] = states[p] * cd_lane + _mm_tn(b_mat[c, g], xdt[idx] * e_out[idx])
    for p in range(n_pairs):
        state_ref[p] = states[p]

    ys = []
    for idx, (c, p) in enumerate(problems):
        la, lb = lanes_of(p)
        y = y_diag[idx] + y_off[idx] + x_pair[idx] * by_half(dskip_all[:, la:la + 1], dskip_all[:, lb:lb + 1])
        ys.append(y * _silu(z[rows_of(c), cols_of(p)]))
    sumsq = [jnp.sum(y * y, axis=-1, keepdims=True) for y in ys]
    for c in range(nchunks):
        for g in range(2):
            members = [c * n_pairs + g * pairs_per_group + k for k in range(pairs_per_group)]
            total = sumsq[members[0]]
            for idx in members[1:]:
                total = total + sumsq[idx]
            inv = lax.rsqrt(total / (pairs_per_group * LANES) + EPS)
            for idx in members:
                cols = cols_of(problems[idx][1])
                o_ref[rows_of(c), cols] = (ys[idx] * inv * nw_ref[:, cols]).astype(o_ref.dtype)


def _mamba2_ssd(proj, layer, conv_w, conv_b, lane_params, norm_w):
    seq = proj.shape[0]
    tb = min(SSD_TB, seq)
    slab = lambda j: pl.BlockSpec((tb, GW), lambda i: (i, j))
    full = lambda shape: _layer_spec(shape, layer)
    return pl.pallas_call(
        _ssd_kernel,
        grid=(seq // tb,),
        in_specs=[
            slab(SLAB_SZ), slab(SLAB_SX), slab(SLAB_SBC),
            _prev_rows_spec(tb, GW, SLAB_SX), _prev_rows_spec(tb, GW, SLAB_SBC),
            pl.BlockSpec((tb, LANES), lambda i: (i, SLAB128_SMALL)),
            full((4, 2 * GW)), full((1, 2 * GW)), full((SUBLANES, LANES)), full((1, GW)),
        ],
        out_specs=pl.BlockSpec((tb, GW), lambda i: (i, 0)),
        out_shape=jax.ShapeDtypeStruct((seq, GW), BF16),
        scratch_shapes=[pltpu.VMEM((SSM_HEADS // 2, SSM_STATE, LANES), F32)],
        compiler_params=_cparams("arbitrary"),
        name="mamba2_ssd",
    )(proj, proj, proj, proj, proj, proj, conv_w, conv_b, lane_params, norm_w)


def _rotate_half(x, lane_in_head_low):
    width = x.shape[1]
    half = ATTN_HEAD_DIM // 2
    ahead = pltpu.roll(x, width - half, axis=1)
    behind = pltpu.roll(x, half, axis=1)
    return jnp.where(lane_in_head_low, ahead, behind)


SWA_TB = 512


def _swa_kernel(q_ref, k_ref, v_ref, cos_ref, sin_ref, sink_ref, cb_ref, cc_ref, ch_ref, ccp_ref, chp_ref, cw_ref,
                o_ref, od_ref, kprev_ref, vprev_ref):
    step = pl.program_id(0)
    tb = q_ref.shape[0]
    win = WINDOW
    hd = ATTN_HEAD_DIM

    @pl.when(step == 0)
    def _():
        kprev_ref[...] = jnp.zeros_like(kprev_ref)
        vprev_ref[...] = jnp.zeros_like(vprev_ref)

    keep_prev = jnp.where(step == 0, 0.0, 1.0)
    c_h = cc_ref[...] * ch_ref[...]
    c_h_prev = ccp_ref[...] * chp_ref[...] * keep_prev
    od_ref[...] = (cb_ref[...] * _causal_conv(c_h, c_h_prev, cw_ref[...], 3)).astype(od_ref.dtype)

    cos = cos_ref[...]
    sin = sin_ref[...]
    lane = lax.broadcasted_iota(jnp.int32, (1, LANES), 1)
    low = (lane & (hd - 1)) < (hd // 2)
    first_half = lane < hd

    k_cur = k_ref[...]
    k_cur = k_cur * cos + _rotate_half(k_cur, low) * sin
    v_cur = v_ref[...]
    k_all = jnp.concatenate([kprev_ref[...], k_cur], axis=0)
    v_all = jnp.concatenate([vprev_ref[...], v_cur], axis=0)
    k_swap = pltpu.roll(k_all, hd, axis=1)
    v_swap = pltpu.roll(v_all, hd, axis=1)

    def placed(x, x_swap, kv_head, pos):
        src = x if kv_head == pos else x_swap
        keep = first_half if pos == 0 else ~first_half
        return jnp.where(keep, src, 0.0).astype(BF16)

    qi = lax.broadcasted_iota(jnp.int32, (win, 2 * win), 0)
    kj = lax.broadcasted_iota(jnp.int32, (win, 2 * win), 1)
    rel = qi + win - kj
    band = (rel >= 0) & (rel < win)
    band_first = band & ((kj >= win) | (step > 0))

    problems = [(j, h) for j in range(tb // win) for h in range(ATTN_Q_HEADS)]
    kv_of = lambda head: head // (ATTN_Q_HEADS // 2)
    pair_cols = lambda pair: slice(pair * LANES, (pair + 1) * LANES)
    k_at = {(g, pos): placed(k_all, k_swap, g, pos) for g in range(2) for pos in range(2)}
    v_at = {(g, pos): placed(v_all, v_swap, g, pos) for g in range(2) for pos in range(2)}
    q_pairs = []
    for pair in range(ATTN_Q_HEADS // 2):
        q_pair = q_ref[:, pair_cols(pair)]
        q_pairs.append((q_pair * cos + _rotate_half(q_pair, low) * sin).astype(BF16))
    sinks = [sink_ref[h:h + 1, 0:1] for _, h in problems]
    sc = [jnp.where(band_first if j == 0 else band,
                    _mm_nt(q_pairs[h // 2][j * win:(j + 1) * win],
                           k_at[kv_of(h), h % 2][j * win:(j + 2) * win]) * (hd ** -0.5),
                    -jnp.inf)
          for j, h in problems]
    mx = [jnp.maximum(jnp.max(s_, axis=-1, keepdims=True), sk) for s_, sk in zip(sc, sinks)]
    p = [jnp.exp(s_ - m_) for s_, m_ in zip(sc, mx)]
    denom = [jnp.sum(p_, axis=-1, keepdims=True) + jnp.exp(sk - m_) for p_, sk, m_ in zip(p, sinks, mx)]
    o = [_mm(p_, v_at[kv_of(h), h % 2][j * win:(j + 2) * win]) / d_
         for (j, h), p_, d_ in zip(problems, p, denom)]
    for idx in range(0, len(problems), 2):
        j, h = problems[idx]
        o_ref[j * win:(j + 1) * win, pair_cols(h // 2)] = (o[idx] + o[idx + 1]).astype(o_ref.dtype)

    kprev_ref[...] = k_cur[tb - win:]
    vprev_ref[...] = v_cur[tb - win:]


def _attention_and_short_conv(proj, layer, cos, sin, sink_rows, sc_conv_w):
    seq = proj.shape[0]
    tb = min(SWA_TB, seq)
    slab = lambda j: pl.BlockSpec((tb, GW), lambda i: (i, j))
    out = jax.ShapeDtypeStruct((seq, GW), BF16)
    return pl.pallas_call(
        _swa_kernel,
        grid=(seq // tb,),
        in_specs=[
            slab(SLAB_AQ),
            pl.BlockSpec((tb, LANES), lambda i: (i, SLAB128_AK)),
            pl.BlockSpec((tb, LANES), lambda i: (i, SLAB128_AV)),
            pl.BlockSpec((tb, LANES), lambda i: (i, 0)),
            pl.BlockSpec((tb, LANES), lambda i: (i, 0)),
            _layer_spec((ATTN_Q_HEADS, LANES), layer),
            slab(SLAB_CB), slab(SLAB_CC), slab(SLAB_CH),
            _prev_rows_spec(tb, GW, SLAB_CC), _prev_rows_spec(tb, GW, SLAB_CH),
            _layer_spec((3, GW), layer),
        ],
        out_specs=[pl.BlockSpec((tb, GW), lambda i: (i, 0))] * 2,
        out_shape=[out, out],
        scratch_shapes=[pltpu.VMEM((WINDOW, LANES), F32), pltpu.VMEM((WINDOW, LANES), F32)],
        compiler_params=_cparams("arbitrary"),
        name="sliding_window_attn",
    )(proj, proj, proj, cos, sin, sink_rows, proj, proj, proj, proj, proj, sc_conv_w)


OUTPROJ_TM = 512


def _outproj_kernel(ya_ref, yb_ref, yc_ref, yd_ref, w_ref, x_ref, gate_ref, nw_ref, o_ref):
    y = jnp.concatenate([ya_ref[...], yb_ref[...], yc_ref[...], yd_ref[...]], axis=1)
    y = jnp.dot(y, w_ref[...], preferred_element_type=F32)
    y = y * lax.rsqrt(jnp.mean(y * y, axis=-1, keepdims=True) + EPS) * nw_ref[...]
    o_ref[...] = x_ref[...] + gate_ref[...] * y


def _out_projection(ys, layer, w_bf16, x, mod, norm_w):
    seq, d = x.shape
    tm = min(OUTPROJ_TM, seq)
    return pl.pallas_call(
        _outproj_kernel,
        grid=(seq // tm,),
        in_specs=[pl.BlockSpec((tm, GW), lambda i: (i, 0))] * 4 + [
            _layer_spec((4 * GW, d), layer),
            pl.BlockSpec((tm, d), lambda i: (i, 0)),
            _mod_spec(MOD_GATE_A),
            _layer_spec((1, d), layer),
        ],
        out_specs=pl.BlockSpec((tm, d), lambda i: (i, 0)),
        out_shape=jax.ShapeDtypeStruct((seq, d), F32),
        compiler_params=_cparams("parallel"),
        name="out_proj",
    )(*ys, w_bf16, x, mod, norm_w)


MLP_TM = 1024
MLP_TH = 512


def _mlp_kernel(*refs, next_modulation):
    if next_modulation:
        (x_ref, nw_ref, sc_ref, sh_ref, wu_ref, wd_ref, gate_ref, pw_ref, c_ref, aw_ref, ab_ref,
         o_ref, mod_ref, h_ref) = refs
    else:
        x_ref, nw_ref, sc_ref, sh_ref, wu_ref, wd_ref, gate_ref, pw_ref, o_ref, h_ref = refs
    j = pl.program_id(1)

    @pl.when(j == 0)
    def _():
        _prenorm_to(x_ref, nw_ref, sc_ref, sh_ref, h_ref)
        o_ref[...] = jnp.zeros_like(o_ref)

    hid = _mm(h_ref[...], wu_ref[...])
    hid = jnp.square(jnp.maximum(hid, 0.0))
    o_ref[...] += _mm(hid, wd_ref[...])

    if next_modulation:
        prod = aw_ref[...] * _silu(c_ref[...])
        part = jnp.sum(prod.reshape(prod.shape[0] // SUBLANES, SUBLANES, LANES), axis=0)
        mod_ref[...] = jnp.sum(part, axis=0, keepdims=True) + ab_ref[...]

    @pl.when(j == pl.num_programs(1) - 1)
    def _():
        def body(r, carry):
            rows = pl.ds(pl.multiple_of(r * NORM_ROWS, NORM_ROWS), NORM_ROWS)
            y = o_ref[rows, :]
            y = y * lax.rsqrt(jnp.mean(y * y, axis=-1, keepdims=True) + EPS) * pw_ref[...]
            o_ref[rows, :] = x_ref[rows, :] + gate_ref[...] * y
            return carry

        lax.fori_loop(0, x_ref.shape[0] // NORM_ROWS, body, 0)


def _mlp(x, layer, norm_w, mod, w_up, w_down, post_w, next_modulation=None):
    seq, d = x.shape
    hidden = w_up.shape[2]
    tm = min(MLP_TM, seq)
    steps_per_tile = hidden // MLP_TH
    vmem = (2 * 2 * tm * d * 4 + tm * d * 2 + 2 * 2 * d * MLP_TH * w_up.dtype.itemsize
            + d * MLP_TH * 2 + tm * MLP_TH * 6)
    in_specs = [
        pl.BlockSpec((tm, d), lambda i, j: (i, 0)),
        _layer_spec((1, d), layer),
        _mod_spec(MOD_SCALE_M),
        _mod_spec(MOD_SHIFT_M),
        _layer_spec((d, MLP_TH), layer, 0, lambda i, j: j),
        _layer_spec((MLP_TH, d), layer, lambda i, j: j, 0),
        _mod_spec(MOD_GATE_M),
        _layer_spec((1, d), layer),
    ]
    out_specs = [pl.BlockSpec((tm, d), lambda i, j: (i, 0))]
    out_shape = [jax.ShapeDtypeStruct((seq, d), F32)]
    args = [x, norm_w, mod, mod, w_up, w_down, mod, post_w]
    if next_modulation:
        c_col, ada_w, ada_b = next_modulation
        n = ada_w.shape[2]
        assert (seq // tm) * steps_per_tile >= n // LANES
        col = lambda i, j: jnp.minimum(i * steps_per_tile + j, n // LANES - 1)
        in_specs += [pl.BlockSpec((d, 1), lambda i, j: (0, 0)),
                     pl.BlockSpec((None, d, LANES), lambda i, j: (layer + 1, 0, col(i, j))),
                     pl.BlockSpec((None, 1, LANES), lambda i, j: (layer + 1, 0, col(i, j)))]
        out_specs.append(pl.BlockSpec((None, 1, LANES), lambda i, j: (0, 0, col(i, j))))
        out_shape.append(jax.ShapeDtypeStruct((1, 1, n), F32))
        args += [c_col, ada_w, ada_b]
        vmem += 2 * 2 * d * LANES * 4
    outs = pl.pallas_call(
        functools.partial(_mlp_kernel, next_modulation=bool(next_modulation)),
        grid=(seq // tm, steps_per_tile),
        in_specs=in_specs,
        out_specs=out_specs,
        out_shape=out_shape,
        scratch_shapes=[pltpu.VMEM((tm, d), BF16)],
        compiler_params=_cparams("arbitrary", "arbitrary", vmem_limit=vmem),
        name="mlp",
    )(*args)
    return outs if next_modulation else outs[0]


def _gathered_group(w_t):
    parts = [w_t[:, IN_OFFSET[name][0]:IN_OFFSET[name][1]] for name in GATHERED_GROUP]
    used = sum(p.shape[1] for p in parts)
    parts.append(jnp.zeros((w_t.shape[0], IN_GROUP - used, w_t.shape[2]), w_t.dtype))
    return jnp.concatenate(parts, axis=1)


def kernel(x, c, positions, ada_w, ada_b, norm_pre_mix, norm_post_mix, norm_pre_mlp, norm_post_mlp,
           w_in, w_out, gdn_conv_w, gdn_a_log, gdn_dt_bias, gdn_norm_w, ssm_conv_w, ssm_conv_b,
           ssm_a_log, ssm_dt_bias, ssm_d, ssm_norm_w, attn_sinks, sc_conv_w, w_up, w_down):
    batch, seq, d = x.shape
    assert batch == 1 and d == D_MODEL
    depth = ada_w.shape[0]
    xs = x.reshape(seq, d)

    adaln = (c.reshape(d, 1), ada_w, ada_b.reshape(depth, 1, -1))
    mod = _modulation(*adaln, 0)
    cos, sin = _rope_tables(positions.reshape(seq, 1))

    rows = lambda v: v.reshape(depth, 1, -1)
    w_in_t = jnp.swapaxes(w_in, 1, 2).astype(BF16)
    w_in_tail = _gathered_group(w_in_t)
    w_out_bf16 = w_out.astype(BF16)
    gdn_lanes = _lane_params(LANE_ALPHA, gdn_a_log, gdn_dt_bias)
    ssm_lanes = _lane_params(LANE_DT, ssm_a_log, ssm_dt_bias, ssm_d)
    sink_rows = jnp.broadcast_to(attn_sinks.astype(F32)[:, :, None], (depth, ATTN_Q_HEADS, LANES))

    for i in range(depth):
        proj = _in_projection(xs, i, rows(norm_pre_mix), mod, w_in_t, w_in_tail)
        y_a = _gated_deltanet(proj, i, gdn_conv_w, gdn_lanes, rows(gdn_norm_w))
        y_b = _mamba2_ssd(proj, i, ssm_conv_w, rows(ssm_conv_b), ssm_lanes, rows(ssm_norm_w))
        y_c, y_d = _attention_and_short_conv(proj, i, cos, sin, sink_rows, sc_conv_w)
        xs = _out_projection((y_a, y_b, y_c, y_d), i, w_out_bf16, xs, mod, rows(norm_post_mix))
        if i + 1 < depth:
            xs, mod = _mlp(xs, i, rows(norm_pre_mlp), mod, w_up, w_down, rows(norm_post_mlp), next_modulation=adaln)
        else:
            xs = _mlp(xs, i, rows(norm_pre_mlp), mod, w_up, w_down, rows(norm_post_mlp))
    return xs.reshape(batch, seq, d)
```

```python
import functools

import jax
import jax.numpy as jnp
from jax import lax
from jax.experimental import pallas as pl
from jax.experimental.pallas import tpu as pltpu

F32 = jnp.float32
BF16 = jnp.bfloat16

D_MODEL = 2048
GW = 512
GDN_HEADS, GDN_HEAD_DIM, GDN_CHUNK = 4, 128, 64
SSM_HEADS, SSM_HEAD_DIM, SSM_STATE, SSM_CHUNK = 8, 64, 128, 128
ATTN_HEAD_DIM, ATTN_Q_HEADS, WINDOW = 64, 8, 128
ROPE_THETA = 10000.0
MLP_HIDDEN = 4 * D_MODEL
EPS = 1e-6

SUBLANES = 8
LANES = 128

_IN_SEGMENTS = (("gq", GW), ("gk", GW), ("gv", GW), ("gz", GW), ("gb", GDN_HEADS), ("ga", GDN_HEADS),
                ("sz", GW), ("sx", GW), ("sbc", 4 * SSM_STATE), ("sdt", SSM_HEADS),
                ("aq", GW), ("ak", 2 * ATTN_HEAD_DIM), ("av", 2 * ATTN_HEAD_DIM),
                ("cb", GW), ("cc", GW), ("ch", GW))
IN_OFFSET = {}
_off = 0
for _name, _size in _IN_SEGMENTS:
    IN_OFFSET[_name] = (_off, _off + _size)
    _off += _size
IN_WIDTH = _off

IN_GROUP = 3 * GW
CONTIGUOUS_GROUPS = (("gq", "gk", "gv"), ("sz", "sx", "sbc"), ("cb", "cc", "ch"))
GATHERED_GROUP = ("gz", "aq", "ak", "av", "gb", "ga", "sdt")
IN_COLS = (len(CONTIGUOUS_GROUPS) + 1) * IN_GROUP
for _grp in CONTIGUOUS_GROUPS:
    assert IN_OFFSET[_grp[0]][0] % SUBLANES == 0
    assert all(IN_OFFSET[a][1] == IN_OFFSET[b][0] for a, b in zip(_grp, _grp[1:]))
(SLAB_GQ, SLAB_GK, SLAB_GV, SLAB_SZ, SLAB_SX, SLAB_SBC, SLAB_CB, SLAB_CC, SLAB_CH, SLAB_GZ, SLAB_AQ) = range(11)
SLAB128_AK, SLAB128_AV, SLAB128_SMALL = 44, 45, 46
LANE_BETA, LANE_ALPHA, LANE_DT = 0, 4, 8

MIB = 1024 * 1024
V7X_VMEM_BYTES = 64 * MIB
VMEM_LIMIT = 56 * MIB


def _cparams(*sem, vmem_limit=VMEM_LIMIT):
    assert vmem_limit < V7X_VMEM_BYTES
    return pltpu.CompilerParams(dimension_semantics=sem, vmem_limit_bytes=vmem_limit)


def _sigmoid(x):
    return jax.nn.sigmoid(x)


def _silu(x):
    return x * jax.nn.sigmoid(x)


def _mm(a, b):
    return jnp.dot(a.astype(BF16), b.astype(BF16), preferred_element_type=F32)


def _mm_nt(a, b):
    return lax.dot_general(a.astype(BF16), b.astype(BF16), (((1,), (1,)), ((), ())),
                           preferred_element_type=F32)


def _mm_tn(a, b):
    return lax.dot_general(a.astype(BF16), b.astype(BF16), (((0,), (0,)), ((), ())),
                           preferred_element_type=F32)


def _blk(idx, size):
    assert size & (size - 1) == 0
    return lax.shift_right_logical(idx, size.bit_length() - 1)


def _split3(x):
    x1 = x.astype(BF16)
    r1 = x - x1.astype(F32)
    x2 = r1.astype(BF16)
    r2 = r1 - x2.astype(F32)
    return x1, x2, r2.astype(BF16)


def _chunk_cumsum(g, chunk):
    rows = g.shape[0]
    r = lax.broadcasted_iota(jnp.int32, (2 * rows, rows), 0)
    s = lax.broadcasted_iota(jnp.int32, (2 * rows, rows), 1)
    rr = jnp.where(r >= rows, r - rows, r)
    same = _blk(rr, chunk) == _blk(s, chunk)
    sel = same & ((r >= rows) | (s <= rr))
    mat = jnp.where(sel, 1.0, 0.0).astype(BF16)
    acc = None
    for part in _split3(g):
        t = jnp.dot(mat, part, preferred_element_type=F32)
        acc = t if acc is None else acc + t
    return acc[:rows], acc[rows:]


def _causal_conv(x, prev, w, taps):
    rows = x.shape[0]
    xp = jnp.concatenate([prev, x], axis=0)
    acc = x * w[taps - 1:taps]
    for d in range(1, taps):
        acc = acc + xp[SUBLANES - d:SUBLANES - d + rows] * w[taps - 1 - d:taps - d]
    return acc


def _causal_conv_staged(x_ref, prev, w, taps, stage_ref):
    rows = x_ref.shape[0]
    x = x_ref[...]
    stage_ref[0:SUBLANES, :] = prev
    stage_ref[SUBLANES:, :] = x
    acc = x * w[taps - 1:taps]
    for d in range(1, taps):
        acc = acc + stage_ref[pl.ds(SUBLANES - d, rows), :] * w[taps - 1 - d:taps - d]
    return acc


MOD_TN = 1536
MOD_ROWS = 64


def _mod_rope_kernel(c_ref, w_ref, b_ref, pos_ref, invf_ref, sign_ref, o_ref, cos_ref, sin_ref):
    d = c_ref.shape[0]

    def body(r, acc):
        rows = pl.ds(pl.multiple_of(r * MOD_ROWS, MOD_ROWS), MOD_ROWS)
        cc = _silu(c_ref[rows, :])
        prod = w_ref[0, rows, :] * cc
        return acc + jnp.sum(prod.reshape(MOD_ROWS // SUBLANES, SUBLANES, MOD_TN), axis=0)

    acc = lax.fori_loop(0, d // MOD_ROWS, body, jnp.zeros((SUBLANES, MOD_TN), F32))
    o_ref[0] = jnp.sum(acc, axis=0, keepdims=True) + b_ref[0]

    ang = pos_ref[...].astype(F32) * invf_ref[...]
    cos_ref[...] = jnp.cos(ang)
    sin_ref[...] = jnp.sin(ang) * sign_ref[...]


def _modulation_and_rope(c_col, ada_w, ada_b, layer, pos_col):
    depth, d, n = ada_w.shape
    seq = pos_col.shape[0]
    steps = n // MOD_TN
    tb = seq // steps
    assert steps * tb == seq and tb % SUBLANES == 0
    half = ATTN_HEAD_DIM // 2
    inv_freq = ROPE_THETA ** (-jnp.arange(0, ATTN_HEAD_DIM, 2, dtype=F32) / ATTN_HEAD_DIM)
    invf = jnp.tile(inv_freq, LANES // half).reshape(1, LANES)
    lane = jnp.arange(LANES)
    sign = jnp.where((lane % ATTN_HEAD_DIM) < half, -1.0, 1.0).astype(F32).reshape(1, LANES)
    table = jax.ShapeDtypeStruct((seq, LANES), F32)
    return pl.pallas_call(
        _mod_rope_kernel,
        grid=(steps,),
        in_specs=[
            pl.BlockSpec((d, 1), lambda j: (0, 0)),
            pl.BlockSpec((1, d, MOD_TN), lambda j: (layer, 0, j)),
            pl.BlockSpec((1, 1, MOD_TN), lambda j: (layer, 0, j)),
            pl.BlockSpec((tb, 1), lambda j: (j, 0)),
            pl.BlockSpec((1, LANES), lambda j: (0, 0)),
            pl.BlockSpec((1, LANES), lambda j: (0, 0)),
        ],
        out_specs=[pl.BlockSpec((1, 1, MOD_TN), lambda j: (0, 0, j)),
                   pl.BlockSpec((tb, LANES), lambda j: (j, 0)),
                   pl.BlockSpec((tb, LANES), lambda j: (j, 0))],
        out_shape=[jax.ShapeDtypeStruct((1, 1, n), F32), table, table],
        compiler_params=_cparams("parallel"),
        name="adaln_mod_rope",
    )(c_col, ada_w, ada_b, pos_col, invf, sign)


def _mod_spec(k):
    return pl.BlockSpec((None, 1, D_MODEL), lambda *grid_idx: (0, 0, k))


NORM_ROWS = 128


def _prenorm_to(x_ref, nw_ref, sc_ref, sh_ref, h_ref):
    tm = x_ref.shape[0]

    def body(r, carry):
        rows = pl.ds(pl.multiple_of(r * NORM_ROWS, NORM_ROWS), NORM_ROWS)
        x = x_ref[rows, :]
        y = x * lax.rsqrt(jnp.mean(x * x, axis=-1, keepdims=True) + EPS) * nw_ref[...]
        h_ref[rows, :] = (y * (1.0 + sc_ref[...]) + sh_ref[...]).astype(BF16)
        return carry

    lax.fori_loop(0, tm // NORM_ROWS, body, 0)


INPROJ_TM = 1024

MOD_SHIFT_A, MOD_SCALE_A, MOD_GATE_A, MOD_SHIFT_M, MOD_SCALE_M, MOD_GATE_M = range(6)


def _layer_spec(shape, layer, *rest):
    rest = rest or (0,) * len(shape)

    def index_map(*grid_idx):
        return (layer,) + tuple(r(*grid_idx) if callable(r) else r for r in rest)

    return pl.BlockSpec((None,) + tuple(shape), index_map)


def _inproj_kernel(x_ref, nw_ref, sc_ref, sh_ref, wt_ref, tail_ref, o_ref, h_ref):
    j = pl.program_id(1)

    @pl.when(j == 0)
    def _():
        _prenorm_to(x_ref, nw_ref, sc_ref, sh_ref, h_ref)

    def project(w_ref):
        o_ref[...] = _mm_nt(h_ref[...], w_ref[...])

    @pl.when(j < len(CONTIGUOUS_GROUPS))
    def _():
        project(wt_ref)

    @pl.when(j == len(CONTIGUOUS_GROUPS))
    def _():
        project(tail_ref)


def _group_row(j):
    starts = [IN_OFFSET[grp[0]][0] // SUBLANES for grp in CONTIGUOUS_GROUPS]
    row8 = starts[-1]
    for k in range(len(starts) - 2, -1, -1):
        row8 = jnp.where(j <= k, starts[k], row8)
    return row8 * SUBLANES


def _in_projection(x, layer, norm_w, mod, w_t, w_gathered):
    seq, d = x.shape
    tm = min(INPROJ_TM, seq)
    return pl.pallas_call(
        _inproj_kernel,
        grid=(seq // tm, IN_COLS // IN_GROUP),
        in_specs=[
            pl.BlockSpec((tm, d), lambda i, j: (i, 0)),
            _layer_spec((1, d), layer),
            _mod_spec(MOD_SCALE_A),
            _mod_spec(MOD_SHIFT_A),
            pl.BlockSpec((pl.Element(IN_GROUP), pl.Element(d)), lambda i, j: (_group_row(j), 0)),
            pl.BlockSpec((IN_GROUP, d), lambda i, j: (0, 0), pipeline_mode=pl.Buffered(1)),
        ],
        out_specs=pl.BlockSpec((tm, IN_GROUP), lambda i, j: (i, j)),
        out_shape=jax.ShapeDtypeStruct((seq, IN_COLS), F32),
        scratch_shapes=[pltpu.VMEM((tm, d), BF16)],
        compiler_params=_cparams("parallel", "arbitrary"),
        name="in_proj",
    )(x, norm_w, mod, mod, w_t, w_gathered)


GDN_TB = 512


def _l2norm(x):
    return x * lax.rsqrt(jnp.sum(x * x, axis=-1, keepdims=True) + EPS)


GDN_BLK = 128
GDN_PREP_STAGES_PER_CHUNK = 4


def _unit_lower_inverse_minus_identity(ms, r, s):
    same8 = _blk(r, 8) == _blk(s, 8)
    same16 = _blk(r, 16) == _blk(s, 16)
    same32 = _blk(r, 32) == _blk(s, 32)
    n1 = [jnp.where(same8, -m, 0.0) for m in ms]
    n2 = [_mm(a, a) for a in n1]
    yield
    n3 = [_mm(a, b) for a, b in zip(n1, n2)]
    n4 = [_mm(b, b) for b in n2]
    yield
    ys = [a + b + c for a, b, c in zip(n1, n2, n3)]
    ts = [_mm(y, d) for y, d in zip(ys, n4)]
    yield
    ys = [y + d + t for y, d, t in zip(ys, n4, ts)]
    for sel in (same16 & ~same8, same32 & ~same16, ~same32):
        blks = [jnp.where(sel, m, 0.0) for m in ms]
        cs = [b + _mm(y, b) for y, b in zip(ys, blks)]
        yield
        ts = [_mm(c, y) for c, y in zip(cs, ys)]
        yield
        ys = [y - c - t for y, c, t in zip(ys, cs, ts)]
    return ys


def _run_interleaved(main, side, main_stages_per_side_stage):
    side_live = True
    done = 0
    while True:
        try:
            next(main)
        except StopIteration as stop:
            result = stop.value
            break
        done += 1
        if side_live and done % main_stages_per_side_stage == 0:
            side_live = next(side, StopIteration) is not StopIteration
    while side_live:
        side_live = next(side, StopIteration) is not StopIteration
    return result


def _gdn_prepare(q_ref, k_ref, v_ref, qp_ref, kp_ref, vp_ref, sm_ref, cw_ref, lp_ref, conv_ref, first_block):
    tb = q_ref.shape[0]
    hd = GDN_HEAD_DIM
    nblk = tb // GDN_BLK

    keep_prev = jnp.where(first_block, 0.0, 1.0)
    cw = cw_ref[...]

    def conv_silu(x_ref, p_ref, idx):
        w = cw[:, idx * GW:(idx + 1) * GW]
        return _silu(_causal_conv_staged(x_ref, p_ref[...] * keep_prev, w, 4, conv_ref.at[idx]))

    q = conv_silu(q_ref, qp_ref, 0)
    yield
    k = conv_silu(k_ref, kp_ref, 1)
    yield
    v = conv_silu(v_ref, vp_ref, 2)
    yield

    small = sm_ref[...]
    a_log = lp_ref[0:1, :]
    dt_bias = lp_ref[1:2, :]
    beta_all = _sigmoid(small)
    g_all = -jnp.exp(a_log) * jax.nn.softplus(small + dt_bias)
    gc_all, gtot_all = _chunk_cumsum(g_all, GDN_CHUNK)
    gc_t = gc_all.T
    yield

    r = lax.broadcasted_iota(jnp.int32, (GDN_BLK, GDN_BLK), 0)
    s = lax.broadcasted_iota(jnp.int32, (GDN_BLK, GDN_BLK), 1)
    same_chunk = _blk(r, GDN_CHUNK) == _blk(s, GDN_CHUNK)
    mask_incl = same_chunk & (s <= r)
    mask_strict = same_chunk & (s < r)

    problems = [(slice(b * GDN_BLK, (b + 1) * GDN_BLK), h) for b in range(nblk) for h in range(GDN_HEADS)]
    head_cols = lambda h: slice(h * hd, (h + 1) * hd)
    gate = lambda arr, rows, lane: arr[rows, lane:lane + 1]

    qn = [_l2norm(q[rows, head_cols(h)]) * (hd ** -0.5) for rows, h in problems]
    kn = [_l2norm(k[rows, head_cols(h)]) for rows, h in problems]
    beta = [gate(beta_all, rows, LANE_BETA + h) for rows, h in problems]
    gcol = [gate(gc_all, rows, LANE_ALPHA + h) for rows, h in problems]
    gtot = [gate(gtot_all, rows, LANE_ALPHA + h) for rows, h in problems]
    grow = [gc_t[LANE_ALPHA + h:LANE_ALPHA + h + 1, rows] for rows, h in problems]
    yield

    decay = [jnp.exp(jnp.where(mask_incl, gc - gr, -jnp.inf)) for gc, gr in zip(gcol, grow)]
    k_beta = [kk * b for kk, b in zip(kn, beta)]
    k16 = [kk.astype(BF16) for kk in kn]
    yield
    ms = [jnp.where(mask_strict, _mm_nt(kb, kk) * d, 0.0) for kb, kk, d in zip(k_beta, k16, decay)]
    yield
    attn = [(_mm_nt(qq, kk) * d).astype(BF16) for qq, kk, d in zip(qn, k16, decay)]
    ys = yield from _unit_lower_inverse_minus_identity(ms, r, s)
    yield

    e_gc = [jnp.exp(gc) for gc in gcol]
    rhs = [jnp.concatenate([v[rows, head_cols(h)] * b, kb * e], axis=1)
           for (rows, h), b, kb, e in zip(problems, beta, k_beta, e_gc)]
    uw = [x + _mm(y, x) for y, x in zip(ys, rhs)]
    yield
    u = [x[:, :hd] for x in uw]
    w = [x[:, hd:].astype(BF16) for x in uw]
    q_dec = [(qq * e).astype(BF16) for qq, e in zip(qn, e_gc)]
    k_dec = [(kk * jnp.exp(gt - gc)).astype(BF16) for kk, gt, gc in zip(kn, gtot, gcol)]
    return u, w, q_dec, k_dec, attn, gtot_all


def _gdn_recurrence(u, w, q_dec, k_dec, attn, gtot_all, state_ref, z, nw_ref, o_ref):
    hd = GDN_HEAD_DIM
    nblk = o_ref.shape[0] // GDN_BLK
    chunks_per_blk = GDN_BLK // GDN_CHUNK
    head_cols = lambda h: slice(h * hd, (h + 1) * hd)

    states = [state_ref[h] for h in range(GDN_HEADS)]
    v_new = [[] for _ in range(nblk * GDN_HEADS)]
    o_inter = [[] for _ in range(nblk * GDN_HEADS)]
    for b in range(nblk):
        for c in range(chunks_per_blk):
            rows = slice(c * GDN_CHUNK, (c + 1) * GDN_CHUNK)
            first = b * GDN_BLK + c * GDN_CHUNK
            for h in range(GDN_HEADS):
                p = b * GDN_HEADS + h
                st16 = states[h].astype(BF16)
                vn = u(p, rows) - jnp.dot(w(p, rows), st16, preferred_element_type=F32)
                o_inter[p].append(jnp.dot(q_dec(p, rows), st16, preferred_element_type=F32))
                chunk_dec = jnp.exp(gtot_all[first:first + 1, LANE_ALPHA + h:LANE_ALPHA + h + 1])
                states[h] = states[h] * chunk_dec + _mm_tn(k_dec(p, rows), vn)
                v_new[p].append(vn)
            yield
        rows = slice(b * GDN_BLK, (b + 1) * GDN_BLK)
        for h in range(GDN_HEADS):
            p = b * GDN_HEADS + h
            o = jnp.concatenate(o_inter[p], axis=0) + _mm(attn(p, slice(None)), jnp.concatenate(v_new[p], axis=0))
            o = o * lax.rsqrt(jnp.mean(o * o, axis=-1, keepdims=True) + EPS) * nw_ref[...]
            o_ref[rows, head_cols(h)] = (o * _silu(z[rows, head_cols(h)])).astype(o_ref.dtype)
    for h in range(GDN_HEADS):
        state_ref[h] = states[h]


def _gdn_kernel(q_ref, k_ref, v_ref, z_ref, qp_ref, kp_ref, vp_ref, sm_ref, cw_ref, lp_ref, nw_ref,
                o_ref, state_ref, conv_ref, u_ref, w_ref, qd_ref, kd_ref, at_ref, gt_ref):
    step = pl.program_id(0)
    write_slot = step % 2
    read_slot = 1 - write_slot
    n_problems = u_ref.shape[1]

    @pl.when(step == 0)
    def _():
        state_ref[...] = jnp.zeros_like(state_ref)
        for ref in (u_ref, w_ref, qd_ref, kd_ref, at_ref, gt_ref):
            ref[1] = jnp.zeros(ref.shape[1:], ref.dtype)

    recurrence = _gdn_recurrence(*[(lambda p, rows, ref=ref: ref[read_slot, p, rows])
                                   for ref in (u_ref, w_ref, qd_ref, kd_ref, at_ref)],
                                 gt_ref[read_slot], state_ref, z_ref[...], nw_ref, o_ref)
    prepare = _gdn_prepare(q_ref, k_ref, v_ref, qp_ref, kp_ref, vp_ref, sm_ref, cw_ref, lp_ref, conv_ref, step == 0)
    u, w, q_dec, k_dec, attn, gtot_all = _run_interleaved(prepare, recurrence, GDN_PREP_STAGES_PER_CHUNK)
    for p in range(n_problems):
        u_ref[write_slot, p] = u[p]
        w_ref[write_slot, p] = w[p]
        qd_ref[write_slot, p] = q_dec[p]
        kd_ref[write_slot, p] = k_dec[p]
        at_ref[write_slot, p] = attn[p]
    gt_ref[write_slot] = gtot_all


def _prev_rows_spec(tb, width, slab):
    blocks = tb // SUBLANES
    return pl.BlockSpec((SUBLANES, width), lambda i: (jnp.maximum(i * blocks - 1, 0), slab))


def _lane_params(lane0, *rows):
    stacked = jnp.stack([r.astype(F32) for r in rows], axis=1)
    nrows, heads = stacked.shape[1:]
    return jnp.pad(stacked, ((0, 0), (0, SUBLANES - nrows), (lane0, LANES - lane0 - heads)))


def _gated_deltanet(proj, layer, conv_w, lane_params, norm_w):
    seq = proj.shape[0]
    tb = min(GDN_TB, seq)
    nsteps = seq // tb
    n_problems = (tb // GDN_BLK) * GDN_HEADS
    prep = lambda t: jnp.minimum(t, nsteps - 1)
    done = lambda t: jnp.maximum(t - 1, 0)
    slab = lambda j: pl.BlockSpec((tb, GW), lambda t: (prep(t), j))
    prev = lambda j: pl.BlockSpec((SUBLANES, GW),
                                  lambda t: (jnp.maximum(prep(t) * (tb // SUBLANES) - 1, 0), j))
    full = lambda shape: _layer_spec(shape, layer)
    factor = lambda dtype: pltpu.VMEM((2, n_problems, GDN_BLK, GDN_HEAD_DIM), dtype)
    return pl.pallas_call(
        _gdn_kernel,
        grid=(nsteps + 1,),
        in_specs=[
            slab(SLAB_GQ), slab(SLAB_GK), slab(SLAB_GV),
            pl.BlockSpec((tb, GW), lambda t: (done(t), SLAB_GZ)),
            prev(SLAB_GQ), prev(SLAB_GK), prev(SLAB_GV),
            pl.BlockSpec((tb, LANES), lambda t: (prep(t), SLAB128_SMALL)),
            full((4, 3 * GW)), full((SUBLANES, LANES)), full((1, GDN_HEAD_DIM)),
        ],
        out_specs=pl.BlockSpec((tb, GW), lambda t: (done(t), 0)),
        out_shape=jax.ShapeDtypeStruct((seq, GW), BF16),
        scratch_shapes=[pltpu.VMEM((GDN_HEADS, GDN_HEAD_DIM, GDN_HEAD_DIM), F32),
                        pltpu.VMEM((3, tb + SUBLANES, GW), F32),
                        factor(F32), factor(BF16), factor(BF16), factor(BF16), factor(BF16),
                        pltpu.VMEM((2, tb, LANES), F32)],
        compiler_params=_cparams("arbitrary"),
        name="gated_deltanet",
    )(proj, proj, proj, proj, proj, proj, proj, proj, conv_w, lane_params, norm_w)


SSD_TB = 256


def _ssd_kernel(z_ref, x_ref, bc_ref, xp_ref, bcp_ref, sm_ref, cw_ref, cb_ref, lp_ref, nw_ref,
                o_ref, state_ref):
    step = pl.program_id(0)
    tb = z_ref.shape[0]
    nstate = SSM_STATE
    chunk = SSM_CHUNK
    nchunks = tb // chunk
    n_pairs = SSM_HEADS // 2
    pairs_per_group = n_pairs // 2

    @pl.when(step == 0)
    def _():
        state_ref[...] = jnp.zeros_like(state_ref)

    keep_prev = jnp.where(step == 0, 0.0, 1.0)
    cw = cw_ref[...]
    cbias = cb_ref[...]
    xs = _silu(_causal_conv(x_ref[...], xp_ref[...] * keep_prev, cw[:, :GW], 4) + cbias[:, :GW])
    bc = _silu(_causal_conv(bc_ref[...], bcp_ref[...] * keep_prev, cw[:, GW:], 4) + cbias[:, GW:])
    z = z_ref[...]

    small = sm_ref[...]
    a_neg = -jnp.exp(lp_ref[0:1, :])
    dt_all = jax.nn.softplus(small + lp_ref[1:2, :])
    dskip_all = lp_ref[2:3, :]
    gc_all, gtot_all = _chunk_cumsum(dt_all * a_neg, chunk)
    gc_t = gc_all.T

    r = lax.broadcasted_iota(jnp.int32, (chunk, chunk), 0)
    s = lax.broadcasted_iota(jnp.int32, (chunk, chunk), 1)
    mask_incl = s <= r
    lane = lax.broadcasted_iota(jnp.int32, (1, LANES), 1)
    first_half = lane < SSM_HEAD_DIM
    by_half = lambda a, b: jnp.where(first_half, a, b)

    problems = [(c, p) for c in range(nchunks) for p in range(n_pairs)]
    rows_of = lambda c: slice(c * chunk, (c + 1) * chunk)
    cols_of = lambda p: slice(p * LANES, (p + 1) * LANES)
    lanes_of = lambda p: (LANE_DT + 2 * p, LANE_DT + 2 * p + 1)
    col = lambda arr, c, l: arr[rows_of(c), l:l + 1]

    b_mat = {(c, g): bc[rows_of(c), g * nstate:(g + 1) * nstate].astype(BF16)
             for c in range(nchunks) for g in range(2)}
    c_mat = {(c, g): bc[rows_of(c), (2 + g) * nstate:(3 + g) * nstate].astype(BF16)
             for c in range(nchunks) for g in range(2)}
    cb = {key: _mm_nt(c_mat[key], b_mat[key]) for key in b_mat}

    x_pair = [xs[rows_of(c), cols_of(p)] for c, p in problems]
    xdt = [x * by_half(col(dt_all, c, lanes_of(p)[0]), col(dt_all, c, lanes_of(p)[1]))
           for x, (c, p) in zip(x_pair, problems)]
    y_diag = []
    for (c, p), xd in zip(problems, xdt):
        halves = (by_half(xd, 0.0), by_half(0.0, xd))
        acc = None
        for l, half in zip(lanes_of(p), halves):
            grow = gc_t[l:l + 1, rows_of(c)]
            lmat = jnp.exp(jnp.where(mask_incl, col(gc_all, c, l) - grow, -jnp.inf))
            t = _mm(cb[c, p // pairs_per_group] * lmat, half)
            acc = t if acc is None else acc + t
        y_diag.append(acc)

    e_in = [by_half(jnp.exp(col(gc_all, c, lanes_of(p)[0])), jnp.exp(col(gc_all, c, lanes_of(p)[1])))
            for c, p in problems]
    e_out = [by_half(jnp.exp(col(gtot_all, c, lanes_of(p)[0]) - col(gc_all, c, lanes_of(p)[0])),
                     jnp.exp(col(gtot_all, c, lanes_of(p)[1]) - col(gc_all, c, lanes_of(p)[1])))
             for c, p in problems]
    states = [state_ref[p] for p in range(n_pairs)]
    y_off = []
    for idx, (c, p) in enumerate(problems):
        la, lb = lanes_of(p)
        g = p // pairs_per_group
        y_off.append(e_in[idx] * _mm(c_mat[c, g], states[p]))
        first = c * chunk
        cd_lane = by_half(jnp.exp(gtot_all[first:first + 1, la:la + 1]),
                          jnp.exp(gtot_all[first:first + 1, lb:lb + 1]))
        states[p] = states[p] * cd_lane + _mm_tn(b_mat[c, g], xdt[idx] * e_out[idx])
    for p in range(n_pairs):
        state_ref[p] = states[p]

    ys = []
    for idx, (c, p) in enumerate(problems):
        la, lb = lanes_of(p)
        y = y_diag[idx] + y_off[idx] + x_pair[idx] * by_half(dskip_all[:, la:la + 1], dskip_all[:, lb:lb + 1])
        ys.append(y * _silu(z[rows_of(c), cols_of(p)]))
    sumsq = [jnp.sum(y * y, axis=-1, keepdims=True) for y in ys]
    for c in range(nchunks):
        for g in range(2):
            members = [c * n_pairs + g * pairs_per_group + k for k in range(pairs_per_group)]
            total = sumsq[members[0]]
            for idx in members[1:]:
                total = total + sumsq[idx]
            inv = lax.rsqrt(total / (pairs_per_group * LANES) + EPS)
            for idx in members:
                cols = cols_of(problems[idx][1])
                o_ref[rows_of(c), cols] = (ys[idx] * inv * nw_ref[:, cols]).astype(o_ref.dtype)


def _mamba2_ssd(proj, layer, conv_w, conv_b, lane_params, norm_w):
    seq = proj.shape[0]
    tb = min(SSD_TB, seq)
    slab = lambda j: pl.BlockSpec((tb, GW), lambda i: (i, j))
    full = lambda shape: _layer_spec(shape, layer)
    return pl.pallas_call(
        _ssd_kernel,
        grid=(seq // tb,),
        in_specs=[
            slab(SLAB_SZ), slab(SLAB_SX), slab(SLAB_SBC),
            _prev_rows_spec(tb, GW, SLAB_SX), _prev_rows_spec(tb, GW, SLAB_SBC),
            pl.BlockSpec((tb, LANES), lambda i: (i, SLAB128_SMALL)),
            full((4, 2 * GW)), full((1, 2 * GW)), full((SUBLANES, LANES)), full((1, GW)),
        ],
        out_specs=pl.BlockSpec((tb, GW), lambda i: (i, 0)),
        out_shape=jax.ShapeDtypeStruct((seq, GW), BF16),
        scratch_shapes=[pltpu.VMEM((SSM_HEADS // 2, SSM_STATE, LANES), F32)],
        compiler_params=_cparams("arbitrary"),
        name="mamba2_ssd",
    )(proj, proj, proj, proj, proj, proj, conv_w, conv_b, lane_params, norm_w)


def _rotate_half(x, lane_in_head_low):
    width = x.shape[1]
    half = ATTN_HEAD_DIM // 2
    ahead = pltpu.roll(x, width - half, axis=1)
    behind = pltpu.roll(x, half, axis=1)
    return jnp.where(lane_in_head_low, ahead, behind)


SWA_TB = 512


def _swa_kernel(q_ref, k_ref, v_ref, cos_ref, sin_ref, sink_ref, cb_ref, cc_ref, ch_ref, ccp_ref, chp_ref, cw_ref,
                o_ref, od_ref, kprev_ref, vprev_ref):
    step = pl.program_id(0)
    tb = q_ref.shape[0]
    win = WINDOW
    hd = ATTN_HEAD_DIM

    @pl.when(step == 0)
    def _():
        kprev_ref[...] = jnp.zeros_like(kprev_ref)
        vprev_ref[...] = jnp.zeros_like(vprev_ref)

    keep_prev = jnp.where(step == 0, 0.0, 1.0)
    c_h = cc_ref[...] * ch_ref[...]
    c_h_prev = ccp_ref[...] * chp_ref[...] * keep_prev
    od_ref[...] = (cb_ref[...] * _causal_conv(c_h, c_h_prev, cw_ref[...], 3)).astype(od_ref.dtype)

    cos = cos_ref[...]
    sin = sin_ref[...]
    lane = lax.broadcasted_iota(jnp.int32, (1, LANES), 1)
    low = (lane & (hd - 1)) < (hd // 2)
    first_half = lane < hd

    k_cur = k_ref[...]
    k_cur = k_cur * cos + _rotate_half(k_cur, low) * sin
    v_cur = v_ref[...]
    k_all = jnp.concatenate([kprev_ref[...], k_cur], axis=0)
    v_all = jnp.concatenate([vprev_ref[...], v_cur], axis=0)
    k_swap = pltpu.roll(k_all, hd, axis=1)
    v_swap = pltpu.roll(v_all, hd, axis=1)

    def placed(x, x_swap, kv_head, pos):
        src = x if kv_head == pos else x_swap
        keep = first_half if pos == 0 else ~first_half
        return jnp.where(keep, src, 0.0).astype(BF16)

    qi = lax.broadcasted_iota(jnp.int32, (win, 2 * win), 0)
    kj = lax.broadcasted_iota(jnp.int32, (win, 2 * win), 1)
    rel = qi + win - kj
    band = (rel >= 0) & (rel < win)
    band_first = band & ((kj >= win) | (step > 0))

    problems = [(j, h) for j in range(tb // win) for h in range(ATTN_Q_HEADS)]
    kv_of = lambda head: head // (ATTN_Q_HEADS // 2)
    pair_cols = lambda pair: slice(pair * LANES, (pair + 1) * LANES)
    k_at = {(g, pos): placed(k_all, k_swap, g, pos) for g in range(2) for pos in range(2)}
    v_at = {(g, pos): placed(v_all, v_swap, g, pos) for g in range(2) for pos in range(2)}
    q_pairs = []
    for pair in range(ATTN_Q_HEADS // 2):
        q_pair = q_ref[:, pair_cols(pair)]
        q_pairs.append((q_pair * cos + _rotate_half(q_pair, low) * sin).astype(BF16))
    sinks = [sink_ref[h:h + 1, 0:1] for _, h in problems]
    sc = [jnp.where(band_first if j == 0 else band,
                    _mm_nt(q_pairs[h // 2][j * win:(j + 1) * win],
                           k_at[kv_of(h), h % 2][j * win:(j + 2) * win]) * (hd ** -0.5),
                    -jnp.inf)
          for j, h in problems]
    mx = [jnp.maximum(jnp.max(s_, axis=-1, keepdims=True), sk) for s_, sk in zip(sc, sinks)]
    p = [jnp.exp(s_ - m_) for s_, m_ in zip(sc, mx)]
    denom = [jnp.sum(p_, axis=-1, keepdims=True) + jnp.exp(sk - m_) for p_, sk, m_ in zip(p, sinks, mx)]
    o = [_mm(p_, v_at[kv_of(h), h % 2][j * win:(j + 2) * win]) / d_
         for (j, h), p_, d_ in zip(problems, p, denom)]
    for idx in range(0, len(problems), 2):
        j, h = problems[idx]
        o_ref[j * win:(j + 1) * win, pair_cols(h // 2)] = (o[idx] + o[idx + 1]).astype(o_ref.dtype)

    kprev_ref[...] = k_cur[tb - win:]
    vprev_ref[...] = v_cur[tb - win:]


def _attention_and_short_conv(proj, layer, cos, sin, sink_rows, sc_conv_w):
    seq = proj.shape[0]
    tb = min(SWA_TB, seq)
    slab = lambda j: pl.BlockSpec((tb, GW), lambda i: (i, j))
    out = jax.ShapeDtypeStruct((seq, GW), BF16)
    return pl.pallas_call(
        _swa_kernel,
        grid=(seq // tb,),
        in_specs=[
            slab(SLAB_AQ),
            pl.BlockSpec((tb, LANES), lambda i: (i, SLAB128_AK)),
            pl.BlockSpec((tb, LANES), lambda i: (i, SLAB128_AV)),
            pl.BlockSpec((tb, LANES), lambda i: (i, 0)),
            pl.BlockSpec((tb, LANES), lambda i: (i, 0)),
            _layer_spec((ATTN_Q_HEADS, LANES), layer),
            slab(SLAB_CB), slab(SLAB_CC), slab(SLAB_CH),
            _prev_rows_spec(tb, GW, SLAB_CC), _prev_rows_spec(tb, GW, SLAB_CH),
            _layer_spec((3, GW), layer),
        ],
        out_specs=[pl.BlockSpec((tb, GW), lambda i: (i, 0))] * 2,
        out_shape=[out, out],
        scratch_shapes=[pltpu.VMEM((WINDOW, LANES), F32), pltpu.VMEM((WINDOW, LANES), F32)],
        compiler_params=_cparams("arbitrary"),
        name="sliding_window_attn",
    )(proj, proj, proj, cos, sin, sink_rows, proj, proj, proj, proj, proj, sc_conv_w)


OUTPROJ_TM = 512


def _outproj_kernel(ya_ref, yb_ref, yc_ref, yd_ref, w_ref, x_ref, gate_ref, nw_ref, o_ref):
    y = jnp.concatenate([ya_ref[...], yb_ref[...], yc_ref[...], yd_ref[...]], axis=1)
    y = jnp.dot(y, w_ref[...], preferred_element_type=F32)
    y = y * lax.rsqrt(jnp.mean(y * y, axis=-1, keepdims=True) + EPS) * nw_ref[...]
    o_ref[...] = x_ref[...] + gate_ref[...] * y


def _out_projection(ys, layer, w_bf16, x, mod, norm_w):
    seq, d = x.shape
    tm = min(OUTPROJ_TM, seq)
    return pl.pallas_call(
        _outproj_kernel,
        grid=(seq // tm,),
        in_specs=[pl.BlockSpec((tm, GW), lambda i: (i, 0))] * 4 + [
            _layer_spec((4 * GW, d), layer),
            pl.BlockSpec((tm, d), lambda i: (i, 0)),
            _mod_spec(MOD_GATE_A),
            _layer_spec((1, d), layer),
        ],
        out_specs=pl.BlockSpec((tm, d), lambda i: (i, 0)),
        out_shape=jax.ShapeDtypeStruct((seq, d), F32),
        compiler_params=_cparams("parallel"),
        name="out_proj",
    )(*ys, w_bf16, x, mod, norm_w)


MLP_TM = 1024
MLP_TH = 512
NEXT_WIN_ROWS = 48


def _mlp_kernel(*refs, next_layer):
    if next_layer:
        (x_ref, nw_ref, sc_ref, sh_ref, wu_ref, wd_ref, gate_ref, pw_ref, c_ref, aw_ref, ab_ref, wt_ref,
         o_ref, mod_ref, wt16_ref, h_ref) = refs
    else:
        x_ref, nw_ref, sc_ref, sh_ref, wu_ref, wd_ref, gate_ref, pw_ref, o_ref, h_ref = refs
    j = pl.program_id(1)

    @pl.when(j == 0)
    def _():
        _prenorm_to(x_ref, nw_ref, sc_ref, sh_ref, h_ref)
        o_ref[...] = jnp.zeros_like(o_ref)

    hid = _mm(h_ref[...], wu_ref[...])
    hid = jnp.square(jnp.maximum(hid, 0.0))
    o_ref[...] += _mm(hid, wd_ref[...])

    if next_layer:
        prod = aw_ref[...] * _silu(c_ref[...])
        part = jnp.sum(prod.reshape(prod.shape[0] // SUBLANES, SUBLANES, LANES), axis=0)
        mod_ref[...] = jnp.sum(part, axis=0, keepdims=True) + ab_ref[...]
        wt16_ref[...] = wt_ref[...].astype(BF16)

    @pl.when(j == pl.num_programs(1) - 1)
    def _():
        def body(r, carry):
            rows = pl.ds(pl.multiple_of(r * NORM_ROWS, NORM_ROWS), NORM_ROWS)
            y = o_ref[rows, :]
            y = y * lax.rsqrt(jnp.mean(y * y, axis=-1, keepdims=True) + EPS) * pw_ref[...]
            o_ref[rows, :] = x_ref[rows, :] + gate_ref[...] * y
            return carry

        lax.fori_loop(0, x_ref.shape[0] // NORM_ROWS, body, 0)


def _mlp(x, layer, norm_w, mod, w_up, w_down, post_w, next_layer=None):
    seq, d = x.shape
    hidden = w_up.shape[2]
    tm = min(MLP_TM, seq)
    steps_per_tile = hidden // MLP_TH
    vmem = (2 * 2 * tm * d * 4 + tm * d * 2 + 2 * 2 * d * MLP_TH * w_up.dtype.itemsize
            + d * MLP_TH * 2 + tm * MLP_TH * 6)
    in_specs = [
        pl.BlockSpec((tm, d), lambda i, j: (i, 0)),
        _layer_spec((1, d), layer),
        _mod_spec(MOD_SCALE_M),
        _mod_spec(MOD_SHIFT_M),
        _layer_spec((d, MLP_TH), layer, 0, lambda i, j: j),
        _layer_spec((MLP_TH, d), layer, lambda i, j: j, 0),
        _mod_spec(MOD_GATE_M),
        _layer_spec((1, d), layer),
    ]
    out_specs = [pl.BlockSpec((tm, d), lambda i, j: (i, 0))]
    out_shape = [jax.ShapeDtypeStruct((seq, d), F32)]
    args = [x, norm_w, mod, mod, w_up, w_down, mod, post_w]
    if next_layer:
        c_col, ada_w, ada_b, w_in_t = next_layer
        n = ada_w.shape[2]
        n_steps = (seq // tm) * steps_per_tile
        wt_rows = NEXT_WIN_ROWS
        assert n_steps >= n // LANES and n_steps * wt_rows >= w_in_t.shape[1] and w_in_t.shape[1] % wt_rows == 0
        col = lambda i, j: jnp.minimum(i * steps_per_tile + j, n // LANES - 1)
        wt_blk = lambda i, j: jnp.minimum(i * steps_per_tile + j, w_in_t.shape[1] // wt_rows - 1)
        in_specs += [pl.BlockSpec((d, 1), lambda i, j: (0, 0)),
                     pl.BlockSpec((None, d, LANES), lambda i, j: (layer + 1, 0, col(i, j))),
                     pl.BlockSpec((None, 1, LANES), lambda i, j: (layer + 1, 0, col(i, j))),
                     pl.BlockSpec((None, wt_rows, d), lambda i, j: (layer + 1, wt_blk(i, j), 0))]
        out_specs += [pl.BlockSpec((None, 1, LANES), lambda i, j: (0, 0, col(i, j))),
                      pl.BlockSpec((wt_rows, d), lambda i, j: (wt_blk(i, j), 0))]
        out_shape += [jax.ShapeDtypeStruct((1, 1, n), F32), jax.ShapeDtypeStruct(w_in_t.shape[1:], BF16)]
        args += [c_col, ada_w, ada_b, w_in_t]
        vmem += 2 * 2 * d * LANES * 4 + 2 * wt_rows * d * 6
    outs = pl.pallas_call(
        functools.partial(_mlp_kernel, next_layer=bool(next_layer)),
        grid=(seq // tm, steps_per_tile),
        in_specs=in_specs,
        out_specs=out_specs,
        out_shape=out_shape,
        scratch_shapes=[pltpu.VMEM((tm, d), BF16)],
        compiler_params=_cparams("arbitrary", "arbitrary", vmem_limit=vmem),
        name="mlp",
    )(*args)
    return outs if next_layer else outs[0]


def _gathered_group(w_t):
    parts = [w_t[IN_OFFSET[name][0]:IN_OFFSET[name][1]] for name in GATHERED_GROUP]
    used = sum(p.shape[0] for p in parts)
    parts.append(jnp.zeros((IN_GROUP - used, w_t.shape[1]), w_t.dtype))
    return jnp.concatenate(parts, axis=0)


def kernel(x, c, positions, ada_w, ada_b, norm_pre_mix, norm_post_mix, norm_pre_mlp, norm_post_mlp,
           w_in, w_out, gdn_conv_w, gdn_a_log, gdn_dt_bias, gdn_norm_w, ssm_conv_w, ssm_conv_b,
           ssm_a_log, ssm_dt_bias, ssm_d, ssm_norm_w, attn_sinks, sc_conv_w, w_up, w_down):
    batch, seq, d = x.shape
    assert batch == 1 and d == D_MODEL
    depth = ada_w.shape[0]
    xs = x.reshape(seq, d)

    adaln = (c.reshape(d, 1), ada_w, ada_b.reshape(depth, 1, -1))
    mod, cos, sin = _modulation_and_rope(*adaln, 0, positions.reshape(seq, 1))

    rows = lambda v: v.reshape(depth, 1, -1)
    w_in_t = jnp.swapaxes(w_in, 1, 2)
    w_in_t16 = w_in_t[0].astype(BF16)
    w_out_bf16 = w_out.astype(BF16)
    gdn_lanes = _lane_params(LANE_ALPHA, gdn_a_log, gdn_dt_bias)
    ssm_lanes = _lane_params(LANE_DT, ssm_a_log, ssm_dt_bias, ssm_d)
    sink_rows = jnp.broadcast_to(attn_sinks.astype(F32)[:, :, None], (depth, ATTN_Q_HEADS, LANES))

    for i in range(depth):
        proj = _in_projection(xs, i, rows(norm_pre_mix), mod, w_in_t16, _gathered_group(w_in_t16))
        y_a = _gated_deltanet(proj, i, gdn_conv_w, gdn_lanes, rows(gdn_norm_w))
        y_b = _mamba2_ssd(proj, i, ssm_conv_w, rows(ssm_conv_b), ssm_lanes, rows(ssm_norm_w))
        y_c, y_d = _attention_and_short_conv(proj, i, cos, sin, sink_rows, sc_conv_w)
        xs = _out_projection((y_a, y_b, y_c, y_d), i, w_out_bf16, xs, mod, rows(norm_post_mix))
        if i + 1 < depth:
            xs, mod, w_in_t16 = _mlp(xs, i, rows(norm_pre_mlp), mod, w_up, w_down, rows(norm_post_mlp),
                                     next_layer=adaln + (w_in_t,))
        else:
            xs = _mlp(xs, i, rows(norm_pre_mlp), mod, w_up, w_down, rows(norm_post_mlp))
    return xs.reshape(batch, seq, d)
```

```python
import functools

import jax
import jax.numpy as jnp
from jax import lax
from jax.experimental import pallas as pl
from jax.experimental.pallas import tpu as pltpu

F32 = jnp.float32
BF16 = jnp.bfloat16

D_MODEL = 2048
GW = 512
GDN_HEADS, GDN_HEAD_DIM, GDN_CHUNK = 4, 128, 64
SSM_HEADS, SSM_HEAD_DIM, SSM_STATE, SSM_CHUNK = 8, 64, 128, 128
ATTN_HEAD_DIM, ATTN_Q_HEADS, WINDOW = 64, 8, 128
ROPE_THETA = 10000.0
MLP_HIDDEN = 4 * D_MODEL
EPS = 1e-6

SUBLANES = 8
LANES = 128

_IN_SEGMENTS = (("gq", GW), ("gk", GW), ("gv", GW), ("gz", GW), ("gb", GDN_HEADS), ("ga", GDN_HEADS),
                ("sz", GW), ("sx", GW), ("sbc", 4 * SSM_STATE), ("sdt", SSM_HEADS),
                ("aq", GW), ("ak", 2 * ATTN_HEAD_DIM), ("av", 2 * ATTN_HEAD_DIM),
                ("cb", GW), ("cc", GW), ("ch", GW))
IN_OFFSET = {}
_off = 0
for _name, _size in _IN_SEGMENTS:
    IN_OFFSET[_name] = (_off, _off + _size)
    _off += _size
IN_WIDTH = _off

IN_GROUP = 3 * GW
CONTIGUOUS_GROUPS = (("gq", "gk", "gv"), ("sz", "sx", "sbc"), ("cb", "cc", "ch"))
GATHERED_GROUP = ("gz", "aq", "ak", "av", "gb", "ga", "sdt")
IN_COLS = (len(CONTIGUOUS_GROUPS) + 1) * IN_GROUP
for _grp in CONTIGUOUS_GROUPS:
    assert IN_OFFSET[_grp[0]][0] % SUBLANES == 0
    assert all(IN_OFFSET[a][1] == IN_OFFSET[b][0] for a, b in zip(_grp, _grp[1:]))
(SLAB_GQ, SLAB_GK, SLAB_GV, SLAB_SZ, SLAB_SX, SLAB_SBC, SLAB_CB, SLAB_CC, SLAB_CH, SLAB_GZ, SLAB_AQ) = range(11)
SLAB128_AK, SLAB128_AV, SLAB128_SMALL = 44, 45, 46
LANE_BETA, LANE_ALPHA, LANE_DT = 0, 4, 8

MIB = 1024 * 1024
V7X_VMEM_BYTES = 64 * MIB
VMEM_LIMIT = 56 * MIB


def _cparams(*sem, vmem_limit=VMEM_LIMIT):
    assert vmem_limit < V7X_VMEM_BYTES
    return pltpu.CompilerParams(dimension_semantics=sem, vmem_limit_bytes=vmem_limit)


def _sigmoid(x):
    return jax.nn.sigmoid(x)


def _silu(x):
    return x * jax.nn.sigmoid(x)


def _mm(a, b):
    return jnp.dot(a.astype(BF16), b.astype(BF16), preferred_element_type=F32)


def _mm_nt(a, b):
    return lax.dot_general(a.astype(BF16), b.astype(BF16), (((1,), (1,)), ((), ())),
                           preferred_element_type=F32)


def _mm_tn(a, b):
    return lax.dot_general(a.astype(BF16), b.astype(BF16), (((0,), (0,)), ((), ())),
                           preferred_element_type=F32)


def _blk(idx, size):
    assert size & (size - 1) == 0
    return lax.shift_right_logical(idx, size.bit_length() - 1)


def _split3(x):
    x1 = x.astype(BF16)
    r1 = x - x1.astype(F32)
    x2 = r1.astype(BF16)
    r2 = r1 - x2.astype(F32)
    return x1, x2, r2.astype(BF16)


def _cumsum_matrix(rows, chunk):
    r = lax.broadcasted_iota(jnp.int32, (2 * rows, rows), 0)
    s = lax.broadcasted_iota(jnp.int32, (2 * rows, rows), 1)
    rr = jnp.where(r >= rows, r - rows, r)
    same = _blk(rr, chunk) == _blk(s, chunk)
    sel = same & ((r >= rows) | (s <= rr))
    return jnp.where(sel, 1.0, 0.0).astype(BF16)


def _chunk_cumsum(g, mat):
    rows = g.shape[0]
    acc = None
    for part in _split3(g):
        t = jnp.dot(mat, part, preferred_element_type=F32)
        acc = t if acc is None else acc + t
    return acc[:rows], acc[rows:]


def _causal_conv(x, prev, w, taps):
    rows = x.shape[0]
    xp = jnp.concatenate([prev, x], axis=0)
    acc = x * w[taps - 1:taps]
    for d in range(1, taps):
        acc = acc + xp[SUBLANES - d:SUBLANES - d + rows] * w[taps - 1 - d:taps - d]
    return acc


def _causal_conv_staged(x_ref, prev, w, taps, stage_ref):
    rows = x_ref.shape[0]
    x = x_ref[...]
    stage_ref[0:SUBLANES, :] = prev
    stage_ref[SUBLANES:, :] = x
    acc = x * w[taps - 1:taps]
    for d in range(1, taps):
        acc = acc + stage_ref[pl.ds(SUBLANES - d, rows), :] * w[taps - 1 - d:taps - d]
    return acc


MOD_TN = 1536
MOD_ROWS = 64


def _mod_rope_kernel(c_ref, w_ref, b_ref, pos_ref, invf_ref, sign_ref, o_ref, cos_ref, sin_ref):
    d = c_ref.shape[0]

    def body(r, acc):
        rows = pl.ds(pl.multiple_of(r * MOD_ROWS, MOD_ROWS), MOD_ROWS)
        cc = _silu(c_ref[rows, :])
        prod = w_ref[0, rows, :] * cc
        return acc + jnp.sum(prod.reshape(MOD_ROWS // SUBLANES, SUBLANES, MOD_TN), axis=0)

    acc = lax.fori_loop(0, d // MOD_ROWS, body, jnp.zeros((SUBLANES, MOD_TN), F32))
    o_ref[0] = jnp.sum(acc, axis=0, keepdims=True) + b_ref[0]

    ang = pos_ref[...].astype(F32) * invf_ref[...]
    cos_ref[...] = jnp.cos(ang)
    sin_ref[...] = jnp.sin(ang) * sign_ref[...]


def _modulation_and_rope(c_col, ada_w, ada_b, layer, pos_col):
    depth, d, n = ada_w.shape
    seq = pos_col.shape[0]
    steps = n // MOD_TN
    tb = seq // steps
    assert steps * tb == seq and tb % SUBLANES == 0
    half = ATTN_HEAD_DIM // 2
    inv_freq = ROPE_THETA ** (-jnp.arange(0, ATTN_HEAD_DIM, 2, dtype=F32) / ATTN_HEAD_DIM)
    invf = jnp.tile(inv_freq, LANES // half).reshape(1, LANES)
    lane = jnp.arange(LANES)
    sign = jnp.where((lane % ATTN_HEAD_DIM) < half, -1.0, 1.0).astype(F32).reshape(1, LANES)
    table = jax.ShapeDtypeStruct((seq, LANES), F32)
    return pl.pallas_call(
        _mod_rope_kernel,
        grid=(steps,),
        in_specs=[
            pl.BlockSpec((d, 1), lambda j: (0, 0)),
            pl.BlockSpec((1, d, MOD_TN), lambda j: (layer, 0, j)),
            pl.BlockSpec((1, 1, MOD_TN), lambda j: (layer, 0, j)),
            pl.BlockSpec((tb, 1), lambda j: (j, 0)),
            pl.BlockSpec((1, LANES), lambda j: (0, 0)),
            pl.BlockSpec((1, LANES), lambda j: (0, 0)),
        ],
        out_specs=[pl.BlockSpec((1, 1, MOD_TN), lambda j: (0, 0, j)),
                   pl.BlockSpec((tb, LANES), lambda j: (j, 0)),
                   pl.BlockSpec((tb, LANES), lambda j: (j, 0))],
        out_shape=[jax.ShapeDtypeStruct((1, 1, n), F32), table, table],
        compiler_params=_cparams("parallel"),
        name="adaln_mod_rope",
    )(c_col, ada_w, ada_b, pos_col, invf, sign)


def _mod_spec(k):
    return pl.BlockSpec((None, 1, D_MODEL), lambda *grid_idx: (0, 0, k))


NORM_ROWS = 128


def _prenorm_to(x_ref, nw_ref, sc_ref, sh_ref, h_ref):
    tm = x_ref.shape[0]
    gain = nw_ref[...] * (1.0 + sc_ref[...])
    shift = sh_ref[...]

    def body(r, carry):
        rows = pl.ds(pl.multiple_of(r * NORM_ROWS, NORM_ROWS), NORM_ROWS)
        x = x_ref[rows, :]
        inv = lax.rsqrt(jnp.mean(x * x, axis=-1, keepdims=True) + EPS)
        h_ref[rows, :] = (x * inv * gain + shift).astype(BF16)
        return carry

    lax.fori_loop(0, tm // NORM_ROWS, body, 0)


INPROJ_TM = 1024

MOD_SHIFT_A, MOD_SCALE_A, MOD_GATE_A, MOD_SHIFT_M, MOD_SCALE_M, MOD_GATE_M = range(6)


def _layer_spec(shape, layer, *rest):
    rest = rest or (0,) * len(shape)

    def index_map(*grid_idx):
        return (layer,) + tuple(r(*grid_idx) if callable(r) else r for r in rest)

    return pl.BlockSpec((None,) + tuple(shape), index_map)


def _inproj_kernel(x_ref, nw_ref, sc_ref, sh_ref, wt_ref, tail_ref, o_ref, h_ref):
    j = pl.program_id(1)

    @pl.when(j == 0)
    def _():
        _prenorm_to(x_ref, nw_ref, sc_ref, sh_ref, h_ref)

    def project(w_ref):
        o_ref[...] = _mm_nt(h_ref[...], w_ref[...])

    @pl.when(j < len(CONTIGUOUS_GROUPS))
    def _():
        project(wt_ref)

    @pl.when(j == len(CONTIGUOUS_GROUPS))
    def _():
        project(tail_ref)


def _group_row(j):
    starts = [IN_OFFSET[grp[0]][0] // SUBLANES for grp in CONTIGUOUS_GROUPS]
    row8 = starts[-1]
    for k in range(len(starts) - 2, -1, -1):
        row8 = jnp.where(j <= k, starts[k], row8)
    return row8 * SUBLANES


def _in_projection(x, layer, norm_w, mod, w_t, w_gathered):
    seq, d = x.shape
    tm = min(INPROJ_TM, seq)
    return pl.pallas_call(
        _inproj_kernel,
        grid=(seq // tm, IN_COLS // IN_GROUP),
        in_specs=[
            pl.BlockSpec((tm, d), lambda i, j: (i, 0)),
            _layer_spec((1, d), layer),
            _mod_spec(MOD_SCALE_A),
            _mod_spec(MOD_SHIFT_A),
            pl.BlockSpec((pl.Element(IN_GROUP), pl.Element(d)), lambda i, j: (_group_row(j), 0)),
            pl.BlockSpec((IN_GROUP, d), lambda i, j: (0, 0), pipeline_mode=pl.Buffered(1)),
        ],
        out_specs=pl.BlockSpec((tm, IN_GROUP), lambda i, j: (i, j)),
        out_shape=jax.ShapeDtypeStruct((seq, IN_COLS), F32),
        scratch_shapes=[pltpu.VMEM((tm, d), BF16)],
        compiler_params=_cparams("parallel", "arbitrary"),
        name="in_proj",
    )(x, norm_w, mod, mod, w_t, w_gathered)


GDN_TB = 512


def _l2norm(x):
    return x * lax.rsqrt(jnp.sum(x * x, axis=-1, keepdims=True) + EPS)


GDN_BLK = 128
GDN_PREP_STAGES_PER_CHUNK = 4


def _unit_lower_inverse_minus_identity(ms, r, s):
    same8 = _blk(r, 8) == _blk(s, 8)
    same16 = _blk(r, 16) == _blk(s, 16)
    same32 = _blk(r, 32) == _blk(s, 32)
    n1 = [jnp.where(same8, -m, 0.0) for m in ms]
    n2 = [_mm(a, a) for a in n1]
    yield
    n3 = [_mm(a, b) for a, b in zip(n1, n2)]
    n4 = [_mm(b, b) for b in n2]
    yield
    ys = [a + b + c for a, b, c in zip(n1, n2, n3)]
    ts = [_mm(y, d) for y, d in zip(ys, n4)]
    yield
    ys = [y + d + t for y, d, t in zip(ys, n4, ts)]
    for sel in (same16 & ~same8, same32 & ~same16, ~same32):
        blks = [jnp.where(sel, m, 0.0) for m in ms]
        cs = [b + _mm(y, b) for y, b in zip(ys, blks)]
        yield
        ts = [_mm(c, y) for c, y in zip(cs, ys)]
        yield
        ys = [y - c - t for y, c, t in zip(ys, cs, ts)]
    return ys


def _run_interleaved(main, side, main_stages_per_side_stage):
    side_live = True
    done = 0
    while True:
        try:
            next(main)
        except StopIteration as stop:
            result = stop.value
            break
        done += 1
        if side_live and done % main_stages_per_side_stage == 0:
            side_live = next(side, StopIteration) is not StopIteration
    while side_live:
        side_live = next(side, StopIteration) is not StopIteration
    return result


def _gdn_prepare(q_ref, k_ref, v_ref, qp_ref, kp_ref, vp_ref, sm_ref, cw_ref, lp_ref, cum_ref, conv_ref, first_block):
    tb = q_ref.shape[0]
    hd = GDN_HEAD_DIM
    nblk = tb // GDN_BLK

    keep_prev = jnp.where(first_block, 0.0, 1.0)
    cw = cw_ref[...]

    def conv_silu(x_ref, p_ref, idx):
        w = cw[:, idx * GW:(idx + 1) * GW]
        return _silu(_causal_conv_staged(x_ref, p_ref[...] * keep_prev, w, 4, conv_ref.at[idx]))

    q = conv_silu(q_ref, qp_ref, 0)
    yield
    k = conv_silu(k_ref, kp_ref, 1)
    yield
    v = conv_silu(v_ref, vp_ref, 2)
    yield

    small = sm_ref[...]
    a_log = lp_ref[0:1, :]
    dt_bias = lp_ref[1:2, :]
    beta_all = _sigmoid(small)
    g_all = -jnp.exp(a_log) * jax.nn.softplus(small + dt_bias)
    gc_all, gtot_all = _chunk_cumsum(g_all, cum_ref[...])
    gc_t = gc_all.T
    yield

    r = lax.broadcasted_iota(jnp.int32, (GDN_BLK, GDN_BLK), 0)
    s = lax.broadcasted_iota(jnp.int32, (GDN_BLK, GDN_BLK), 1)
    same_chunk = _blk(r, GDN_CHUNK) == _blk(s, GDN_CHUNK)
    mask_incl = same_chunk & (s <= r)
    mask_strict = same_chunk & (s < r)

    problems = [(slice(b * GDN_BLK, (b + 1) * GDN_BLK), h) for b in range(nblk) for h in range(GDN_HEADS)]
    head_cols = lambda h: slice(h * hd, (h + 1) * hd)
    gate = lambda arr, rows, lane: arr[rows, lane:lane + 1]

    qn = [_l2norm(q[rows, head_cols(h)]) * (hd ** -0.5) for rows, h in problems]
    kn = [_l2norm(k[rows, head_cols(h)]) for rows, h in problems]
    beta = [gate(beta_all, rows, LANE_BETA + h) for rows, h in problems]
    gcol = [gate(gc_all, rows, LANE_ALPHA + h) for rows, h in problems]
    gtot = [gate(gtot_all, rows, LANE_ALPHA + h) for rows, h in problems]
    grow = [gc_t[LANE_ALPHA + h:LANE_ALPHA + h + 1, rows] for rows, h in problems]
    yield

    decay = [jnp.exp(jnp.where(mask_incl, gc - gr, -jnp.inf)) for gc, gr in zip(gcol, grow)]
    k_beta = [kk * b for kk, b in zip(kn, beta)]
    k16 = [kk.astype(BF16) for kk in kn]
    yield
    ms = [jnp.where(mask_strict, _mm_nt(kb, kk) * d, 0.0) for kb, kk, d in zip(k_beta, k16, decay)]
    yield
    attn = [(_mm_nt(qq, kk) * d).astype(BF16) for qq, kk, d in zip(qn, k16, decay)]
    ys = yield from _unit_lower_inverse_minus_identity(ms, r, s)
    yield

    e_gc = [jnp.exp(gc) for gc in gcol]
    rhs = [jnp.concatenate([v[rows, head_cols(h)] * b, kb * e], axis=1)
           for (rows, h), b, kb, e in zip(problems, beta, k_beta, e_gc)]
    uw = [x + _mm(y, x) for y, x in zip(ys, rhs)]
    yield
    u = [x[:, :hd] for x in uw]
    w = [x[:, hd:].astype(BF16) for x in uw]
    q_dec = [(qq * e).astype(BF16) for qq, e in zip(qn, e_gc)]
    k_dec = [(kk * jnp.exp(gt - gc)).astype(BF16) for kk, gt, gc in zip(kn, gtot, gcol)]
    return u, w, q_dec, k_dec, attn, gtot_all


def _gdn_recurrence(u, w, q_dec, k_dec, attn, gtot_all, state_ref, z, nw_ref, o_ref):
    hd = GDN_HEAD_DIM
    nblk = o_ref.shape[0] // GDN_BLK
    chunks_per_blk = GDN_BLK // GDN_CHUNK
    head_cols = lambda h: slice(h * hd, (h + 1) * hd)

    states = [state_ref[h] for h in range(GDN_HEADS)]
    v_new = [[] for _ in range(nblk * GDN_HEADS)]
    o_inter = [[] for _ in range(nblk * GDN_HEADS)]
    for b in range(nblk):
        for c in range(chunks_per_blk):
            rows = slice(c * GDN_CHUNK, (c + 1) * GDN_CHUNK)
            first = b * GDN_BLK + c * GDN_CHUNK
            for h in range(GDN_HEADS):
                p = b * GDN_HEADS + h
                st16 = states[h].astype(BF16)
                vn = u(p, rows) - jnp.dot(w(p, rows), st16, preferred_element_type=F32)
                o_inter[p].append(jnp.dot(q_dec(p, rows), st16, preferred_element_type=F32))
                chunk_dec = jnp.exp(gtot_all[first:first + 1, LANE_ALPHA + h:LANE_ALPHA + h + 1])
                states[h] = states[h] * chunk_dec + _mm_tn(k_dec(p, rows), vn)
                v_new[p].append(vn)
            yield
        rows = slice(b * GDN_BLK, (b + 1) * GDN_BLK)
        for h in range(GDN_HEADS):
            p = b * GDN_HEADS + h
            o = jnp.concatenate(o_inter[p], axis=0) + _mm(attn(p, slice(None)), jnp.concatenate(v_new[p], axis=0))
            o = o * lax.rsqrt(jnp.mean(o * o, axis=-1, keepdims=True) + EPS) * nw_ref[...]
            o_ref[rows, head_cols(h)] = (o * _silu(z[rows, head_cols(h)])).astype(o_ref.dtype)
    for h in range(GDN_HEADS):
        state_ref[h] = states[h]


def _gdn_kernel(q_ref, k_ref, v_ref, z_ref, qp_ref, kp_ref, vp_ref, sm_ref, cw_ref, lp_ref, nw_ref, cum_ref,
                o_ref, state_ref, conv_ref, u_ref, w_ref, qd_ref, kd_ref, at_ref, gt_ref):
    step = pl.program_id(0)
    write_slot = step % 2
    read_slot = 1 - write_slot
    n_problems = u_ref.shape[1]

    @pl.when(step == 0)
    def _():
        state_ref[...] = jnp.zeros_like(state_ref)
        for ref in (u_ref, w_ref, qd_ref, kd_ref, at_ref, gt_ref):
            ref[1] = jnp.zeros(ref.shape[1:], ref.dtype)

    recurrence = _gdn_recurrence(*[(lambda p, rows, ref=ref: ref[read_slot, p, rows])
                                   for ref in (u_ref, w_ref, qd_ref, kd_ref, at_ref)],
                                 gt_ref[read_slot], state_ref, z_ref[...], nw_ref, o_ref)
    prepare = _gdn_prepare(q_ref, k_ref, v_ref, qp_ref, kp_ref, vp_ref, sm_ref, cw_ref, lp_ref, cum_ref, conv_ref,
                           step == 0)
    u, w, q_dec, k_dec, attn, gtot_all = _run_interleaved(prepare, recurrence, GDN_PREP_STAGES_PER_CHUNK)
    for p in range(n_problems):
        u_ref[write_slot, p] = u[p]
        w_ref[write_slot, p] = w[p]
        qd_ref[write_slot, p] = q_dec[p]
        kd_ref[write_slot, p] = k_dec[p]
        at_ref[write_slot, p] = attn[p]
    gt_ref[write_slot] = gtot_all


def _prev_rows_spec(tb, width, slab):
    blocks = tb // SUBLANES
    return pl.BlockSpec((SUBLANES, width), lambda i: (jnp.maximum(i * blocks - 1, 0), slab))


def _lane_params(lane0, *rows):
    stacked = jnp.stack([r.astype(F32) for r in rows], axis=1)
    nrows, heads = stacked.shape[1:]
    return jnp.pad(stacked, ((0, 0), (0, SUBLANES - nrows), (lane0, LANES - lane0 - heads)))


def _gated_deltanet(proj, layer, conv_w, lane_params, norm_w):
    seq = proj.shape[0]
    tb = min(GDN_TB, seq)
    nsteps = seq // tb
    n_problems = (tb // GDN_BLK) * GDN_HEADS
    prep = lambda t: jnp.minimum(t, nsteps - 1)
    done = lambda t: jnp.maximum(t - 1, 0)
    slab = lambda j: pl.BlockSpec((tb, GW), lambda t: (prep(t), j))
    prev = lambda j: pl.BlockSpec((SUBLANES, GW),
                                  lambda t: (jnp.maximum(prep(t) * (tb // SUBLANES) - 1, 0), j))
    full = lambda shape: _layer_spec(shape, layer)
    factor = lambda dtype: pltpu.VMEM((2, n_problems, GDN_BLK, GDN_HEAD_DIM), dtype)
    return pl.pallas_call(
        _gdn_kernel,
        grid=(nsteps + 1,),
        in_specs=[
            slab(SLAB_GQ), slab(SLAB_GK), slab(SLAB_GV),
            pl.BlockSpec((tb, GW), lambda t: (done(t), SLAB_GZ)),
            prev(SLAB_GQ), prev(SLAB_GK), prev(SLAB_GV),
            pl.BlockSpec((tb, LANES), lambda t: (prep(t), SLAB128_SMALL)),
            full((4, 3 * GW)), full((SUBLANES, LANES)), full((1, GDN_HEAD_DIM)),
            pl.BlockSpec((2 * tb, tb), lambda t: (0, 0)),
        ],
        out_specs=pl.BlockSpec((tb, GW), lambda t: (done(t), 0)),
        out_shape=jax.ShapeDtypeStruct((seq, GW), BF16),
        scratch_shapes=[pltpu.VMEM((GDN_HEADS, GDN_HEAD_DIM, GDN_HEAD_DIM), F32),
                        pltpu.VMEM((3, tb + SUBLANES, GW), F32),
                        factor(F32), factor(BF16), factor(BF16), factor(BF16), factor(BF16),
                        pltpu.VMEM((2, tb, LANES), F32)],
        compiler_params=_cparams("arbitrary"),
        name="gated_deltanet",
    )(proj, proj, proj, proj, proj, proj, proj, proj, conv_w, lane_params, norm_w, _cumsum_matrix(tb, GDN_CHUNK))


SSD_TB = 256


def _ssd_kernel(z_ref, x_ref, bc_ref, xp_ref, bcp_ref, sm_ref, cw_ref, cb_ref, lp_ref, nw_ref, cum_ref,
                o_ref, state_ref):
    step = pl.program_id(0)
    tb = z_ref.shape[0]
    nstate = SSM_STATE
    chunk = SSM_CHUNK
    nchunks = tb // chunk
    n_pairs = SSM_HEADS // 2
    pairs_per_group = n_pairs // 2

    @pl.when(step == 0)
    def _():
        state_ref[...] = jnp.zeros_like(state_ref)

    keep_prev = jnp.where(step == 0, 0.0, 1.0)
    cw = cw_ref[...]
    cbias = cb_ref[...]
    xs = _silu(_causal_conv(x_ref[...], xp_ref[...] * keep_prev, cw[:, :GW], 4) + cbias[:, :GW])
    bc = _silu(_causal_conv(bc_ref[...], bcp_ref[...] * keep_prev, cw[:, GW:], 4) + cbias[:, GW:])
    z = z_ref[...]

    small = sm_ref[...]
    a_neg = -jnp.exp(lp_ref[0:1, :])
    dt_all = jax.nn.softplus(small + lp_ref[1:2, :])
    dskip_all = lp_ref[2:3, :]
    gc_all, gtot_all = _chunk_cumsum(dt_all * a_neg, cum_ref[...])
    gc_t = gc_all.T

    r = lax.broadcasted_iota(jnp.int32, (chunk, chunk), 0)
    s = lax.broadcasted_iota(jnp.int32, (chunk, chunk), 1)
    mask_incl = s <= r
    lane = lax.broadcasted_iota(jnp.int32, (1, LANES), 1)
    first_half = lane < SSM_HEAD_DIM
    by_half = lambda a, b: jnp.where(first_half, a, b)

    problems = [(c, p) for c in range(nchunks) for p in range(n_pairs)]
    rows_of = lambda c: slice(c * chunk, (c + 1) * chunk)
    cols_of = lambda p: slice(p * LANES, (p + 1) * LANES)
    lanes_of = lambda p: (LANE_DT + 2 * p, LANE_DT + 2 * p + 1)
    col = lambda arr, c, l: arr[rows_of(c), l:l + 1]

    b_mat = {(c, g): bc[rows_of(c), g * nstate:(g + 1) * nstate].astype(BF16)
             for c in range(nchunks) for g in range(2)}
    c_mat = {(c, g): bc[rows_of(c), (2 + g) * nstate:(3 + g) * nstate].astype(BF16)
             for c in range(nchunks) for g in range(2)}
    cb = {key: _mm_nt(c_mat[key], b_mat[key]) for key in b_mat}

    x_pair = [xs[rows_of(c), cols_of(p)] for c, p in problems]
    xdt = [x * by_half(col(dt_all, c, lanes_of(p)[0]), col(dt_all, c, lanes_of(p)[1]))
           for x, (c, p) in zip(x_pair, problems)]
    y_diag = []
    for (c, p), xd in zip(problems, xdt):
        halves = (by_half(xd, 0.0), by_half(0.0, xd))
        acc = None
        for l, half in zip(lanes_of(p), halves):
            grow = gc_t[l:l + 1, rows_of(c)]
            lmat = jnp.exp(jnp.where(mask_incl, col(gc_all, c, l) - grow, -jnp.inf))
            t = _mm(cb[c, p // pairs_per_group] * lmat, half)
            acc = t if acc is None else acc + t
        y_diag.append(acc)

    e_in = [by_half(jnp.exp(col(gc_all, c, lanes_of(p)[0])), jnp.exp(col(gc_all, c, lanes_of(p)[1])))
            for c, p in problems]
    e_out = [by_half(jnp.exp(col(gtot_all, c, lanes_of(p)[0]) - col(gc_all, c, lanes_of(p)[0])),
                     jnp.exp(col(gtot_all, c, lanes_of(p)[1]) - col(gc_all, c, lanes_of(p)[1])))
             for c, p in problems]
    states = [state_ref[p] for p in range(n_pairs)]
    y_off = []
    for idx, (c, p) in enumerate(problems):
        la, lb = lanes_of(p)
        g = p // pairs_per_group
        y_off.append(e_in[idx] * _mm(c_mat[c, g], states[p]))
        first = c * chunk
        cd_lane = by_half(jnp.exp(gtot_all[first:first + 1, la:la + 1]),
                          jnp.exp(gtot_all[first:first + 1, lb:lb + 1]))
        states[p] = states[p] * cd_lane + _mm_tn(b_mat[c, g], xdt[idx] * e_out[idx])
    for p in range(n_pairs):
        state_ref[p] = states[p]

    ys = []
    for idx, (c, p) in enumerate(problems):
        la, lb = lanes_of(p)
        y = y_diag[idx] + y_off[idx] + x_pair[idx] * by_half(dskip_all[:, la:la + 1], dskip_all[:, lb:lb + 1])
        ys.append(y * _silu(z[rows_of(c), cols_of(p)]))
    sumsq = [jnp.sum(y * y, axis=-1, keepdims=True) for y in ys]
    for c in range(nchunks):
        for g in range(2):
            members = [c * n_pairs + g * pairs_per_group + k for k in range(pairs_per_group)]
            total = sumsq[members[0]]
            for idx in members[1:]:
                total = total + sumsq[idx]
            inv = lax.rsqrt(total / (pairs_per_group * LANES) + EPS)
            for idx in members:
                cols = cols_of(problems[idx][1])
                o_ref[rows_of(c), cols] = (ys[idx] * inv * nw_ref[:, cols]).astype(o_ref.dtype)


def _mamba2_ssd(proj, layer, conv_w, conv_b, lane_params, norm_w):
    seq = proj.shape[0]
    tb = min(SSD_TB, seq)
    slab = lambda j: pl.BlockSpec((tb, GW), lambda i: (i, j))
    full = lambda shape: _layer_spec(shape, layer)
    return pl.pallas_call(
        _ssd_kernel,
        grid=(seq // tb,),
        in_specs=[
            slab(SLAB_SZ), slab(SLAB_SX), slab(SLAB_SBC),
            _prev_rows_spec(tb, GW, SLAB_SX), _prev_rows_spec(tb, GW, SLAB_SBC),
            pl.BlockSpec((tb, LANES), lambda i: (i, SLAB128_SMALL)),
            full((4, 2 * GW)), full((1, 2 * GW)), full((SUBLANES, LANES)), full((1, GW)),
            pl.BlockSpec((2 * tb, tb), lambda i: (0, 0)),
        ],
        out_specs=pl.BlockSpec((tb, GW), lambda i: (i, 0)),
        out_shape=jax.ShapeDtypeStruct((seq, GW), BF16),
        scratch_shapes=[pltpu.VMEM((SSM_HEADS // 2, SSM_STATE, LANES), F32)],
        compiler_params=_cparams("arbitrary"),
        name="mamba2_ssd",
    )(proj, proj, proj, proj, proj, proj, conv_w, conv_b, lane_params, norm_w, _cumsum_matrix(tb, SSM_CHUNK))


def _rotate_half(x, lane_in_head_low):
    width = x.shape[1]
    half = ATTN_HEAD_DIM // 2
    ahead = pltpu.roll(x, width - half, axis=1)
    behind = pltpu.roll(x, half, axis=1)
    return jnp.where(lane_in_head_low, ahead, behind)


SWA_TB = 512


def _swa_kernel(q_ref, k_ref, v_ref, cos_ref, sin_ref, sink_ref, cb_ref, cc_ref, ch_ref, ccp_ref, chp_ref, cw_ref,
                o_ref, od_ref, kprev_ref, vprev_ref):
    step = pl.program_id(0)
    tb = q_ref.shape[0]
    win = WINDOW
    hd = ATTN_HEAD_DIM

    @pl.when(step == 0)
    def _():
        kprev_ref[...] = jnp.zeros_like(kprev_ref)
        vprev_ref[...] = jnp.zeros_like(vprev_ref)

    keep_prev = jnp.where(step == 0, 0.0, 1.0)
    c_h = cc_ref[...] * ch_ref[...]
    c_h_prev = ccp_ref[...] * chp_ref[...] * keep_prev
    od_ref[...] = (cb_ref[...] * _causal_conv(c_h, c_h_prev, cw_ref[...], 3)).astype(od_ref.dtype)

    cos = cos_ref[...]
    sin = sin_ref[...]
    lane = lax.broadcasted_iota(jnp.int32, (1, LANES), 1)
    low = (lane & (hd - 1)) < (hd // 2)
    first_half = lane < hd

    k_cur = k_ref[...]
    k_cur = k_cur * cos + _rotate_half(k_cur, low) * sin
    v_cur = v_ref[...]
    k_all = jnp.concatenate([kprev_ref[...], k_cur], axis=0)
    v_all = jnp.concatenate([vprev_ref[...], v_cur], axis=0)
    k_swap = pltpu.roll(k_all, hd, axis=1)
    v_swap = pltpu.roll(v_all, hd, axis=1)

    def placed(x, x_swap, kv_head, pos):
        src = x if kv_head == pos else x_swap
        keep = first_half if pos == 0 else ~first_half
        return jnp.where(keep, src, 0.0).astype(BF16)

    qi = lax.broadcasted_iota(jnp.int32, (win, 2 * win), 0)
    kj = lax.broadcasted_iota(jnp.int32, (win, 2 * win), 1)
    rel = qi + win - kj
    band = (rel >= 0) & (rel < win)
    band_first = band & ((kj >= win) | (step > 0))

    problems = [(j, h) for j in range(tb // win) for h in range(ATTN_Q_HEADS)]
    kv_of = lambda head: head // (ATTN_Q_HEADS // 2)
    pair_cols = lambda pair: slice(pair * LANES, (pair + 1) * LANES)
    k_at = {(g, pos): placed(k_all, k_swap, g, pos) for g in range(2) for pos in range(2)}
    v_at = {(g, pos): placed(v_all, v_swap, g, pos) for g in range(2) for pos in range(2)}
    q_pairs = []
    for pair in range(ATTN_Q_HEADS // 2):
        q_pair = q_ref[:, pair_cols(pair)]
        q_pairs.append((q_pair * cos + _rotate_half(q_pair, low) * sin).astype(BF16))
    sinks = [sink_ref[h:h + 1, 0:1] for _, h in problems]
    sc = [jnp.where(band_first if j == 0 else band,
                    _mm_nt(q_pairs[h // 2][j * win:(j + 1) * win],
                           k_at[kv_of(h), h % 2][j * win:(j + 2) * win]) * (hd ** -0.5),
                    -jnp.inf)
          for j, h in problems]
    mx = [jnp.maximum(jnp.max(s_, axis=-1, keepdims=True), sk) for s_, sk in zip(sc, sinks)]
    p = [jnp.exp(s_ - m_) for s_, m_ in zip(sc, mx)]
    denom = [jnp.sum(p_, axis=-1, keepdims=True) + jnp.exp(sk - m_) for p_, sk, m_ in zip(p, sinks, mx)]
    o = [_mm(p_, v_at[kv_of(h), h % 2][j * win:(j + 2) * win]) / d_
         for (j, h), p_, d_ in zip(problems, p, denom)]
    for idx in range(0, len(problems), 2):
        j, h = problems[idx]
        o_ref[j * win:(j + 1) * win, pair_cols(h // 2)] = (o[idx] + o[idx + 1]).astype(o_ref.dtype)

    kprev_ref[...] = k_cur[tb - win:]
    vprev_ref[...] = v_cur[tb - win:]


def _attention_and_short_conv(proj, layer, cos, sin, sink_rows, sc_conv_w):
    seq = proj.shape[0]
    tb = min(SWA_TB, seq)
    slab = lambda j: pl.BlockSpec((tb, GW), lambda i: (i, j))
    out = jax.ShapeDtypeStruct((seq, GW), BF16)
    return pl.pallas_call(
        _swa_kernel,
        grid=(seq // tb,),
        in_specs=[
            slab(SLAB_AQ),
            pl.BlockSpec((tb, LANES), lambda i: (i, SLAB128_AK)),
            pl.BlockSpec((tb, LANES), lambda i: (i, SLAB128_AV)),
            pl.BlockSpec((tb, LANES), lambda i: (i, 0)),
            pl.BlockSpec((tb, LANES), lambda i: (i, 0)),
            _layer_spec((ATTN_Q_HEADS, LANES), layer),
            slab(SLAB_CB), slab(SLAB_CC), slab(SLAB_CH),
            _prev_rows_spec(tb, GW, SLAB_CC), _prev_rows_spec(tb, GW, SLAB_CH),
            _layer_spec((3, GW), layer),
        ],
        out_specs=[pl.BlockSpec((tb, GW), lambda i: (i, 0))] * 2,
        out_shape=[out, out],
        scratch_shapes=[pltpu.VMEM((WINDOW, LANES), F32), pltpu.VMEM((WINDOW, LANES), F32)],
        compiler_params=_cparams("arbitrary"),
        name="sliding_window_attn",
    )(proj, proj, proj, cos, sin, sink_rows, proj, proj, proj, proj, proj, sc_conv_w)


OUTPROJ_TM = 512


def _outproj_kernel(ya_ref, yb_ref, yc_ref, yd_ref, w_ref, x_ref, gate_ref, nw_ref, o_ref):
    y = jnp.concatenate([ya_ref[...], yb_ref[...], yc_ref[...], yd_ref[...]], axis=1)
    y = jnp.dot(y, w_ref[...], preferred_element_type=F32)
    inv = lax.rsqrt(jnp.mean(y * y, axis=-1, keepdims=True) + EPS)
    o_ref[...] = x_ref[...] + y * inv * (gate_ref[...] * nw_ref[...])


def _out_projection(ys, layer, w_bf16, x, mod, norm_w):
    seq, d = x.shape
    tm = min(OUTPROJ_TM, seq)
    return pl.pallas_call(
        _outproj_kernel,
        grid=(seq // tm,),
        in_specs=[pl.BlockSpec((tm, GW), lambda i: (i, 0))] * 4 + [
            _layer_spec((4 * GW, d), layer),
            pl.BlockSpec((tm, d), lambda i: (i, 0)),
            _mod_spec(MOD_GATE_A),
            _layer_spec((1, d), layer),
        ],
        out_specs=pl.BlockSpec((tm, d), lambda i: (i, 0)),
        out_shape=jax.ShapeDtypeStruct((seq, d), F32),
        compiler_params=_cparams("parallel"),
        name="out_proj",
    )(*ys, w_bf16, x, mod, norm_w)


MLP_TM = 1024
MLP_TH = 512
NEXT_WIN_ROWS = 48


def _mlp_kernel(*refs, next_layer):
    if next_layer:
        (x_ref, nw_ref, sc_ref, sh_ref, wu_ref, wd_ref, gate_ref, pw_ref, c_ref, aw_ref, ab_ref, wt_ref,
         o_ref, mod_ref, wt16_ref, h_ref) = refs
    else:
        x_ref, nw_ref, sc_ref, sh_ref, wu_ref, wd_ref, gate_ref, pw_ref, o_ref, h_ref = refs
    j = pl.program_id(1)

    @pl.when(j == 0)
    def _():
        _prenorm_to(x_ref, nw_ref, sc_ref, sh_ref, h_ref)
        o_ref[...] = jnp.zeros_like(o_ref)

    hid = _mm(h_ref[...], wu_ref[...])
    hid = jnp.square(jnp.maximum(hid, 0.0))
    o_ref[...] += _mm(hid, wd_ref[...])

    if next_layer:
        prod = aw_ref[...] * _silu(c_ref[...])
        part = jnp.sum(prod.reshape(prod.shape[0] // SUBLANES, SUBLANES, LANES), axis=0)
        mod_ref[...] = jnp.sum(part, axis=0, keepdims=True) + ab_ref[...]
        wt16_ref[...] = wt_ref[...].astype(BF16)

    @pl.when(j == pl.num_programs(1) - 1)
    def _():
        gain = gate_ref[...] * pw_ref[...]

        def body(r, carry):
            rows = pl.ds(pl.multiple_of(r * NORM_ROWS, NORM_ROWS), NORM_ROWS)
            y = o_ref[rows, :]
            inv = lax.rsqrt(jnp.mean(y * y, axis=-1, keepdims=True) + EPS)
            o_ref[rows, :] = x_ref[rows, :] + y * inv * gain
            return carry

        lax.fori_loop(0, x_ref.shape[0] // NORM_ROWS, body, 0)


def _mlp(x, layer, norm_w, mod, w_up, w_down, post_w, next_layer=None):
    seq, d = x.shape
    hidden = w_up.shape[2]
    tm = min(MLP_TM, seq)
    steps_per_tile = hidden // MLP_TH
    vmem = (2 * 2 * tm * d * 4 + tm * d * 2 + 2 * 2 * d * MLP_TH * w_up.dtype.itemsize
            + d * MLP_TH * 2 + tm * MLP_TH * 6)
    in_specs = [
        pl.BlockSpec((tm, d), lambda i, j: (i, 0)),
        _layer_spec((1, d), layer),
        _mod_spec(MOD_SCALE_M),
        _mod_spec(MOD_SHIFT_M),
        _layer_spec((d, MLP_TH), layer, 0, lambda i, j: j),
        _layer_spec((MLP_TH, d), layer, lambda i, j: j, 0),
        _mod_spec(MOD_GATE_M),
        _layer_spec((1, d), layer),
    ]
    out_specs = [pl.BlockSpec((tm, d), lambda i, j: (i, 0))]
    out_shape = [jax.ShapeDtypeStruct((seq, d), F32)]
    args = [x, norm_w, mod, mod, w_up, w_down, mod, post_w]
    if next_layer:
        c_col, ada_w, ada_b, w_in_t = next_layer
        n = ada_w.shape[2]
        n_steps = (seq // tm) * steps_per_tile
        wt_rows = NEXT_WIN_ROWS
        assert n_steps >= n // LANES and n_steps * wt_rows >= w_in_t.shape[1] and w_in_t.shape[1] % wt_rows == 0
        col = lambda i, j: jnp.minimum(i * steps_per_tile + j, n // LANES - 1)
        wt_blk = lambda i, j: jnp.minimum(i * steps_per_tile + j, w_in_t.shape[1] // wt_rows - 1)
        in_specs += [pl.BlockSpec((d, 1), lambda i, j: (0, 0)),
                     pl.BlockSpec((None, d, LANES), lambda i, j: (layer + 1, 0, col(i, j))),
                     pl.BlockSpec((None, 1, LANES), lambda i, j: (layer + 1, 0, col(i, j))),
                     pl.BlockSpec((None, wt_rows, d), lambda i, j: (layer + 1, wt_blk(i, j), 0))]
        out_specs += [pl.BlockSpec((None, 1, LANES), lambda i, j: (0, 0, col(i, j))),
                      pl.BlockSpec((wt_rows, d), lambda i, j: (wt_blk(i, j), 0))]
        out_shape += [jax.ShapeDtypeStruct((1, 1, n), F32), jax.ShapeDtypeStruct(w_in_t.shape[1:], BF16)]
        args += [c_col, ada_w, ada_b, w_in_t]
        vmem += 2 * 2 * d * LANES * 4 + 2 * wt_rows * d * 6
    outs = pl.pallas_call(
        functools.partial(_mlp_kernel, next_layer=bool(next_layer)),
        grid=(seq // tm, steps_per_tile),
        in_specs=in_specs,
        out_specs=out_specs,
        out_shape=out_shape,
        scratch_shapes=[pltpu.VMEM((tm, d), BF16)],
        compiler_params=_cparams("arbitrary", "arbitrary", vmem_limit=vmem),
        name="mlp",
    )(*args)
    return outs if next_layer else outs[0]


def _gathered_group(w_t):
    parts = [w_t[IN_OFFSET[name][0]:IN_OFFSET[name][1]] for name in GATHERED_GROUP]
    used = sum(p.shape[0] for p in parts)
    parts.append(jnp.zeros((IN_GROUP - used, w_t.shape[1]), w_t.dtype))
    return jnp.concatenate(parts, axis=0)


def kernel(x, c, positions, ada_w, ada_b, norm_pre_mix, norm_post_mix, norm_pre_mlp, norm_post_mlp,
           w_in, w_out, gdn_conv_w, gdn_a_log, gdn_dt_bias, gdn_norm_w, ssm_conv_w, ssm_conv_b,
           ssm_a_log, ssm_dt_bias, ssm_d, ssm_norm_w, attn_sinks, sc_conv_w, w_up, w_down):
    batch, seq, d = x.shape
    assert batch == 1 and d == D_MODEL
    depth = ada_w.shape[0]
    xs = x.reshape(seq, d)

    adaln = (c.reshape(d, 1), ada_w, ada_b.reshape(depth, 1, -1))
    mod, cos, sin = _modulation_and_rope(*adaln, 0, positions.reshape(seq, 1))

    rows = lambda v: v.reshape(depth, 1, -1)
    w_in_t = jnp.swapaxes(w_in, 1, 2)
    w_in_t16 = w_in_t[0].astype(BF16)
    w_out_bf16 = w_out.astype(BF16)
    gdn_lanes = _lane_params(LANE_ALPHA, gdn_a_log, gdn_dt_bias)
    ssm_lanes = _lane_params(LANE_DT, ssm_a_log, ssm_dt_bias, ssm_d)
    sink_rows = jnp.broadcast_to(attn_sinks.astype(F32)[:, :, None], (depth, ATTN_Q_HEADS, LANES))

    for i in range(depth):
        proj = _in_projection(xs, i, rows(norm_pre_mix), mod, w_in_t16, _gathered_group(w_in_t16))
        y_a = _gated_deltanet(proj, i, gdn_conv_w, gdn_lanes, rows(gdn_norm_w))
        y_b = _mamba2_ssd(proj, i, ssm_conv_w, rows(ssm_conv_b), ssm_lanes, rows(ssm_norm_w))
        y_c, y_d = _attention_and_short_conv(proj, i, cos, sin, sink_rows, sc_conv_w)
        xs = _out_projection((y_a, y_b, y_c, y_d), i, w_out_bf16, xs, mod, rows(norm_post_mix))
        if i + 1 < depth:
            xs, mod, w_in_t16 = _mlp(xs, i, rows(norm_pre_mlp), mod, w_up, w_down, rows(norm_post_mlp),
                                     next_layer=adaln + (w_in_t,))
        else:
            xs = _mlp(xs, i, rows(norm_pre_mlp), mod, w_up, w_down, rows(norm_post_mlp))
    return xs.reshape(batch, seq, d)
```

```python
import functools

import jax
import jax.numpy as jnp
from jax import lax
from jax.experimental import pallas as pl
from jax.experimental.pallas import tpu as pltpu

F32 = jnp.float32
BF16 = jnp.bfloat16

D_MODEL = 2048
GW = 512
GDN_HEADS, GDN_HEAD_DIM, GDN_CHUNK = 4, 128, 64
SSM_HEADS, SSM_HEAD_DIM, SSM_STATE, SSM_CHUNK = 8, 64, 128, 128
ATTN_HEAD_DIM, ATTN_Q_HEADS, WINDOW = 64, 8, 128
ROPE_THETA = 10000.0
MLP_HIDDEN = 4 * D_MODEL
EPS = 1e-6

SUBLANES = 8
LANES = 128

_IN_SEGMENTS = (("gq", GW), ("gk", GW), ("gv", GW), ("gz", GW), ("gb", GDN_HEADS), ("ga", GDN_HEADS),
                ("sz", GW), ("sx", GW), ("sbc", 4 * SSM_STATE), ("sdt", SSM_HEADS),
                ("aq", GW), ("ak", 2 * ATTN_HEAD_DIM), ("av", 2 * ATTN_HEAD_DIM),
                ("cb", GW), ("cc", GW), ("ch", GW))
IN_OFFSET = {}
_off = 0
for _name, _size in _IN_SEGMENTS:
    IN_OFFSET[_name] = (_off, _off + _size)
    _off += _size
IN_WIDTH = _off

IN_GROUP = 3 * GW
CONTIGUOUS_GROUPS = (("gq", "gk", "gv"), ("sz", "sx", "sbc"), ("cb", "cc", "ch"))
GATHERED_GROUP = ("gz", "aq", "ak", "av", "gb", "ga", "sdt")
IN_COLS = (len(CONTIGUOUS_GROUPS) + 1) * IN_GROUP
for _grp in CONTIGUOUS_GROUPS:
    assert IN_OFFSET[_grp[0]][0] % SUBLANES == 0
    assert all(IN_OFFSET[a][1] == IN_OFFSET[b][0] for a, b in zip(_grp, _grp[1:]))
(SLAB_GQ, SLAB_GK, SLAB_GV, SLAB_SZ, SLAB_SX, SLAB_SBC, SLAB_CB, SLAB_CC, SLAB_CH, SLAB_GZ, SLAB_AQ) = range(11)
SLAB128_AK, SLAB128_AV, SLAB128_SMALL = 44, 45, 46
LANE_BETA, LANE_ALPHA, LANE_DT = 0, 4, 8

MIB = 1024 * 1024
V7X_VMEM_BYTES = 64 * MIB
VMEM_LIMIT = 56 * MIB


def _cparams(*sem, vmem_limit=VMEM_LIMIT):
    assert vmem_limit < V7X_VMEM_BYTES
    return pltpu.CompilerParams(dimension_semantics=sem, vmem_limit_bytes=vmem_limit)


def _sigmoid(x):
    return jax.nn.sigmoid(x)


def _silu(x):
    return x * jax.nn.sigmoid(x)


def _mm(a, b):
    return jnp.dot(a.astype(BF16), b.astype(BF16), preferred_element_type=F32)


def _mm_nt(a, b):
    return lax.dot_general(a.astype(BF16), b.astype(BF16), (((1,), (1,)), ((), ())),
                           preferred_element_type=F32)


def _mm_tn(a, b):
    return lax.dot_general(a.astype(BF16), b.astype(BF16), (((0,), (0,)), ((), ())),
                           preferred_element_type=F32)


def _blk(idx, size):
    assert size & (size - 1) == 0
    return lax.shift_right_logical(idx, size.bit_length() - 1)


def _split3(x):
    x1 = x.astype(BF16)
    r1 = x - x1.astype(F32)
    x2 = r1.astype(BF16)
    r2 = r1 - x2.astype(F32)
    return x1, x2, r2.astype(BF16)


def _cumsum_matrix(rows, chunk):
    r = lax.broadcasted_iota(jnp.int32, (2 * rows, rows), 0)
    s = lax.broadcasted_iota(jnp.int32, (2 * rows, rows), 1)
    rr = jnp.where(r >= rows, r - rows, r)
    same = _blk(rr, chunk) == _blk(s, chunk)
    sel = same & ((r >= rows) | (s <= rr))
    return jnp.where(sel, 1.0, 0.0).astype(BF16)


def _chunk_cumsum(g, mat):
    rows = g.shape[0]
    acc = None
    for part in _split3(g):
        t = jnp.dot(mat, part, preferred_element_type=F32)
        acc = t if acc is None else acc + t
    return acc[:rows], acc[rows:]


def _causal_conv(x, prev, w, taps):
    rows = x.shape[0]
    xp = jnp.concatenate([prev, x], axis=0)
    acc = x * w[taps - 1:taps]
    for d in range(1, taps):
        acc = acc + xp[SUBLANES - d:SUBLANES - d + rows] * w[taps - 1 - d:taps - d]
    return acc


def _causal_conv_staged(x_ref, prev, w, taps, stage_ref):
    rows = x_ref.shape[0]
    x = x_ref[...]
    stage_ref[0:SUBLANES, :] = prev
    stage_ref[SUBLANES:, :] = x
    acc = x * w[taps - 1:taps]
    for d in range(1, taps):
        acc = acc + stage_ref[pl.ds(SUBLANES - d, rows), :] * w[taps - 1 - d:taps - d]
    return acc


MOD_TN = 1536
MOD_ROWS = 64


def _mod_rope_kernel(c_ref, w_ref, b_ref, pos_ref, invf_ref, sign_ref, o_ref, cos_ref, sin_ref):
    d = c_ref.shape[0]

    def body(r, acc):
        rows = pl.ds(pl.multiple_of(r * MOD_ROWS, MOD_ROWS), MOD_ROWS)
        cc = _silu(c_ref[rows, :])
        prod = w_ref[0, rows, :] * cc
        return acc + jnp.sum(prod.reshape(MOD_ROWS // SUBLANES, SUBLANES, MOD_TN), axis=0)

    acc = lax.fori_loop(0, d // MOD_ROWS, body, jnp.zeros((SUBLANES, MOD_TN), F32))
    o_ref[0] = jnp.sum(acc, axis=0, keepdims=True) + b_ref[0]

    ang = pos_ref[...].astype(F32) * invf_ref[...]
    cos_ref[...] = jnp.cos(ang)
    sin_ref[...] = jnp.sin(ang) * sign_ref[...]


def _modulation_and_rope(c_col, ada_w, ada_b, layer, pos_col):
    depth, d, n = ada_w.shape
    seq = pos_col.shape[0]
    steps = n // MOD_TN
    tb = seq // steps
    assert steps * tb == seq and tb % SUBLANES == 0
    half = ATTN_HEAD_DIM // 2
    inv_freq = ROPE_THETA ** (-jnp.arange(0, ATTN_HEAD_DIM, 2, dtype=F32) / ATTN_HEAD_DIM)
    invf = jnp.tile(inv_freq, LANES // half).reshape(1, LANES)
    lane = jnp.arange(LANES)
    sign = jnp.where((lane % ATTN_HEAD_DIM) < half, -1.0, 1.0).astype(F32).reshape(1, LANES)
    table = jax.ShapeDtypeStruct((seq, LANES), F32)
    return pl.pallas_call(
        _mod_rope_kernel,
        grid=(steps,),
        in_specs=[
            pl.BlockSpec((d, 1), lambda j: (0, 0)),
            pl.BlockSpec((1, d, MOD_TN), lambda j: (layer, 0, j)),
            pl.BlockSpec((1, 1, MOD_TN), lambda j: (layer, 0, j)),
            pl.BlockSpec((tb, 1), lambda j: (j, 0)),
            pl.BlockSpec((1, LANES), lambda j: (0, 0)),
            pl.BlockSpec((1, LANES), lambda j: (0, 0)),
        ],
        out_specs=[pl.BlockSpec((1, 1, MOD_TN), lambda j: (0, 0, j)),
                   pl.BlockSpec((tb, LANES), lambda j: (j, 0)),
                   pl.BlockSpec((tb, LANES), lambda j: (j, 0))],
        out_shape=[jax.ShapeDtypeStruct((1, 1, n), F32), table, table],
        compiler_params=_cparams("parallel"),
        name="adaln_mod_rope",
    )(c_col, ada_w, ada_b, pos_col, invf, sign)


def _mod_spec(k):
    return pl.BlockSpec((None, 1, D_MODEL), lambda *grid_idx: (0, 0, k))


NORM_ROWS = 256


def _prenorm_to(x_ref, nw_ref, sc_ref, sh_ref, h_ref):
    tm = x_ref.shape[0]
    gain = nw_ref[...] * (1.0 + sc_ref[...])
    shift = sh_ref[...]

    def body(r, carry):
        rows = pl.ds(pl.multiple_of(r * NORM_ROWS, NORM_ROWS), NORM_ROWS)
        x = x_ref[rows, :]
        inv = lax.rsqrt(jnp.mean(x * x, axis=-1, keepdims=True) + EPS)
        h_ref[rows, :] = (x * inv * gain + shift).astype(BF16)
        return carry

    lax.fori_loop(0, tm // NORM_ROWS, body, 0)


INPROJ_TM = 1024

MOD_SHIFT_A, MOD_SCALE_A, MOD_GATE_A, MOD_SHIFT_M, MOD_SCALE_M, MOD_GATE_M = range(6)


def _layer_spec(shape, layer, *rest):
    rest = rest or (0,) * len(shape)

    def index_map(*grid_idx):
        return (layer,) + tuple(r(*grid_idx) if callable(r) else r for r in rest)

    return pl.BlockSpec((None,) + tuple(shape), index_map)


def _inproj_kernel(x_ref, nw_ref, sc_ref, sh_ref, wt_ref, tail_ref, wo_ref, o_ref, wo16_ref, h_ref):
    j = pl.program_id(1)
    wo16_ref[...] = wo_ref[...].astype(BF16)

    @pl.when(j == 0)
    def _():
        _prenorm_to(x_ref, nw_ref, sc_ref, sh_ref, h_ref)

    def project(w_ref):
        o_ref[...] = _mm_nt(h_ref[...], w_ref[...])

    @pl.when(j < len(CONTIGUOUS_GROUPS))
    def _():
        project(wt_ref)

    @pl.when(j == len(CONTIGUOUS_GROUPS))
    def _():
        project(tail_ref)


def _group_row(j):
    starts = [IN_OFFSET[grp[0]][0] // SUBLANES for grp in CONTIGUOUS_GROUPS]
    row8 = starts[-1]
    for k in range(len(starts) - 2, -1, -1):
        row8 = jnp.where(j <= k, starts[k], row8)
    return row8 * SUBLANES


def _in_projection(x, layer, norm_w, mod, w_t, w_gathered, w_out):
    seq, d = x.shape
    tm = min(INPROJ_TM, seq)
    n_groups = IN_COLS // IN_GROUP
    wo_rows = w_out.shape[1] // ((seq // tm) * n_groups)
    assert wo_rows * (seq // tm) * n_groups == w_out.shape[1] and wo_rows % (2 * SUBLANES) == 0
    return pl.pallas_call(
        _inproj_kernel,
        grid=(seq // tm, n_groups),
        in_specs=[
            pl.BlockSpec((tm, d), lambda i, j: (i, 0)),
            _layer_spec((1, d), layer),
            _mod_spec(MOD_SCALE_A),
            _mod_spec(MOD_SHIFT_A),
            pl.BlockSpec((pl.Element(IN_GROUP), pl.Element(d)), lambda i, j: (_group_row(j), 0)),
            pl.BlockSpec((IN_GROUP, d), lambda i, j: (0, 0), pipeline_mode=pl.Buffered(1)),
            pl.BlockSpec((None, wo_rows, d), lambda i, j: (layer, i * n_groups + j, 0)),
        ],
        out_specs=[pl.BlockSpec((tm, IN_GROUP), lambda i, j: (i, j)),
                   pl.BlockSpec((wo_rows, d), lambda i, j: (i * n_groups + j, 0))],
        out_shape=[jax.ShapeDtypeStruct((seq, IN_COLS), F32), jax.ShapeDtypeStruct(w_out.shape[1:], BF16)],
        scratch_shapes=[pltpu.VMEM((tm, d), BF16)],
        compiler_params=_cparams("arbitrary", "arbitrary"),
        name="in_proj",
    )(x, norm_w, mod, mod, w_t, w_gathered, w_out)


GDN_TB = 512


def _l2norm(x):
    return x * lax.rsqrt(jnp.sum(x * x, axis=-1, keepdims=True) + EPS)


GDN_BLK = 128
GDN_PREP_STAGES_PER_CHUNK = 2


def _unit_lower_inverse_minus_identity(ms, r, s):
    same8 = _blk(r, 8) == _blk(s, 8)
    same16 = _blk(r, 16) == _blk(s, 16)
    same32 = _blk(r, 32) == _blk(s, 32)
    n1 = [jnp.where(same8, -m, 0.0) for m in ms]
    n2 = [_mm(a, a) for a in n1]
    yield
    n3 = [_mm(a, b) for a, b in zip(n1, n2)]
    n4 = [_mm(b, b) for b in n2]
    yield
    ys = [a + b + c for a, b, c in zip(n1, n2, n3)]
    ts = [_mm(y, d) for y, d in zip(ys, n4)]
    yield
    ys = [y + d + t for y, d, t in zip(ys, n4, ts)]
    for sel in (same16 & ~same8, same32 & ~same16, ~same32):
        blks = [jnp.where(sel, m, 0.0) for m in ms]
        cs = [b + _mm(y, b) for y, b in zip(ys, blks)]
        yield
        ts = [_mm(c, y) for c, y in zip(cs, ys)]
        yield
        ys = [y - c - t for y, c, t in zip(ys, cs, ts)]
    return ys


def _run_interleaved(main, side, main_stages_per_side_stage):
    side_live = True
    done = 0
    while True:
        try:
            next(main)
        except StopIteration as stop:
            result = stop.value
            break
        done += 1
        if side_live and done % main_stages_per_side_stage == 0:
            side_live = next(side, StopIteration) is not StopIteration
    while side_live:
        side_live = next(side, StopIteration) is not StopIteration
    return result


def _gdn_prepare(q_ref, k_ref, v_ref, qp_ref, kp_ref, vp_ref, sm_ref, cw_ref, lp_ref, cum_ref, conv_ref, first_block):
    tb = q_ref.shape[0]
    hd = GDN_HEAD_DIM
    nblk = tb // GDN_BLK

    keep_prev = jnp.where(first_block, 0.0, 1.0)
    cw = cw_ref[...]

    def conv_silu(x_ref, p_ref, idx):
        w = cw[:, idx * GW:(idx + 1) * GW]
        return _silu(_causal_conv_staged(x_ref, p_ref[...] * keep_prev, w, 4, conv_ref.at[idx]))

    q = conv_silu(q_ref, qp_ref, 0)
    yield
    k = conv_silu(k_ref, kp_ref, 1)
    yield
    v = conv_silu(v_ref, vp_ref, 2)
    yield

    small = sm_ref[...]
    a_log = lp_ref[0:1, :]
    dt_bias = lp_ref[1:2, :]
    beta_all = _sigmoid(small)
    g_all = -jnp.exp(a_log) * jax.nn.softplus(small + dt_bias)
    gc_all, gtot_all = _chunk_cumsum(g_all, cum_ref[...])
    gc_t = gc_all.T
    yield

    r = lax.broadcasted_iota(jnp.int32, (GDN_BLK, GDN_BLK), 0)
    s = lax.broadcasted_iota(jnp.int32, (GDN_BLK, GDN_BLK), 1)
    same_chunk = _blk(r, GDN_CHUNK) == _blk(s, GDN_CHUNK)
    mask_incl = same_chunk & (s <= r)
    mask_strict = same_chunk & (s < r)

    problems = [(slice(b * GDN_BLK, (b + 1) * GDN_BLK), h) for b in range(nblk) for h in range(GDN_HEADS)]
    head_cols = lambda h: slice(h * hd, (h + 1) * hd)
    gate = lambda arr, rows, lane: arr[rows, lane:lane + 1]

    qn = [_l2norm(q[rows, head_cols(h)]) * (hd ** -0.5) for rows, h in problems]
    kn = [_l2norm(k[rows, head_cols(h)]) for rows, h in problems]
    beta = [gate(beta_all, rows, LANE_BETA + h) for rows, h in problems]
    gcol = [gate(gc_all, rows, LANE_ALPHA + h) for rows, h in problems]
    gtot = [gate(gtot_all, rows, LANE_ALPHA + h) for rows, h in problems]
    grow = [gc_t[LANE_ALPHA + h:LANE_ALPHA + h + 1, rows] for rows, h in problems]
    yield

    decay = [jnp.exp(jnp.where(mask_incl, gc - gr, -jnp.inf)) for gc, gr in zip(gcol, grow)]
    k_beta = [kk * b for kk, b in zip(kn, beta)]
    k16 = [kk.astype(BF16) for kk in kn]
    yield
    ms = [jnp.where(mask_strict, _mm_nt(kb, kk) * d, 0.0) for kb, kk, d in zip(k_beta, k16, decay)]
    yield
    attn = [(_mm_nt(qq, kk) * d).astype(BF16) for qq, kk, d in zip(qn, k16, decay)]
    ys = yield from _unit_lower_inverse_minus_identity(ms, r, s)
    yield

    e_gc = [jnp.exp(gc) for gc in gcol]
    rhs = [jnp.concatenate([v[rows, head_cols(h)] * b, kb * e], axis=1)
           for (rows, h), b, kb, e in zip(problems, beta, k_beta, e_gc)]
    uw = [x + _mm(y, x) for y, x in zip(ys, rhs)]
    yield
    u = [x[:, :hd] for x in uw]
    w = [x[:, hd:].astype(BF16) for x in uw]
    q_dec = [(qq * e).astype(BF16) for qq, e in zip(qn, e_gc)]
    k_dec = [(kk * jnp.exp(gt - gc)).astype(BF16) for kk, gt, gc in zip(kn, gtot, gcol)]
    return u, w, q_dec, k_dec, attn, gtot_all


def _gdn_recurrence(u, w, q_dec, k_dec, attn, gtot_all, state_ref, z, nw_ref, o_ref):
    hd = GDN_HEAD_DIM
    nblk = o_ref.shape[0] // GDN_BLK
    chunks_per_blk = GDN_BLK // GDN_CHUNK
    head_cols = lambda h: slice(h * hd, (h + 1) * hd)

    states = [state_ref[h] for h in range(GDN_HEADS)]
    v_new = [[] for _ in range(nblk * GDN_HEADS)]
    o_inter = [[] for _ in range(nblk * GDN_HEADS)]
    for b in range(nblk):
        for c in range(chunks_per_blk):
            rows = slice(c * GDN_CHUNK, (c + 1) * GDN_CHUNK)
            first = b * GDN_BLK + c * GDN_CHUNK
            for h in range(GDN_HEADS):
                p = b * GDN_HEADS + h
                st16 = states[h].astype(BF16)
                vn = u(p, rows) - jnp.dot(w(p, rows), st16, preferred_element_type=F32)
                o_inter[p].append(jnp.dot(q_dec(p, rows), st16, preferred_element_type=F32))
                chunk_dec = jnp.exp(gtot_all[first:first + 1, LANE_ALPHA + h:LANE_ALPHA + h + 1])
                states[h] = states[h] * chunk_dec + _mm_tn(k_dec(p, rows), vn)
                v_new[p].append(vn)
            yield
        rows = slice(b * GDN_BLK, (b + 1) * GDN_BLK)
        for h in range(GDN_HEADS):
            p = b * GDN_HEADS + h
            o = jnp.concatenate(o_inter[p], axis=0) + _mm(attn(p, slice(None)), jnp.concatenate(v_new[p], axis=0))
            o = o * lax.rsqrt(jnp.mean(o * o, axis=-1, keepdims=True) + EPS) * nw_ref[...]
            o_ref[rows, head_cols(h)] = (o * _silu(z[rows, head_cols(h)])).astype(o_ref.dtype)
    for h in range(GDN_HEADS):
        state_ref[h] = states[h]


def _gdn_kernel(q_ref, k_ref, v_ref, z_ref, qp_ref, kp_ref, vp_ref, sm_ref, cw_ref, lp_ref, nw_ref, cum_ref,
                o_ref, state_ref, conv_ref, u_ref, w_ref, qd_ref, kd_ref, at_ref, gt_ref):
    step = pl.program_id(0)
    write_slot = step % 2
    read_slot = 1 - write_slot
    n_problems = u_ref.shape[1]

    @pl.when(step == 0)
    def _():
        state_ref[...] = jnp.zeros_like(state_ref)
        for ref in (u_ref, w_ref, qd_ref, kd_ref, at_ref, gt_ref):
            ref[1] = jnp.zeros(ref.shape[1:], ref.dtype)

    recurrence = _gdn_recurrence(*[(lambda p, rows, ref=ref: ref[read_slot, p, rows])
                                   for ref in (u_ref, w_ref, qd_ref, kd_ref, at_ref)],
                                 gt_ref[read_slot], state_ref, z_ref[...], nw_ref, o_ref)
    prepare = _gdn_prepare(q_ref, k_ref, v_ref, qp_ref, kp_ref, vp_ref, sm_ref, cw_ref, lp_ref, cum_ref, conv_ref,
                           step == 0)
    u, w, q_dec, k_dec, attn, gtot_all = _run_interleaved(prepare, recurrence, GDN_PREP_STAGES_PER_CHUNK)
    for p in range(n_problems):
        u_ref[write_slot, p] = u[p]
        w_ref[write_slot, p] = w[p]
        qd_ref[write_slot, p] = q_dec[p]
        kd_ref[write_slot, p] = k_dec[p]
        at_ref[write_slot, p] = attn[p]
    gt_ref[write_slot] = gtot_all


def _prev_rows_spec(tb, width, slab):
    blocks = tb // SUBLANES
    return pl.BlockSpec((SUBLANES, width), lambda i: (jnp.maximum(i * blocks - 1, 0), slab))


def _lane_params(lane0, *rows):
    stacked = jnp.stack([r.astype(F32) for r in rows], axis=1)
    nrows, heads = stacked.shape[1:]
    return jnp.pad(stacked, ((0, 0), (0, SUBLANES - nrows), (lane0, LANES - lane0 - heads)))


def _gated_deltanet(proj, layer, conv_w, lane_params, norm_w):
    seq = proj.shape[0]
    tb = min(GDN_TB, seq)
    nsteps = seq // tb
    n_problems = (tb // GDN_BLK) * GDN_HEADS
    prep = lambda t: jnp.minimum(t, nsteps - 1)
    done = lambda t: jnp.maximum(t - 1, 0)
    slab = lambda j: pl.BlockSpec((tb, GW), lambda t: (prep(t), j))
    prev = lambda j: pl.BlockSpec((SUBLANES, GW),
                                  lambda t: (jnp.maximum(prep(t) * (tb // SUBLANES) - 1, 0), j))
    full = lambda shape: _layer_spec(shape, layer)
    factor = lambda dtype: pltpu.VMEM((2, n_problems, GDN_BLK, GDN_HEAD_DIM), dtype)
    return pl.pallas_call(
        _gdn_kernel,
        grid=(nsteps + 1,),
        in_specs=[
            slab(SLAB_GQ), slab(SLAB_GK), slab(SLAB_GV),
            pl.BlockSpec((tb, GW), lambda t: (done(t), SLAB_GZ)),
            prev(SLAB_GQ), prev(SLAB_GK), prev(SLAB_GV),
            pl.BlockSpec((tb, LANES), lambda t: (prep(t), SLAB128_SMALL)),
            full((4, 3 * GW)), full((SUBLANES, LANES)), full((1, GDN_HEAD_DIM)),
            pl.BlockSpec((2 * tb, tb), lambda t: (0, 0)),
        ],
        out_specs=pl.BlockSpec((tb, GW), lambda t: (done(t), 0)),
        out_shape=jax.ShapeDtypeStruct((seq, GW), BF16),
        scratch_shapes=[pltpu.VMEM((GDN_HEADS, GDN_HEAD_DIM, GDN_HEAD_DIM), F32),
                        pltpu.VMEM((3, tb + SUBLANES, GW), F32),
                        factor(F32), factor(BF16), factor(BF16), factor(BF16), factor(BF16),
                        pltpu.VMEM((2, tb, LANES), F32)],
        compiler_params=_cparams("arbitrary"),
        name="gated_deltanet",
    )(proj, proj, proj, proj, proj, proj, proj, proj, conv_w, lane_params, norm_w, _cumsum_matrix(tb, GDN_CHUNK))


SSD_TB = 256


def _ssd_kernel(z_ref, x_ref, bc_ref, xp_ref, bcp_ref, sm_ref, cw_ref, cb_ref, lp_ref, nw_ref, cum_ref,
                o_ref, state_ref):
    step = pl.program_id(0)
    tb = z_ref.shape[0]
    nstate = SSM_STATE
    chunk = SSM_CHUNK
    nchunks = tb // chunk
    n_pairs = SSM_HEADS // 2
    pairs_per_group = n_pairs // 2

    @pl.when(step == 0)
    def _():
        state_ref[...] = jnp.zeros_like(state_ref)

    keep_prev = jnp.where(step == 0, 0.0, 1.0)
    cw = cw_ref[...]
    cbias = cb_ref[...]
    xs = _silu(_causal_conv(x_ref[...], xp_ref[...] * keep_prev, cw[:, :GW], 4) + cbias[:, :GW])
    bc = _silu(_causal_conv(bc_ref[...], bcp_ref[...] * keep_prev, cw[:, GW:], 4) + cbias[:, GW:])
    z = z_ref[...]

    small = sm_ref[...]
    a_neg = -jnp.exp(lp_ref[0:1, :])
    dt_all = jax.nn.softplus(small + lp_ref[1:2, :])
    dskip_all = lp_ref[2:3, :]
    gc_all, gtot_all = _chunk_cumsum(dt_all * a_neg, cum_ref[...])
    gc_t = gc_all.T

    r = lax.broadcasted_iota(jnp.int32, (chunk, chunk), 0)
    s = lax.broadcasted_iota(jnp.int32, (chunk, chunk), 1)
    mask_incl = s <= r
    lane = lax.broadcasted_iota(jnp.int32, (1, LANES), 1)
    first_half = lane < SSM_HEAD_DIM
    by_half = lambda a, b: jnp.where(first_half, a, b)

    problems = [(c, p) for c in range(nchunks) for p in range(n_pairs)]
    rows_of = lambda c: slice(c * chunk, (c + 1) * chunk)
    cols_of = lambda p: slice(p * LANES, (p + 1) * LANES)
    lanes_of = lambda p: (LANE_DT + 2 * p, LANE_DT + 2 * p + 1)
    col = lambda arr, c, l: arr[rows_of(c), l:l + 1]

    b_mat = {(c, g): bc[rows_of(c), g * nstate:(g + 1) * nstate].astype(BF16)
             for c in range(nchunks) for g in range(2)}
    c_mat = {(c, g): bc[rows_of(c), (2 + g) * nstate:(3 + g) * nstate].astype(BF16)
             for c in range(nchunks) for g in range(2)}
    cb = {key: _mm_nt(c_mat[key], b_mat[key]) for key in b_mat}

    x_pair = [xs[rows_of(c), cols_of(p)] for c, p in problems]
    xdt = [x * by_half(col(dt_all, c, lanes_of(p)[0]), col(dt_all, c, lanes_of(p)[1]))
           for x, (c, p) in zip(x_pair, problems)]
    y_diag = []
    for (c, p), xd in zip(problems, xdt):
        halves = (by_half(xd, 0.0), by_half(0.0, xd))
        acc = None
        for l, half in zip(lanes_of(p), halves):
            grow = gc_t[l:l + 1, rows_of(c)]
            lmat = jnp.exp(jnp.where(mask_incl, col(gc_all, c, l) - grow, -jnp.inf))
            t = _mm(cb[c, p // pairs_per_group] * lmat, half)
            acc = t if acc is None else acc + t
        y_diag.append(acc)

    e_in = [by_half(jnp.exp(col(gc_all, c, lanes_of(p)[0])), jnp.exp(col(gc_all, c, lanes_of(p)[1])))
            for c, p in problems]
    e_out = [by_half(jnp.exp(col(gtot_all, c, lanes_of(p)[0]) - col(gc_all, c, lanes_of(p)[0])),
                     jnp.exp(col(gtot_all, c, lanes_of(p)[1]) - col(gc_all, c, lanes_of(p)[1])))
             for c, p in problems]
    states = [state_ref[p] for p in range(n_pairs)]
    y_off = []
    for idx, (c, p) in enumerate(problems):
        la, lb = lanes_of(p)
        g = p // pairs_per_group
        y_off.append(e_in[idx] * _mm(c_mat[c, g], states[p]))
        first = c * chunk
        cd_lane = by_half(jnp.exp(gtot_all[first:first + 1, la:la + 1]),
                          jnp.exp(gtot_all[first:first + 1, lb:lb + 1]))
        states[p] = states[p] * cd_lane + _mm_tn(b_mat[c, g], xdt[idx] * e_out[idx])
    for p in range(n_pairs):
        state_ref[p] = states[p]

    ys = []
    for idx, (c, p) in enumerate(problems):
        la, lb = lanes_of(p)
        y = y_diag[idx] + y_off[idx] + x_pair[idx] * by_half(dskip_all[:, la:la + 1], dskip_all[:, lb:lb + 1])
        ys.append(y * _silu(z[rows_of(c), cols_of(p)]))
    sumsq = [jnp.sum(y * y, axis=-1, keepdims=True) for y in ys]
    for c in range(nchunks):
        for g in range(2):
            members = [c * n_pairs + g * pairs_per_group + k for k in range(pairs_per_group)]
            total = sumsq[members[0]]
            for idx in members[1:]:
                total = total + sumsq[idx]
            inv = lax.rsqrt(total / (pairs_per_group * LANES) + EPS)
            for idx in members:
                cols = cols_of(problems[idx][1])
                o_ref[rows_of(c), cols] = (ys[idx] * inv * nw_ref[:, cols]).astype(o_ref.dtype)


def _mamba2_ssd(proj, layer, conv_w, conv_b, lane_params, norm_w):
    seq = proj.shape[0]
    tb = min(SSD_TB, seq)
    slab = lambda j: pl.BlockSpec((tb, GW), lambda i: (i, j))
    full = lambda shape: _layer_spec(shape, layer)
    return pl.pallas_call(
        _ssd_kernel,
        grid=(seq // tb,),
        in_specs=[
            slab(SLAB_SZ), slab(SLAB_SX), slab(SLAB_SBC),
            _prev_rows_spec(tb, GW, SLAB_SX), _prev_rows_spec(tb, GW, SLAB_SBC),
            pl.BlockSpec((tb, LANES), lambda i: (i, SLAB128_SMALL)),
            full((4, 2 * GW)), full((1, 2 * GW)), full((SUBLANES, LANES)), full((1, GW)),
            pl.BlockSpec((2 * tb, tb), lambda i: (0, 0)),
        ],
        out_specs=pl.BlockSpec((tb, GW), lambda i: (i, 0)),
        out_shape=jax.ShapeDtypeStruct((seq, GW), BF16),
        scratch_shapes=[pltpu.VMEM((SSM_HEADS // 2, SSM_STATE, LANES), F32)],
        compiler_params=_cparams("arbitrary"),
        name="mamba2_ssd",
    )(proj, proj, proj, proj, proj, proj, conv_w, conv_b, lane_params, norm_w, _cumsum_matrix(tb, SSM_CHUNK))


def _rotate_half(x, lane_in_head_low):
    width = x.shape[1]
    half = ATTN_HEAD_DIM // 2
    ahead = pltpu.roll(x, width - half, axis=1)
    behind = pltpu.roll(x, half, axis=1)
    return jnp.where(lane_in_head_low, ahead, behind)


SWA_TB = 512


def _swa_kernel(q_ref, k_ref, v_ref, cos_ref, sin_ref, sink_ref, cb_ref, cc_ref, ch_ref, ccp_ref, chp_ref, cw_ref,
                o_ref, od_ref, kprev_ref, vprev_ref):
    step = pl.program_id(0)
    tb = q_ref.shape[0]
    win = WINDOW
    hd = ATTN_HEAD_DIM

    @pl.when(step == 0)
    def _():
        kprev_ref[...] = jnp.zeros_like(kprev_ref)
        vprev_ref[...] = jnp.zeros_like(vprev_ref)

    keep_prev = jnp.where(step == 0, 0.0, 1.0)
    c_h = cc_ref[...] * ch_ref[...]
    c_h_prev = ccp_ref[...] * chp_ref[...] * keep_prev
    od_ref[...] = (cb_ref[...] * _causal_conv(c_h, c_h_prev, cw_ref[...], 3)).astype(od_ref.dtype)

    cos = cos_ref[...]
    sin = sin_ref[...]
    lane = lax.broadcasted_iota(jnp.int32, (1, LANES), 1)
    low = (lane & (hd - 1)) < (hd // 2)
    first_half = lane < hd

    k_cur = k_ref[...]
    k_cur = k_cur * cos + _rotate_half(k_cur, low) * sin
    v_cur = v_ref[...]
    k_all = jnp.concatenate([kprev_ref[...], k_cur], axis=0)
    v_all = jnp.concatenate([vprev_ref[...], v_cur], axis=0)
    k_swap = pltpu.roll(k_all, hd, axis=1)
    v_swap = pltpu.roll(v_all, hd, axis=1)

    def placed(x, x_swap, kv_head, pos):
        src = x if kv_head == pos else x_swap
        keep = first_half if pos == 0 else ~first_half
        return jnp.where(keep, src, 0.0).astype(BF16)

    qi = lax.broadcasted_iota(jnp.int32, (win, 2 * win), 0)
    kj = lax.broadcasted_iota(jnp.int32, (win, 2 * win), 1)
    rel = qi + win - kj
    band = (rel >= 0) & (rel < win)
    band_first = band & ((kj >= win) | (step > 0))

    problems = [(j, h) for j in range(tb // win) for h in range(ATTN_Q_HEADS)]
    kv_of = lambda head: head // (ATTN_Q_HEADS // 2)
    pair_cols = lambda pair: slice(pair * LANES, (pair + 1) * LANES)
    k_at = {(g, pos): placed(k_all, k_swap, g, pos) for g in range(2) for pos in range(2)}
    v_at = {(g, pos): placed(v_all, v_swap, g, pos) for g in range(2) for pos in range(2)}
    q_pairs = []
    for pair in range(ATTN_Q_HEADS // 2):
        q_pair = q_ref[:, pair_cols(pair)]
        q_pairs.append((q_pair * cos + _rotate_half(q_pair, low) * sin).astype(BF16))
    sinks = [sink_ref[h:h + 1, 0:1] for _, h in problems]
    sc = [jnp.where(band_first if j == 0 else band,
                    _mm_nt(q_pairs[h // 2][j * win:(j + 1) * win],
                           k_at[kv_of(h), h % 2][j * win:(j + 2) * win]) * (hd ** -0.5),
                    -jnp.inf)
          for j, h in problems]
    mx = [jnp.maximum(jnp.max(s_, axis=-1, keepdims=True), sk) for s_, sk in zip(sc, sinks)]
    p = [jnp.exp(s_ - m_) for s_, m_ in zip(sc, mx)]
    denom = [jnp.sum(p_, axis=-1, keepdims=True) + jnp.exp(sk - m_) for p_, sk, m_ in zip(p, sinks, mx)]
    o = [_mm(p_, v_at[kv_of(h), h % 2][j * win:(j + 2) * win]) / d_
         for (j, h), p_, d_ in zip(problems, p, denom)]
    for idx in range(0, len(problems), 2):
        j, h = problems[idx]
        o_ref[j * win:(j + 1) * win, pair_cols(h // 2)] = (o[idx] + o[idx + 1]).astype(o_ref.dtype)

    kprev_ref[...] = k_cur[tb - win:]
    vprev_ref[...] = v_cur[tb - win:]


def _attention_and_short_conv(proj, layer, cos, sin, sink_rows, sc_conv_w):
    seq = proj.shape[0]
    tb = min(SWA_TB, seq)
    slab = lambda j: pl.BlockSpec((tb, GW), lambda i: (i, j))
    out = jax.ShapeDtypeStruct((seq, GW), BF16)
    return pl.pallas_call(
        _swa_kernel,
        grid=(seq // tb,),
        in_specs=[
            slab(SLAB_AQ),
            pl.BlockSpec((tb, LANES), lambda i: (i, SLAB128_AK)),
            pl.BlockSpec((tb, LANES), lambda i: (i, SLAB128_AV)),
            pl.BlockSpec((tb, LANES), lambda i: (i, 0)),
            pl.BlockSpec((tb, LANES), lambda i: (i, 0)),
            _layer_spec((ATTN_Q_HEADS, LANES), layer),
            slab(SLAB_CB), slab(SLAB_CC), slab(SLAB_CH),
            _prev_rows_spec(tb, GW, SLAB_CC), _prev_rows_spec(tb, GW, SLAB_CH),
            _layer_spec((3, GW), layer),
        ],
        out_specs=[pl.BlockSpec((tb, GW), lambda i: (i, 0))] * 2,
        out_shape=[out, out],
        scratch_shapes=[pltpu.VMEM((WINDOW, LANES), F32), pltpu.VMEM((WINDOW, LANES), F32)],
        compiler_params=_cparams("arbitrary"),
        name="sliding_window_attn",
    )(proj, proj, proj, cos, sin, sink_rows, proj, proj, proj, proj, proj, sc_conv_w)


OUTPROJ_TM = 512


def _outproj_kernel(ya_ref, yb_ref, yc_ref, yd_ref, w_ref, x_ref, gate_ref, nw_ref, o_ref):
    y = jnp.concatenate([ya_ref[...], yb_ref[...], yc_ref[...], yd_ref[...]], axis=1)
    y = jnp.dot(y, w_ref[...], preferred_element_type=F32)
    inv = lax.rsqrt(jnp.mean(y * y, axis=-1, keepdims=True) + EPS)
    o_ref[...] = x_ref[...] + y * inv * (gate_ref[...] * nw_ref[...])


def _out_projection(ys, layer, w_bf16, x, mod, norm_w):
    seq, d = x.shape
    tm = min(OUTPROJ_TM, seq)
    return pl.pallas_call(
        _outproj_kernel,
        grid=(seq // tm,),
        in_specs=[pl.BlockSpec((tm, GW), lambda i: (i, 0))] * 4 + [
            pl.BlockSpec((4 * GW, d), lambda i: (0, 0)),
            pl.BlockSpec((tm, d), lambda i: (i, 0)),
            _mod_spec(MOD_GATE_A),
            _layer_spec((1, d), layer),
        ],
        out_specs=pl.BlockSpec((tm, d), lambda i: (i, 0)),
        out_shape=jax.ShapeDtypeStruct((seq, d), F32),
        compiler_params=_cparams("parallel"),
        name="out_proj",
    )(*ys, w_bf16, x, mod, norm_w)


MLP_TM = 1024
MLP_TH = 512
NEXT_WIN_ROWS = 48


def _mlp_kernel(*refs, next_layer):
    if next_layer:
        (x_ref, nw_ref, sc_ref, sh_ref, wu_ref, wd_ref, gate_ref, pw_ref, c_ref, aw_ref, ab_ref, wt_ref,
         o_ref, mod_ref, wt16_ref, h_ref) = refs
    else:
        x_ref, nw_ref, sc_ref, sh_ref, wu_ref, wd_ref, gate_ref, pw_ref, o_ref, h_ref = refs
    j = pl.program_id(1)

    @pl.when(j == 0)
    def _():
        _prenorm_to(x_ref, nw_ref, sc_ref, sh_ref, h_ref)
        o_ref[...] = jnp.zeros_like(o_ref)

    hid = _mm(h_ref[...], wu_ref[...])
    hid = jnp.square(jnp.maximum(hid, 0.0))
    o_ref[...] += _mm(hid, wd_ref[...])

    if next_layer:
        prod = aw_ref[...] * _silu(c_ref[...])
        part = jnp.sum(prod.reshape(prod.shape[0] // SUBLANES, SUBLANES, LANES), axis=0)
        mod_ref[...] = jnp.sum(part, axis=0, keepdims=True) + ab_ref[...]
        wt16_ref[...] = wt_ref[...].astype(BF16)

    @pl.when(j == pl.num_programs(1) - 1)
    def _():
        gain = gate_ref[...] * pw_ref[...]

        def body(r, carry):
            rows = pl.ds(pl.multiple_of(r * NORM_ROWS, NORM_ROWS), NORM_ROWS)
            y = o_ref[rows, :]
            inv = lax.rsqrt(jnp.mean(y * y, axis=-1, keepdims=True) + EPS)
            o_ref[rows, :] = x_ref[rows, :] + y * inv * gain
            return carry

        lax.fori_loop(0, x_ref.shape[0] // NORM_ROWS, body, 0)


def _mlp(x, layer, norm_w, mod, w_up, w_down, post_w, next_layer=None):
    seq, d = x.shape
    hidden = w_up.shape[2]
    tm = min(MLP_TM, seq)
    steps_per_tile = hidden // MLP_TH
    vmem = (2 * 2 * tm * d * 4 + tm * d * 2 + 2 * 2 * d * MLP_TH * w_up.dtype.itemsize
            + d * MLP_TH * 2 + tm * MLP_TH * 6)
    in_specs = [
        pl.BlockSpec((tm, d), lambda i, j: (i, 0)),
        _layer_spec((1, d), layer),
        _mod_spec(MOD_SCALE_M),
        _mod_spec(MOD_SHIFT_M),
        _layer_spec((d, MLP_TH), layer, 0, lambda i, j: j),
        _layer_spec((MLP_TH, d), layer, lambda i, j: j, 0),
        _mod_spec(MOD_GATE_M),
        _layer_spec((1, d), layer),
    ]
    out_specs = [pl.BlockSpec((tm, d), lambda i, j: (i, 0))]
    out_shape = [jax.ShapeDtypeStruct((seq, d), F32)]
    args = [x, norm_w, mod, mod, w_up, w_down, mod, post_w]
    if next_layer:
        c_col, ada_w, ada_b, w_in_t = next_layer
        n = ada_w.shape[2]
        n_steps = (seq // tm) * steps_per_tile
        wt_rows = NEXT_WIN_ROWS
        assert n_steps >= n // LANES and n_steps * wt_rows >= w_in_t.shape[1] and w_in_t.shape[1] % wt_rows == 0
        col = lambda i, j: jnp.minimum(i * steps_per_tile + j, n // LANES - 1)
        wt_blk = lambda i, j: jnp.minimum(i * steps_per_tile + j, w_in_t.shape[1] // wt_rows - 1)
        in_specs += [pl.BlockSpec((d, 1), lambda i, j: (0, 0)),
                     pl.BlockSpec((None, d, LANES), lambda i, j: (layer + 1, 0, col(i, j))),
                     pl.BlockSpec((None, 1, LANES), lambda i, j: (layer + 1, 0, col(i, j))),
                     pl.BlockSpec((None, wt_rows, d), lambda i, j: (layer + 1, wt_blk(i, j), 0))]
        out_specs += [pl.BlockSpec((None, 1, LANES), lambda i, j: (0, 0, col(i, j))),
                      pl.BlockSpec((wt_rows, d), lambda i, j: (wt_blk(i, j), 0))]
        out_shape += [jax.ShapeDtypeStruct((1, 1, n), F32), jax.ShapeDtypeStruct(w_in_t.shape[1:], BF16)]
        args += [c_col, ada_w, ada_b, w_in_t]
        vmem += 2 * 2 * d * LANES * 4 + 2 * wt_rows * d * 6
    outs = pl.pallas_call(
        functools.partial(_mlp_kernel, next_layer=bool(next_layer)),
        grid=(seq // tm, steps_per_tile),
        in_specs=in_specs,
        out_specs=out_specs,
        out_shape=out_shape,
        scratch_shapes=[pltpu.VMEM((tm, d), BF16)],
        compiler_params=_cparams("arbitrary", "arbitrary", vmem_limit=vmem),
        name="mlp",
    )(*args)
    return outs if next_layer else outs[0]


def _gathered_group(w_t):
    parts = [w_t[IN_OFFSET[name][0]:IN_OFFSET[name][1]] for name in GATHERED_GROUP]
    used = sum(p.shape[0] for p in parts)
    parts.append(jnp.zeros((IN_GROUP - used, w_t.shape[1]), w_t.dtype))
    return jnp.concatenate(parts, axis=0)


def kernel(x, c, positions, ada_w, ada_b, norm_pre_mix, norm_post_mix, norm_pre_mlp, norm_post_mlp,
           w_in, w_out, gdn_conv_w, gdn_a_log, gdn_dt_bias, gdn_norm_w, ssm_conv_w, ssm_conv_b,
           ssm_a_log, ssm_dt_bias, ssm_d, ssm_norm_w, attn_sinks, sc_conv_w, w_up, w_down):
    batch, seq, d = x.shape
    assert batch == 1 and d == D_MODEL
    depth = ada_w.shape[0]
    xs = x.reshape(seq, d)

    adaln = (c.reshape(d, 1), ada_w, ada_b.reshape(depth, 1, -1))
    mod, cos, sin = _modulation_and_rope(*adaln, 0, positions.reshape(seq, 1))

    rows = lambda v: v.reshape(depth, 1, -1)
    w_in_t = jnp.swapaxes(w_in, 1, 2)
    w_in_t16 = w_in_t[0].astype(BF16)
    gdn_lanes = _lane_params(LANE_ALPHA, gdn_a_log, gdn_dt_bias)
    ssm_lanes = _lane_params(LANE_DT, ssm_a_log, ssm_dt_bias, ssm_d)
    sink_rows = jnp.broadcast_to(attn_sinks.astype(F32)[:, :, None], (depth, ATTN_Q_HEADS, LANES))

    for i in range(depth):
        proj, w_out_bf16 = _in_projection(xs, i, rows(norm_pre_mix), mod, w_in_t16, _gathered_group(w_in_t16), w_out)
        y_a = _gated_deltanet(proj, i, gdn_conv_w, gdn_lanes, rows(gdn_norm_w))
        y_b = _mamba2_ssd(proj, i, ssm_conv_w, rows(ssm_conv_b), ssm_lanes, rows(ssm_norm_w))
        y_c, y_d = _attention_and_short_conv(proj, i, cos, sin, sink_rows, sc_conv_w)
        xs = _out_projection((y_a, y_b, y_c, y_d), i, w_out_bf16, xs, mod, rows(norm_post_mix))
        if i + 1 < depth:
            xs, mod, w_in_t16 = _mlp(xs, i, rows(norm_pre_mlp), mod, w_up, w_down, rows(norm_post_mlp),
                                     next_layer=adaln + (w_in_t,))
        else:
            xs = _mlp(xs, i, rows(norm_pre_mlp), mod, w_up, w_down, rows(norm_post_mlp))
    return xs.reshape(batch, seq, d)
```

```python
import functools

import jax
import jax.numpy as jnp
from jax import lax
from jax.experimental import pallas as pl
from jax.experimental.pallas import tpu as pltpu

F32 = jnp.float32
BF16 = jnp.bfloat16

D_MODEL = 2048
GW = 512
GDN_HEADS, GDN_HEAD_DIM, GDN_CHUNK = 4, 128, 64
SSM_HEADS, SSM_HEAD_DIM, SSM_STATE, SSM_CHUNK = 8, 64, 128, 128
ATTN_HEAD_DIM, ATTN_Q_HEADS, WINDOW = 64, 8, 128
ROPE_THETA = 10000.0
MLP_HIDDEN = 4 * D_MODEL
EPS = 1e-6

SUBLANES = 8
LANES = 128

_IN_SEGMENTS = (("gq", GW), ("gk", GW), ("gv", GW), ("gz", GW), ("gb", GDN_HEADS), ("ga", GDN_HEADS),
                ("sz", GW), ("sx", GW), ("sbc", 4 * SSM_STATE), ("sdt", SSM_HEADS),
                ("aq", GW), ("ak", 2 * ATTN_HEAD_DIM), ("av", 2 * ATTN_HEAD_DIM),
                ("cb", GW), ("cc", GW), ("ch", GW))
IN_OFFSET = {}
_off = 0
for _name, _size in _IN_SEGMENTS:
    IN_OFFSET[_name] = (_off, _off + _size)
    _off += _size
IN_WIDTH = _off

IN_GROUP = 3 * GW
CONTIGUOUS_GROUPS = (("gq", "gk", "gv"), ("sz", "sx", "sbc"), ("cb", "cc", "ch"))
GATHERED_GROUP = ("gz", "aq", "ak", "av", "gb", "ga", "sdt")
IN_COLS = (len(CONTIGUOUS_GROUPS) + 1) * IN_GROUP
for _grp in CONTIGUOUS_GROUPS:
    assert IN_OFFSET[_grp[0]][0] % SUBLANES == 0
    assert all(IN_OFFSET[a][1] == IN_OFFSET[b][0] for a, b in zip(_grp, _grp[1:]))
(SLAB_GQ, SLAB_GK, SLAB_GV, SLAB_SZ, SLAB_SX, SLAB_SBC, SLAB_CB, SLAB_CC, SLAB_CH, SLAB_GZ, SLAB_AQ) = range(11)
SLAB128_AK, SLAB128_AV, SLAB128_SMALL = 44, 45, 46
LANE_BETA, LANE_ALPHA, LANE_DT = 0, 4, 8

MIB = 1024 * 1024
V7X_VMEM_BYTES = 64 * MIB
VMEM_LIMIT = 56 * MIB


def _cparams(*sem, vmem_limit=VMEM_LIMIT):
    assert vmem_limit < V7X_VMEM_BYTES
    return pltpu.CompilerParams(dimension_semantics=sem, vmem_limit_bytes=vmem_limit)


def _sigmoid(x):
    return jax.nn.sigmoid(x)


def _silu(x):
    return x * jax.nn.sigmoid(x)


def _mm(a, b):
    return jnp.dot(a.astype(BF16), b.astype(BF16), preferred_element_type=F32)


def _mm_nt(a, b):
    return lax.dot_general(a.astype(BF16), b.astype(BF16), (((1,), (1,)), ((), ())),
                           preferred_element_type=F32)


def _mm_tn(a, b):
    return lax.dot_general(a.astype(BF16), b.astype(BF16), (((0,), (0,)), ((), ())),
                           preferred_element_type=F32)


def _blk(idx, size):
    assert size & (size - 1) == 0
    return lax.shift_right_logical(idx, size.bit_length() - 1)


def _split3(x):
    x1 = x.astype(BF16)
    r1 = x - x1.astype(F32)
    x2 = r1.astype(BF16)
    r2 = r1 - x2.astype(F32)
    return x1, x2, r2.astype(BF16)


def _cumsum_matrix(rows, chunk):
    r = lax.broadcasted_iota(jnp.int32, (2 * rows, rows), 0)
    s = lax.broadcasted_iota(jnp.int32, (2 * rows, rows), 1)
    rr = jnp.where(r >= rows, r - rows, r)
    same = _blk(rr, chunk) == _blk(s, chunk)
    sel = same & ((r >= rows) | (s <= rr))
    return jnp.where(sel, 1.0, 0.0).astype(BF16)


def _chunk_cumsum(g, mat):
    rows = g.shape[0]
    acc = None
    for part in _split3(g):
        t = jnp.dot(mat, part, preferred_element_type=F32)
        acc = t if acc is None else acc + t
    return acc[:rows], acc[rows:]


def _causal_conv(x, prev, w, taps):
    rows = x.shape[0]
    xp = jnp.concatenate([prev, x], axis=0)
    acc = x * w[taps - 1:taps]
    for d in range(1, taps):
        acc = acc + xp[SUBLANES - d:SUBLANES - d + rows] * w[taps - 1 - d:taps - d]
    return acc


def _causal_conv_staged(x_ref, prev, w, taps, stage_ref):
    rows = x_ref.shape[0]
    x = x_ref[...]
    stage_ref[0:SUBLANES, :] = prev
    stage_ref[SUBLANES:, :] = x
    acc = x * w[taps - 1:taps]
    for d in range(1, taps):
        acc = acc + stage_ref[pl.ds(SUBLANES - d, rows), :] * w[taps - 1 - d:taps - d]
    return acc


MOD_TN = 1536
MOD_ROWS = 64


def _mod_rope_kernel(c_ref, w_ref, b_ref, pos_ref, invf_ref, sign_ref, o_ref, cos_ref, sin_ref):
    d = c_ref.shape[0]

    def body(r, acc):
        rows = pl.ds(pl.multiple_of(r * MOD_ROWS, MOD_ROWS), MOD_ROWS)
        cc = _silu(c_ref[rows, :])
        prod = w_ref[0, rows, :] * cc
        return acc + jnp.sum(prod.reshape(MOD_ROWS // SUBLANES, SUBLANES, MOD_TN), axis=0)

    acc = lax.fori_loop(0, d // MOD_ROWS, body, jnp.zeros((SUBLANES, MOD_TN), F32))
    o_ref[0] = jnp.sum(acc, axis=0, keepdims=True) + b_ref[0]

    ang = pos_ref[...].astype(F32) * invf_ref[...]
    cos_ref[...] = jnp.cos(ang)
    sin_ref[...] = jnp.sin(ang) * sign_ref[...]


def _modulation_and_rope(c_col, ada_w, ada_b, layer, pos_col):
    depth, d, n = ada_w.shape
    seq = pos_col.shape[0]
    steps = n // MOD_TN
    tb = seq // steps
    assert steps * tb == seq and tb % SUBLANES == 0
    half = ATTN_HEAD_DIM // 2
    inv_freq = ROPE_THETA ** (-jnp.arange(0, ATTN_HEAD_DIM, 2, dtype=F32) / ATTN_HEAD_DIM)
    invf = jnp.tile(inv_freq, LANES // half).reshape(1, LANES)
    lane = jnp.arange(LANES)
    sign = jnp.where((lane % ATTN_HEAD_DIM) < half, -1.0, 1.0).astype(F32).reshape(1, LANES)
    table = jax.ShapeDtypeStruct((seq, LANES), F32)
    return pl.pallas_call(
        _mod_rope_kernel,
        grid=(steps,),
        in_specs=[
            pl.BlockSpec((d, 1), lambda j: (0, 0)),
            pl.BlockSpec((1, d, MOD_TN), lambda j: (layer, 0, j)),
            pl.BlockSpec((1, 1, MOD_TN), lambda j: (layer, 0, j)),
            pl.BlockSpec((tb, 1), lambda j: (j, 0)),
            pl.BlockSpec((1, LANES), lambda j: (0, 0)),
            pl.BlockSpec((1, LANES), lambda j: (0, 0)),
        ],
        out_specs=[pl.BlockSpec((1, 1, MOD_TN), lambda j: (0, 0, j)),
                   pl.BlockSpec((tb, LANES), lambda j: (j, 0)),
                   pl.BlockSpec((tb, LANES), lambda j: (j, 0))],
        out_shape=[jax.ShapeDtypeStruct((1, 1, n), F32), table, table],
        compiler_params=_cparams("parallel"),
        name="adaln_mod_rope",
    )(c_col, ada_w, ada_b, pos_col, invf, sign)


def _mod_spec(k):
    return pl.BlockSpec((None, 1, D_MODEL), lambda *grid_idx: (0, 0, k))


NORM_ROWS = 256


def _prenorm_to(x_ref, nw_ref, sc_ref, sh_ref, h_ref):
    tm = x_ref.shape[0]
    gain = nw_ref[...] * (1.0 + sc_ref[...])
    shift = sh_ref[...]

    def body(r, carry):
        rows = pl.ds(pl.multiple_of(r * NORM_ROWS, NORM_ROWS), NORM_ROWS)
        x = x_ref[rows, :]
        inv = lax.rsqrt(jnp.mean(x * x, axis=-1, keepdims=True) + EPS)
        h_ref[rows, :] = (x * inv * gain + shift).astype(BF16)
        return carry

    lax.fori_loop(0, tm // NORM_ROWS, body, 0)


INPROJ_TM = 1024

MOD_SHIFT_A, MOD_SCALE_A, MOD_GATE_A, MOD_SHIFT_M, MOD_SCALE_M, MOD_GATE_M = range(6)


def _layer_spec(shape, layer, *rest):
    rest = rest or (0,) * len(shape)

    def index_map(*grid_idx):
        return (layer,) + tuple(r(*grid_idx) if callable(r) else r for r in rest)

    return pl.BlockSpec((None,) + tuple(shape), index_map)


def _inproj_kernel(x_ref, nw_ref, sc_ref, sh_ref, wt_ref, tail_ref, wo_ref, o_ref, wo16_ref, h_ref):
    j = pl.program_id(1)
    wo16_ref[...] = wo_ref[...].astype(BF16)

    @pl.when(j == 0)
    def _():
        _prenorm_to(x_ref, nw_ref, sc_ref, sh_ref, h_ref)

    def project(w_ref):
        o_ref[...] = _mm_nt(h_ref[...], w_ref[...])

    @pl.when(j < len(CONTIGUOUS_GROUPS))
    def _():
        project(wt_ref)

    @pl.when(j == len(CONTIGUOUS_GROUPS))
    def _():
        project(tail_ref)


def _group_row(j):
    starts = [IN_OFFSET[grp[0]][0] // SUBLANES for grp in CONTIGUOUS_GROUPS]
    row8 = starts[-1]
    for k in range(len(starts) - 2, -1, -1):
        row8 = jnp.where(j <= k, starts[k], row8)
    return row8 * SUBLANES


def _in_projection(x, layer, norm_w, mod, w_t, w_gathered, w_out):
    seq, d = x.shape
    tm = min(INPROJ_TM, seq)
    n_groups = IN_COLS // IN_GROUP
    wo_rows = w_out.shape[1] // ((seq // tm) * n_groups)
    assert wo_rows * (seq // tm) * n_groups == w_out.shape[1] and wo_rows % (2 * SUBLANES) == 0
    return pl.pallas_call(
        _inproj_kernel,
        grid=(seq // tm, n_groups),
        in_specs=[
            pl.BlockSpec((tm, d), lambda i, j: (i, 0)),
            _layer_spec((1, d), layer),
            _mod_spec(MOD_SCALE_A),
            _mod_spec(MOD_SHIFT_A),
            pl.BlockSpec((pl.Element(IN_GROUP), pl.Element(d)), lambda i, j: (_group_row(j), 0)),
            pl.BlockSpec((IN_GROUP, d), lambda i, j: (0, 0), pipeline_mode=pl.Buffered(1)),
            pl.BlockSpec((None, wo_rows, d), lambda i, j: (layer, i * n_groups + j, 0)),
        ],
        out_specs=[pl.BlockSpec((tm, IN_GROUP), lambda i, j: (i, j)),
                   pl.BlockSpec((wo_rows, d), lambda i, j: (i * n_groups + j, 0))],
        out_shape=[jax.ShapeDtypeStruct((seq, IN_COLS), F32), jax.ShapeDtypeStruct(w_out.shape[1:], BF16)],
        scratch_shapes=[pltpu.VMEM((tm, d), BF16)],
        compiler_params=_cparams("arbitrary", "arbitrary"),
        name="in_proj",
    )(x, norm_w, mod, mod, w_t, w_gathered, w_out)


GDN_TB = 512


def _l2norm(x):
    return x * lax.rsqrt(jnp.sum(x * x, axis=-1, keepdims=True) + EPS)


GDN_BLK = 128
GDN_PREP_STAGES_PER_CHUNK = 2


def _unit_lower_inverse_minus_identity(ms, r, s):
    same8 = _blk(r, 8) == _blk(s, 8)
    same16 = _blk(r, 16) == _blk(s, 16)
    same32 = _blk(r, 32) == _blk(s, 32)
    n1 = [jnp.where(same8, -m, 0.0) for m in ms]
    n2 = [_mm(a, a) for a in n1]
    yield
    n3 = [_mm(a, b) for a, b in zip(n1, n2)]
    n4 = [_mm(b, b) for b in n2]
    yield
    ys = [a + b + c for a, b, c in zip(n1, n2, n3)]
    ts = [_mm(y, d) for y, d in zip(ys, n4)]
    yield
    ys = [y + d + t for y, d, t in zip(ys, n4, ts)]
    for sel in (same16 & ~same8, same32 & ~same16, ~same32):
        blks = [jnp.where(sel, m, 0.0) for m in ms]
        cs = [b + _mm(y, b) for y, b in zip(ys, blks)]
        yield
        ts = [_mm(c, y) for c, y in zip(cs, ys)]
        yield
        ys = [y - c - t for y, c, t in zip(ys, cs, ts)]
    return ys


def _run_interleaved(main, side, main_stages_per_side_stage):
    side_live = True
    done = 0
    while True:
        try:
            next(main)
        except StopIteration as stop:
            result = stop.value
            break
        done += 1
        if side_live and done % main_stages_per_side_stage == 0:
            side_live = next(side, StopIteration) is not StopIteration
    while side_live:
        side_live = next(side, StopIteration) is not StopIteration
    return result


def _gdn_prepare(q_ref, k_ref, v_ref, qp_ref, kp_ref, vp_ref, sm_ref, cw_ref, lp_ref, cum_ref, conv_ref, first_block):
    tb = q_ref.shape[0]
    hd = GDN_HEAD_DIM
    nblk = tb // GDN_BLK

    keep_prev = jnp.where(first_block, 0.0, 1.0)
    cw = cw_ref[...]

    def conv_silu(x_ref, p_ref, idx):
        w = cw[:, idx * GW:(idx + 1) * GW]
        return _silu(_causal_conv_staged(x_ref, p_ref[...] * keep_prev, w, 4, conv_ref.at[idx]))

    q = conv_silu(q_ref, qp_ref, 0)
    yield
    k = conv_silu(k_ref, kp_ref, 1)
    yield
    v = conv_silu(v_ref, vp_ref, 2)
    yield

    small = sm_ref[...]
    a_log = lp_ref[0:1, :]
    dt_bias = lp_ref[1:2, :]
    beta_all = _sigmoid(small)
    g_all = -jnp.exp(a_log) * jax.nn.softplus(small + dt_bias)
    gc_all, gtot_all = _chunk_cumsum(g_all, cum_ref[...])
    gc_t = gc_all.T
    yield

    r = lax.broadcasted_iota(jnp.int32, (GDN_BLK, GDN_BLK), 0)
    s = lax.broadcasted_iota(jnp.int32, (GDN_BLK, GDN_BLK), 1)
    same_chunk = _blk(r, GDN_CHUNK) == _blk(s, GDN_CHUNK)
    mask_incl = same_chunk & (s <= r)
    mask_strict = same_chunk & (s < r)

    problems = [(slice(b * GDN_BLK, (b + 1) * GDN_BLK), h) for b in range(nblk) for h in range(GDN_HEADS)]
    head_cols = lambda h: slice(h * hd, (h + 1) * hd)
    gate = lambda arr, rows, lane: arr[rows, lane:lane + 1]

    qn = [_l2norm(q[rows, head_cols(h)]) * (hd ** -0.5) for rows, h in problems]
    kn = [_l2norm(k[rows, head_cols(h)]) for rows, h in problems]
    beta = [gate(beta_all, rows, LANE_BETA + h) for rows, h in problems]
    gcol = [gate(gc_all, rows, LANE_ALPHA + h) for rows, h in problems]
    gtot = [gate(gtot_all, rows, LANE_ALPHA + h) for rows, h in problems]
    grow = [gc_t[LANE_ALPHA + h:LANE_ALPHA + h + 1, rows] for rows, h in problems]
    yield

    decay = [jnp.exp(jnp.where(mask_incl, gc - gr, -jnp.inf)) for gc, gr in zip(gcol, grow)]
    k_beta = [kk * b for kk, b in zip(kn, beta)]
    k16 = [kk.astype(BF16) for kk in kn]
    yield
    ms = [jnp.where(mask_strict, _mm_nt(kb, kk) * d, 0.0) for kb, kk, d in zip(k_beta, k16, decay)]
    yield
    attn = [(_mm_nt(qq, kk) * d).astype(BF16) for qq, kk, d in zip(qn, k16, decay)]
    ys = yield from _unit_lower_inverse_minus_identity(ms, r, s)
    yield

    e_gc = [jnp.exp(gc) for gc in gcol]
    rhs = [jnp.concatenate([v[rows, head_cols(h)] * b, kb * e], axis=1)
           for (rows, h), b, kb, e in zip(problems, beta, k_beta, e_gc)]
    uw = [x + _mm(y, x) for y, x in zip(ys, rhs)]
    yield
    u = [x[:, :hd] for x in uw]
    w = [x[:, hd:].astype(BF16) for x in uw]
    q_dec = [(qq * e).astype(BF16) for qq, e in zip(qn, e_gc)]
    k_dec = [(kk * jnp.exp(gt - gc)).astype(BF16) for kk, gt, gc in zip(kn, gtot, gcol)]
    return u, w, q_dec, k_dec, attn, gtot_all


def _gdn_recurrence(u, w, q_dec, k_dec, attn, gtot_all, state_ref, z, nw_ref, o_ref):
    hd = GDN_HEAD_DIM
    nblk = o_ref.shape[0] // GDN_BLK
    chunks_per_blk = GDN_BLK // GDN_CHUNK
    head_cols = lambda h: slice(h * hd, (h + 1) * hd)

    states = [state_ref[h] for h in range(GDN_HEADS)]
    v_new = [[] for _ in range(nblk * GDN_HEADS)]
    o_inter = [[] for _ in range(nblk * GDN_HEADS)]
    for b in range(nblk):
        for c in range(chunks_per_blk):
            rows = slice(c * GDN_CHUNK, (c + 1) * GDN_CHUNK)
            first = b * GDN_BLK + c * GDN_CHUNK
            for h in range(GDN_HEADS):
                p = b * GDN_HEADS + h
                st16 = states[h].astype(BF16)
                vn = u(p, rows) - jnp.dot(w(p, rows), st16, preferred_element_type=F32)
                o_inter[p].append(jnp.dot(q_dec(p, rows), st16, preferred_element_type=F32))
                chunk_dec = jnp.exp(gtot_all[first:first + 1, LANE_ALPHA + h:LANE_ALPHA + h + 1])
                states[h] = states[h] * chunk_dec + _mm_tn(k_dec(p, rows), vn)
                v_new[p].append(vn)
            yield
        rows = slice(b * GDN_BLK, (b + 1) * GDN_BLK)
        for h in range(GDN_HEADS):
            p = b * GDN_HEADS + h
            o = jnp.concatenate(o_inter[p], axis=0) + _mm(attn(p, slice(None)), jnp.concatenate(v_new[p], axis=0))
            o = o * lax.rsqrt(jnp.mean(o * o, axis=-1, keepdims=True) + EPS) * nw_ref[...]
            o_ref[rows, head_cols(h)] = (o * _silu(z[rows, head_cols(h)])).astype(o_ref.dtype)
    for h in range(GDN_HEADS):
        state_ref[h] = states[h]


def _gdn_kernel(q_ref, k_ref, v_ref, z_ref, qp_ref, kp_ref, vp_ref, sm_ref, cw_ref, lp_ref, nw_ref, cum_ref,
                o_ref, state_ref, conv_ref, u_ref, w_ref, qd_ref, kd_ref, at_ref, gt_ref):
    step = pl.program_id(0)
    write_slot = step % 2
    read_slot = 1 - write_slot
    n_problems = u_ref.shape[1]

    @pl.when(step == 0)
    def _():
        state_ref[...] = jnp.zeros_like(state_ref)
        for ref in (u_ref, w_ref, qd_ref, kd_ref, at_ref, gt_ref):
            ref[1] = jnp.zeros(ref.shape[1:], ref.dtype)

    recurrence = _gdn_recurrence(*[(lambda p, rows, ref=ref: ref[read_slot, p, rows])
                                   for ref in (u_ref, w_ref, qd_ref, kd_ref, at_ref)],
                                 gt_ref[read_slot], state_ref, z_ref[...], nw_ref, o_ref)
    prepare = _gdn_prepare(q_ref, k_ref, v_ref, qp_ref, kp_ref, vp_ref, sm_ref, cw_ref, lp_ref, cum_ref, conv_ref,
                           step == 0)
    u, w, q_dec, k_dec, attn, gtot_all = _run_interleaved(prepare, recurrence, GDN_PREP_STAGES_PER_CHUNK)
    for p in range(n_problems):
        u_ref[write_slot, p] = u[p]
        w_ref[write_slot, p] = w[p]
        qd_ref[write_slot, p] = q_dec[p]
        kd_ref[write_slot, p] = k_dec[p]
        at_ref[write_slot, p] = attn[p]
    gt_ref[write_slot] = gtot_all


def _prev_rows_spec(tb, width, slab):
    blocks = tb // SUBLANES
    return pl.BlockSpec((SUBLANES, width), lambda i: (jnp.maximum(i * blocks - 1, 0), slab))


def _lane_params(lane0, *rows):
    stacked = jnp.stack([r.astype(F32) for r in rows], axis=1)
    nrows, heads = stacked.shape[1:]
    return jnp.pad(stacked, ((0, 0), (0, SUBLANES - nrows), (lane0, LANES - lane0 - heads)))


def _gated_deltanet(proj, layer, conv_w, lane_params, norm_w):
    seq = proj.shape[0]
    tb = min(GDN_TB, seq)
    nsteps = seq // tb
    n_problems = (tb // GDN_BLK) * GDN_HEADS
    prep = lambda t: jnp.minimum(t, nsteps - 1)
    done = lambda t: jnp.maximum(t - 1, 0)
    slab = lambda j: pl.BlockSpec((tb, GW), lambda t: (prep(t), j))
    prev = lambda j: pl.BlockSpec((SUBLANES, GW),
                                  lambda t: (jnp.maximum(prep(t) * (tb // SUBLANES) - 1, 0), j))
    full = lambda shape: _layer_spec(shape, layer)
    factor = lambda dtype: pltpu.VMEM((2, n_problems, GDN_BLK, GDN_HEAD_DIM), dtype)
    return pl.pallas_call(
        _gdn_kernel,
        grid=(nsteps + 1,),
        in_specs=[
            slab(SLAB_GQ), slab(SLAB_GK), slab(SLAB_GV),
            pl.BlockSpec((tb, GW), lambda t: (done(t), SLAB_GZ)),
            prev(SLAB_GQ), prev(SLAB_GK), prev(SLAB_GV),
            pl.BlockSpec((tb, LANES), lambda t: (prep(t), SLAB128_SMALL)),
            full((4, 3 * GW)), full((SUBLANES, LANES)), full((1, GDN_HEAD_DIM)),
            pl.BlockSpec((2 * tb, tb), lambda t: (0, 0)),
        ],
        out_specs=pl.BlockSpec((tb, GW), lambda t: (done(t), 0)),
        out_shape=jax.ShapeDtypeStruct((seq, GW), BF16),
        scratch_shapes=[pltpu.VMEM((GDN_HEADS, GDN_HEAD_DIM, GDN_HEAD_DIM), F32),
                        pltpu.VMEM((3, tb + SUBLANES, GW), F32),
                        factor(F32), factor(BF16), factor(BF16), factor(BF16), factor(BF16),
                        pltpu.VMEM((2, tb, LANES), F32)],
        compiler_params=_cparams("arbitrary"),
        name="gated_deltanet",
    )(proj, proj, proj, proj, proj, proj, proj, proj, conv_w, lane_params, norm_w, _cumsum_matrix(tb, GDN_CHUNK))


SSD_SUB = 256


def _ssd_block(first_block, z_ref, x_ref, bc_ref, xp_ref, bcp_ref, sm_ref, cw_ref, cb_ref, lp_ref, nw_ref, cum_ref,
               o_ref, state_ref):
    keep_prev = jnp.where(first_block, 0.0, 1.0)
    for sub in range(z_ref.shape[0] // SSD_SUB):
        r0 = sub * SSD_SUB
        if sub == 0:
            x_prev, bc_prev = xp_ref[...] * keep_prev, bcp_ref[...] * keep_prev
        else:
            x_prev, bc_prev = x_ref[r0 - SUBLANES:r0, :], bc_ref[r0 - SUBLANES:r0, :]
        _ssd_sub_block(r0, x_prev, bc_prev, z_ref, x_ref, bc_ref, sm_ref, cw_ref, cb_ref, lp_ref, nw_ref, cum_ref,
                       o_ref, state_ref)


def _ssd_sub_block(r0, x_prev, bc_prev, z_ref, x_ref, bc_ref, sm_ref, cw_ref, cb_ref, lp_ref, nw_ref, cum_ref,
                   o_ref, state_ref):
    tb = SSD_SUB
    block = slice(r0, r0 + tb)
    nstate = SSM_STATE
    chunk = SSM_CHUNK
    nchunks = tb // chunk
    n_pairs = SSM_HEADS // 2
    pairs_per_group = n_pairs // 2

    cw = cw_ref[...]
    cbias = cb_ref[...]
    xs = _silu(_causal_conv(x_ref[block, :], x_prev, cw[:, :GW], 4) + cbias[:, :GW])
    bc = _silu(_causal_conv(bc_ref[block, :], bc_prev, cw[:, GW:], 4) + cbias[:, GW:])
    z = z_ref[block, :]

    small = sm_ref[block, :]
    a_neg = -jnp.exp(lp_ref[0:1, :])
    dt_all = jax.nn.softplus(small + lp_ref[1:2, :])
    dskip_all = lp_ref[2:3, :]
    gc_all, gtot_all = _chunk_cumsum(dt_all * a_neg, cum_ref[...])
    gc_t = gc_all.T

    r = lax.broadcasted_iota(jnp.int32, (chunk, chunk), 0)
    s = lax.broadcasted_iota(jnp.int32, (chunk, chunk), 1)
    mask_incl = s <= r
    lane = lax.broadcasted_iota(jnp.int32, (1, LANES), 1)
    first_half = lane < SSM_HEAD_DIM
    by_half = lambda a, b: jnp.where(first_half, a, b)

    problems = [(c, p) for c in range(nchunks) for p in range(n_pairs)]
    rows_of = lambda c: slice(c * chunk, (c + 1) * chunk)
    cols_of = lambda p: slice(p * LANES, (p + 1) * LANES)
    lanes_of = lambda p: (LANE_DT + 2 * p, LANE_DT + 2 * p + 1)
    col = lambda arr, c, l: arr[rows_of(c), l:l + 1]

    b_mat = {(c, g): bc[rows_of(c), g * nstate:(g + 1) * nstate].astype(BF16)
             for c in range(nchunks) for g in range(2)}
    c_mat = {(c, g): bc[rows_of(c), (2 + g) * nstate:(3 + g) * nstate].astype(BF16)
             for c in range(nchunks) for g in range(2)}
    cb = {key: _mm_nt(c_mat[key], b_mat[key]) for key in b_mat}

    x_pair = [xs[rows_of(c), cols_of(p)] for c, p in problems]
    xdt = [x * by_half(col(dt_all, c, lanes_of(p)[0]), col(dt_all, c, lanes_of(p)[1]))
           for x, (c, p) in zip(x_pair, problems)]
    y_diag = []
    for (c, p), xd in zip(problems, xdt):
        halves = (by_half(xd, 0.0), by_half(0.0, xd))
        acc = None
        for l, half in zip(lanes_of(p), halves):
            grow = gc_t[l:l + 1, rows_of(c)]
            lmat = jnp.exp(jnp.where(mask_incl, col(gc_all, c, l) - grow, -jnp.inf))
            t = _mm(cb[c, p // pairs_per_group] * lmat, half)
            acc = t if acc is None else acc + t
        y_diag.append(acc)

    e_in = [by_half(jnp.exp(col(gc_all, c, lanes_of(p)[0])), jnp.exp(col(gc_all, c, lanes_of(p)[1])))
            for c, p in problems]
    e_out = [by_half(jnp.exp(col(gtot_all, c, lanes_of(p)[0]) - col(gc_all, c, lanes_of(p)[0])),
                     jnp.exp(col(gtot_all, c, lanes_of(p)[1]) - col(gc_all, c, lanes_of(p)[1])))
             for c, p in problems]
    states = [state_ref[p] for p in range(n_pairs)]
    y_off = []
    for idx, (c, p) in enumerate(problems):
        la, lb = lanes_of(p)
        g = p // pairs_per_group
        y_off.append(e_in[idx] * _mm(c_mat[c, g], states[p]))
        first = c * chunk
        cd_lane = by_half(jnp.exp(gtot_all[first:first + 1, la:la + 1]),
                          jnp.exp(gtot_all[first:first + 1, lb:lb + 1]))
        states[p] = states[p] * cd_lane + _mm_tn(b_mat[c, g], xdt[idx] * e_out[idx])
    for p in range(n_pairs):
        state_ref[p] = states[p]

    ys = []
    for idx, (c, p) in enumerate(problems):
        la, lb = lanes_of(p)
        y = y_diag[idx] + y_off[idx] + x_pair[idx] * by_half(dskip_all[:, la:la + 1], dskip_all[:, lb:lb + 1])
        ys.append(y * _silu(z[rows_of(c), cols_of(p)]))
    sumsq = [jnp.sum(y * y, axis=-1, keepdims=True) for y in ys]
    for c in range(nchunks):
        for g in range(2):
            members = [c * n_pairs + g * pairs_per_group + k for k in range(pairs_per_group)]
            total = sumsq[members[0]]
            for idx in members[1:]:
                total = total + sumsq[idx]
            inv = lax.rsqrt(total / (pairs_per_group * LANES) + EPS)
            for idx in members:
                cols = cols_of(problems[idx][1])
                out_rows = slice(r0 + c * chunk, r0 + (c + 1) * chunk)
                o_ref[out_rows, cols] = (ys[idx] * inv * nw_ref[:, cols]).astype(o_ref.dtype)


def _rotate_half(x, lane_in_head_low):
    width = x.shape[1]
    half = ATTN_HEAD_DIM // 2
    ahead = pltpu.roll(x, width - half, axis=1)
    behind = pltpu.roll(x, half, axis=1)
    return jnp.where(lane_in_head_low, ahead, behind)


SWA_TB = 512


def _swa_kernel(q_ref, k_ref, v_ref, cos_ref, sin_ref, sink_ref, cb_ref, cc_ref, ch_ref, ccp_ref, chp_ref, cw_ref,
                sz_ref, sx_ref, sbc_ref, sxp_ref, sbcp_ref, sm_ref, scw_ref, scb_ref, slp_ref, snw_ref, cum_ref,
                o_ref, od_ref, ob_ref, kprev_ref, vprev_ref, ssm_state_ref):
    step = pl.program_id(0)
    tb = q_ref.shape[0]
    win = WINDOW
    hd = ATTN_HEAD_DIM

    @pl.when(step == 0)
    def _():
        kprev_ref[...] = jnp.zeros_like(kprev_ref)
        vprev_ref[...] = jnp.zeros_like(vprev_ref)
        ssm_state_ref[...] = jnp.zeros_like(ssm_state_ref)

    _ssd_block(step == 0, sz_ref, sx_ref, sbc_ref, sxp_ref, sbcp_ref, sm_ref, scw_ref, scb_ref, slp_ref, snw_ref,
               cum_ref, ob_ref, ssm_state_ref)

    keep_prev = jnp.where(step == 0, 0.0, 1.0)
    c_h = cc_ref[...] * ch_ref[...]
    c_h_prev = ccp_ref[...] * chp_ref[...] * keep_prev
    od_ref[...] = (cb_ref[...] * _causal_conv(c_h, c_h_prev, cw_ref[...], 3)).astype(od_ref.dtype)

    cos = cos_ref[...]
    sin = sin_ref[...]
    lane = lax.broadcasted_iota(jnp.int32, (1, LANES), 1)
    low = (lane & (hd - 1)) < (hd // 2)
    first_half = lane < hd

    k_cur = k_ref[...]
    k_cur = k_cur * cos + _rotate_half(k_cur, low) * sin
    v_cur = v_ref[...]
    k_all = jnp.concatenate([kprev_ref[...], k_cur], axis=0)
    v_all = jnp.concatenate([vprev_ref[...], v_cur], axis=0)
    k_swap = pltpu.roll(k_all, hd, axis=1)
    v_swap = pltpu.roll(v_all, hd, axis=1)

    def placed(x, x_swap, kv_head, pos):
        src = x if kv_head == pos else x_swap
        keep = first_half if pos == 0 else ~first_half
        return jnp.where(keep, src, 0.0).astype(BF16)

    qi = lax.broadcasted_iota(jnp.int32, (win, 2 * win), 0)
    kj = lax.broadcasted_iota(jnp.int32, (win, 2 * win), 1)
    rel = qi + win - kj
    band = (rel >= 0) & (rel < win)
    band_first = band & ((kj >= win) | (step > 0))

    problems = [(j, h) for j in range(tb // win) for h in range(ATTN_Q_HEADS)]
    kv_of = lambda head: head // (ATTN_Q_HEADS // 2)
    pair_cols = lambda pair: slice(pair * LANES, (pair + 1) * LANES)
    k_at = {(g, pos): placed(k_all, k_swap, g, pos) for g in range(2) for pos in range(2)}
    v_at = {(g, pos): placed(v_all, v_swap, g, pos) for g in range(2) for pos in range(2)}
    q_pairs = []
    for pair in range(ATTN_Q_HEADS // 2):
        q_pair = q_ref[:, pair_cols(pair)]
        q_pairs.append((q_pair * cos + _rotate_half(q_pair, low) * sin).astype(BF16))
    sinks = [sink_ref[h:h + 1, 0:1] for _, h in problems]
    sc = [jnp.where(band_first if j == 0 else band,
                    _mm_nt(q_pairs[h // 2][j * win:(j + 1) * win],
                           k_at[kv_of(h), h % 2][j * win:(j + 2) * win]) * (hd ** -0.5),
                    -jnp.inf)
          for j, h in problems]
    mx = [jnp.maximum(jnp.max(s_, axis=-1, keepdims=True), sk) for s_, sk in zip(sc, sinks)]
    p = [jnp.exp(s_ - m_) for s_, m_ in zip(sc, mx)]
    denom = [jnp.sum(p_, axis=-1, keepdims=True) + jnp.exp(sk - m_) for p_, sk, m_ in zip(p, sinks, mx)]
    o = [_mm(p_, v_at[kv_of(h), h % 2][j * win:(j + 2) * win]) / d_
         for (j, h), p_, d_ in zip(problems, p, denom)]
    for idx in range(0, len(problems), 2):
        j, h = problems[idx]
        o_ref[j * win:(j + 1) * win, pair_cols(h // 2)] = (o[idx] + o[idx + 1]).astype(o_ref.dtype)

    kprev_ref[...] = k_cur[tb - win:]
    vprev_ref[...] = v_cur[tb - win:]


def _ssd_attention_short_conv(proj, layer, cos, sin, sink_rows, sc_conv_w, ssm_conv_w, ssm_conv_b, ssm_lanes,
                              ssm_norm_w):
    seq = proj.shape[0]
    tb = min(SWA_TB, seq)
    assert tb % SSD_SUB == 0
    slab = lambda j: pl.BlockSpec((tb, GW), lambda i: (i, j))
    full = lambda shape: _layer_spec(shape, layer)
    out = jax.ShapeDtypeStruct((seq, GW), BF16)
    return pl.pallas_call(
        _swa_kernel,
        grid=(seq // tb,),
        in_specs=[
            slab(SLAB_AQ),
            pl.BlockSpec((tb, LANES), lambda i: (i, SLAB128_AK)),
            pl.BlockSpec((tb, LANES), lambda i: (i, SLAB128_AV)),
            pl.BlockSpec((tb, LANES), lambda i: (i, 0)),
            pl.BlockSpec((tb, LANES), lambda i: (i, 0)),
            _layer_spec((ATTN_Q_HEADS, LANES), layer),
            slab(SLAB_CB), slab(SLAB_CC), slab(SLAB_CH),
            _prev_rows_spec(tb, GW, SLAB_CC), _prev_rows_spec(tb, GW, SLAB_CH),
            _layer_spec((3, GW), layer),
            slab(SLAB_SZ), slab(SLAB_SX), slab(SLAB_SBC),
            _prev_rows_spec(tb, GW, SLAB_SX), _prev_rows_spec(tb, GW, SLAB_SBC),
            pl.BlockSpec((tb, LANES), lambda i: (i, SLAB128_SMALL)),
            full((4, 2 * GW)), full((1, 2 * GW)), full((SUBLANES, LANES)), full((1, GW)),
            pl.BlockSpec((2 * SSD_SUB, SSD_SUB), lambda i: (0, 0)),
        ],
        out_specs=[pl.BlockSpec((tb, GW), lambda i: (i, 0))] * 3,
        out_shape=[out, out, out],
        scratch_shapes=[pltpu.VMEM((WINDOW, LANES), F32), pltpu.VMEM((WINDOW, LANES), F32),
                        pltpu.VMEM((SSM_HEADS // 2, SSM_STATE, LANES), F32)],
        compiler_params=_cparams("arbitrary"),
        name="ssd_swa_sconv",
    )(proj, proj, proj, cos, sin, sink_rows, proj, proj, proj, proj, proj, sc_conv_w,
      proj, proj, proj, proj, proj, proj, ssm_conv_w, ssm_conv_b, ssm_lanes, ssm_norm_w,
      _cumsum_matrix(SSD_SUB, SSM_CHUNK))


OUTPROJ_TM = 512


def _outproj_kernel(ya_ref, yb_ref, yc_ref, yd_ref, w_ref, x_ref, gate_ref, nw_ref, o_ref):
    y = jnp.concatenate([ya_ref[...], yb_ref[...], yc_ref[...], yd_ref[...]], axis=1)
    y = jnp.dot(y, w_ref[...], preferred_element_type=F32)
    inv = lax.rsqrt(jnp.mean(y * y, axis=-1, keepdims=True) + EPS)
    o_ref[...] = x_ref[...] + y * inv * (gate_ref[...] * nw_ref[...])


def _out_projection(ys, layer, w_bf16, x, mod, norm_w):
    seq, d = x.shape
    tm = min(OUTPROJ_TM, seq)
    return pl.pallas_call(
        _outproj_kernel,
        grid=(seq // tm,),
        in_specs=[pl.BlockSpec((tm, GW), lambda i: (i, 0))] * 4 + [
            pl.BlockSpec((4 * GW, d), lambda i: (0, 0)),
            pl.BlockSpec((tm, d), lambda i: (i, 0)),
            _mod_spec(MOD_GATE_A),
            _layer_spec((1, d), layer),
        ],
        out_specs=pl.BlockSpec((tm, d), lambda i: (i, 0)),
        out_shape=jax.ShapeDtypeStruct((seq, d), F32),
        compiler_params=_cparams("parallel"),
        name="out_proj",
    )(*ys, w_bf16, x, mod, norm_w)


MLP_TM = 1024
MLP_TH = 512
NEXT_WIN_ROWS = 48


def _mlp_kernel(*refs, next_layer):
    if next_layer:
        (x_ref, nw_ref, sc_ref, sh_ref, wu_ref, wd_ref, gate_ref, pw_ref, c_ref, aw_ref, ab_ref, wt_ref,
         o_ref, mod_ref, wt16_ref, h_ref) = refs
    else:
        x_ref, nw_ref, sc_ref, sh_ref, wu_ref, wd_ref, gate_ref, pw_ref, o_ref, h_ref = refs
    j = pl.program_id(1)

    @pl.when(j == 0)
    def _():
        _prenorm_to(x_ref, nw_ref, sc_ref, sh_ref, h_ref)
        o_ref[...] = jnp.zeros_like(o_ref)

    hid = _mm(h_ref[...], wu_ref[...])
    hid = jnp.square(jnp.maximum(hid, 0.0))
    o_ref[...] += _mm(hid, wd_ref[...])

    if next_layer:
        prod = aw_ref[...] * _silu(c_ref[...])
        part = jnp.sum(prod.reshape(prod.shape[0] // SUBLANES, SUBLANES, LANES), axis=0)
        mod_ref[...] = jnp.sum(part, axis=0, keepdims=True) + ab_ref[...]
        wt16_ref[...] = wt_ref[...].astype(BF16)

    @pl.when(j == pl.num_programs(1) - 1)
    def _():
        gain = gate_ref[...] * pw_ref[...]

        def body(r, carry):
            rows = pl.ds(pl.multiple_of(r * NORM_ROWS, NORM_ROWS), NORM_ROWS)
            y = o_ref[rows, :]
            inv = lax.rsqrt(jnp.mean(y * y, axis=-1, keepdims=True) + EPS)
            o_ref[rows, :] = x_ref[rows, :] + y * inv * gain
            return carry

        lax.fori_loop(0, x_ref.shape[0] // NORM_ROWS, body, 0)


def _mlp(x, layer, norm_w, mod, w_up, w_down, post_w, next_layer=None):
    seq, d = x.shape
    hidden = w_up.shape[2]
    tm = min(MLP_TM, seq)
    steps_per_tile = hidden // MLP_TH
    vmem = (2 * 2 * tm * d * 4 + tm * d * 2 + 2 * 2 * d * MLP_TH * w_up.dtype.itemsize
            + d * MLP_TH * 2 + tm * MLP_TH * 6)
    in_specs = [
        pl.BlockSpec((tm, d), lambda i, j: (i, 0)),
        _layer_spec((1, d), layer),
        _mod_spec(MOD_SCALE_M),
        _mod_spec(MOD_SHIFT_M),
        _layer_spec((d, MLP_TH), layer, 0, lambda i, j: j),
        _layer_spec((MLP_TH, d), layer, lambda i, j: j, 0),
        _mod_spec(MOD_GATE_M),
        _layer_spec((1, d), layer),
    ]
    out_specs = [pl.BlockSpec((tm, d), lambda i, j: (i, 0))]
    out_shape = [jax.ShapeDtypeStruct((seq, d), F32)]
    args = [x, norm_w, mod, mod, w_up, w_down, mod, post_w]
    if next_layer:
        c_col, ada_w, ada_b, w_in_t = next_layer
        n = ada_w.shape[2]
        n_steps = (seq // tm) * steps_per_tile
        wt_rows = NEXT_WIN_ROWS
        assert n_steps >= n // LANES and n_steps * wt_rows >= w_in_t.shape[1] and w_in_t.shape[1] % wt_rows == 0
        col = lambda i, j: jnp.minimum(i * steps_per_tile + j, n // LANES - 1)
        wt_blk = lambda i, j: jnp.minimum(i * steps_per_tile + j, w_in_t.shape[1] // wt_rows - 1)
        in_specs += [pl.BlockSpec((d, 1), lambda i, j: (0, 0)),
                     pl.BlockSpec((None, d, LANES), lambda i, j: (layer + 1, 0, col(i, j))),
                     pl.BlockSpec((None, 1, LANES), lambda i, j: (layer + 1, 0, col(i, j))),
                     pl.BlockSpec((None, wt_rows, d), lambda i, j: (layer + 1, wt_blk(i, j), 0))]
        out_specs += [pl.BlockSpec((None, 1, LANES), lambda i, j: (0, 0, col(i, j))),
                      pl.BlockSpec((wt_rows, d), lambda i, j: (wt_blk(i, j), 0))]
        out_shape += [jax.ShapeDtypeStruct((1, 1, n), F32), jax.ShapeDtypeStruct(w_in_t.shape[1:], BF16)]
        args += [c_col, ada_w, ada_b, w_in_t]
        vmem += 2 * 2 * d * LANES * 4 + 2 * wt_rows * d * 6
    outs = pl.pallas_call(
        functools.partial(_mlp_kernel, next_layer=bool(next_layer)),
        grid=(seq // tm, steps_per_tile),
        in_specs=in_specs,
        out_specs=out_specs,
        out_shape=out_shape,
        scratch_shapes=[pltpu.VMEM((tm, d), BF16)],
        compiler_params=_cparams("arbitrary", "arbitrary", vmem_limit=vmem),
        name="mlp",
    )(*args)
    return outs if next_layer else outs[0]


def _gathered_group(w_t):
    parts = [w_t[IN_OFFSET[name][0]:IN_OFFSET[name][1]] for name in GATHERED_GROUP]
    used = sum(p.shape[0] for p in parts)
    parts.append(jnp.zeros((IN_GROUP - used, w_t.shape[1]), w_t.dtype))
    return jnp.concatenate(parts, axis=0)


def kernel(x, c, positions, ada_w, ada_b, norm_pre_mix, norm_post_mix, norm_pre_mlp, norm_post_mlp,
           w_in, w_out, gdn_conv_w, gdn_a_log, gdn_dt_bias, gdn_norm_w, ssm_conv_w, ssm_conv_b,
           ssm_a_log, ssm_dt_bias, ssm_d, ssm_norm_w, attn_sinks, sc_conv_w, w_up, w_down):
    batch, seq, d = x.shape
    assert batch == 1 and d == D_MODEL
    depth = ada_w.shape[0]
    xs = x.reshape(seq, d)

    adaln = (c.reshape(d, 1), ada_w, ada_b.reshape(depth, 1, -1))
    mod, cos, sin = _modulation_and_rope(*adaln, 0, positions.reshape(seq, 1))

    rows = lambda v: v.reshape(depth, 1, -1)
    w_in_t = jnp.swapaxes(w_in, 1, 2)
    w_in_t16 = w_in_t[0].astype(BF16)
    gdn_lanes = _lane_params(LANE_ALPHA, gdn_a_log, gdn_dt_bias)
    ssm_lanes = _lane_params(LANE_DT, ssm_a_log, ssm_dt_bias, ssm_d)
    sink_rows = jnp.broadcast_to(attn_sinks.astype(F32)[:, :, None], (depth, ATTN_Q_HEADS, LANES))

    for i in range(depth):
        proj, w_out_bf16 = _in_projection(xs, i, rows(norm_pre_mix), mod, w_in_t16, _gathered_group(w_in_t16), w_out)
        y_a = _gated_deltanet(proj, i, gdn_conv_w, gdn_lanes, rows(gdn_norm_w))
        y_c, y_d, y_b = _ssd_attention_short_conv(proj, i, cos, sin, sink_rows, sc_conv_w,
                                                  ssm_conv_w, rows(ssm_conv_b), ssm_lanes, rows(ssm_norm_w))
        xs = _out_projection((y_a, y_b, y_c, y_d), i, w_out_bf16, xs, mod, rows(norm_post_mix))
        if i + 1 < depth:
            xs, mod, w_in_t16 = _mlp(xs, i, rows(norm_pre_mlp), mod, w_up, w_down, rows(norm_post_mlp),
                                     next_layer=adaln + (w_in_t,))
        else:
            xs = _mlp(xs, i, rows(norm_pre_mlp), mod, w_up, w_down, rows(norm_post_mlp))
    return xs.reshape(batch, seq, d)
```

```python
import functools

import jax
import jax.numpy as jnp
from jax import lax
from jax.experimental import pallas as pl
from jax.experimental.pallas import tpu as pltpu

F32 = jnp.float32
BF16 = jnp.bfloat16

D_MODEL = 2048
GW = 512
GDN_HEADS, GDN_HEAD_DIM, GDN_CHUNK = 4, 128, 64
SSM_HEADS, SSM_HEAD_DIM, SSM_STATE, SSM_CHUNK = 8, 64, 128, 128
ATTN_HEAD_DIM, ATTN_Q_HEADS, WINDOW = 64, 8, 128
ROPE_THETA = 10000.0
MLP_HIDDEN = 4 * D_MODEL
EPS = 1e-6

SUBLANES = 8
LANES = 128

_IN_SEGMENTS = (("gq", GW), ("gk", GW), ("gv", GW), ("gz", GW), ("gb", GDN_HEADS), ("ga", GDN_HEADS),
                ("sz", GW), ("sx", GW), ("sbc", 4 * SSM_STATE), ("sdt", SSM_HEADS),
                ("aq", GW), ("ak", 2 * ATTN_HEAD_DIM), ("av", 2 * ATTN_HEAD_DIM),
                ("cb", GW), ("cc", GW), ("ch", GW))
IN_OFFSET = {}
_off = 0
for _name, _size in _IN_SEGMENTS:
    IN_OFFSET[_name] = (_off, _off + _size)
    _off += _size
IN_WIDTH = _off

IN_GROUP = 3 * GW
CONTIGUOUS_GROUPS = (("gq", "gk", "gv"), ("sz", "sx", "sbc"), ("cb", "cc", "ch"))
GATHERED_GROUP = ("gz", "aq", "ak", "av", "gb", "ga", "sdt")
IN_COLS = (len(CONTIGUOUS_GROUPS) + 1) * IN_GROUP
for _grp in CONTIGUOUS_GROUPS:
    assert IN_OFFSET[_grp[0]][0] % SUBLANES == 0
    assert all(IN_OFFSET[a][1] == IN_OFFSET[b][0] for a, b in zip(_grp, _grp[1:]))
(SLAB_GQ, SLAB_GK, SLAB_GV, SLAB_SZ, SLAB_SX, SLAB_SBC, SLAB_CB, SLAB_CC, SLAB_CH, SLAB_GZ, SLAB_AQ) = range(11)
SLAB128_AK, SLAB128_AV, SLAB128_SMALL = 44, 45, 46
LANE_BETA, LANE_ALPHA, LANE_DT = 0, 4, 8

MIB = 1024 * 1024
V7X_VMEM_BYTES = 64 * MIB
VMEM_LIMIT = 56 * MIB


def _cparams(*sem, vmem_limit=VMEM_LIMIT):
    assert vmem_limit < V7X_VMEM_BYTES
    return pltpu.CompilerParams(dimension_semantics=sem, vmem_limit_bytes=vmem_limit)


def _sigmoid(x):
    return jax.nn.sigmoid(x)


def _silu(x):
    return x * jax.nn.sigmoid(x)


def _mm(a, b):
    return jnp.dot(a.astype(BF16), b.astype(BF16), preferred_element_type=F32)


def _mm_nt(a, b):
    return lax.dot_general(a.astype(BF16), b.astype(BF16), (((1,), (1,)), ((), ())),
                           preferred_element_type=F32)


def _mm_tn(a, b):
    return lax.dot_general(a.astype(BF16), b.astype(BF16), (((0,), (0,)), ((), ())),
                           preferred_element_type=F32)


def _blk(idx, size):
    assert size & (size - 1) == 0
    return lax.shift_right_logical(idx, size.bit_length() - 1)


def _split3(x):
    x1 = x.astype(BF16)
    r1 = x - x1.astype(F32)
    x2 = r1.astype(BF16)
    r2 = r1 - x2.astype(F32)
    return x1, x2, r2.astype(BF16)


def _cumsum_matrix(rows, chunk):
    r = lax.broadcasted_iota(jnp.int32, (2 * rows, rows), 0)
    s = lax.broadcasted_iota(jnp.int32, (2 * rows, rows), 1)
    rr = jnp.where(r >= rows, r - rows, r)
    same = _blk(rr, chunk) == _blk(s, chunk)
    sel = same & ((r >= rows) | (s <= rr))
    return jnp.where(sel, 1.0, 0.0).astype(BF16)


def _chunk_cumsum(g, mat):
    rows = g.shape[0]
    acc = None
    for part in _split3(g):
        t = jnp.dot(mat, part, preferred_element_type=F32)
        acc = t if acc is None else acc + t
    return acc[:rows], acc[rows:]


def _causal_conv(x, prev, w, taps):
    rows = x.shape[0]
    xp = jnp.concatenate([prev, x], axis=0)
    acc = x * w[taps - 1:taps]
    for d in range(1, taps):
        acc = acc + xp[SUBLANES - d:SUBLANES - d + rows] * w[taps - 1 - d:taps - d]
    return acc


def _causal_conv_staged(x_ref, prev, w, taps, stage_ref):
    rows = x_ref.shape[0]
    x = x_ref[...]
    stage_ref[0:SUBLANES, :] = prev
    stage_ref[SUBLANES:, :] = x
    acc = x * w[taps - 1:taps]
    for d in range(1, taps):
        acc = acc + stage_ref[pl.ds(SUBLANES - d, rows), :] * w[taps - 1 - d:taps - d]
    return acc


MOD_TN = 1536
MOD_ROWS = 64


def _mod_rope_kernel(c_ref, w_ref, b_ref, pos_ref, invf_ref, sign_ref, o_ref, cos_ref, sin_ref):
    d = c_ref.shape[0]

    def body(r, acc):
        rows = pl.ds(pl.multiple_of(r * MOD_ROWS, MOD_ROWS), MOD_ROWS)
        cc = _silu(c_ref[rows, :])
        prod = w_ref[0, rows, :] * cc
        return acc + jnp.sum(prod.reshape(MOD_ROWS // SUBLANES, SUBLANES, MOD_TN), axis=0)

    acc = lax.fori_loop(0, d // MOD_ROWS, body, jnp.zeros((SUBLANES, MOD_TN), F32))
    o_ref[0] = jnp.sum(acc, axis=0, keepdims=True) + b_ref[0]

    ang = pos_ref[...].astype(F32) * invf_ref[...]
    cos_ref[...] = jnp.cos(ang)
    sin_ref[...] = jnp.sin(ang) * sign_ref[...]


def _modulation_and_rope(c_col, ada_w, ada_b, layer, pos_col):
    depth, d, n = ada_w.shape
    seq = pos_col.shape[0]
    steps = n // MOD_TN
    tb = seq // steps
    assert steps * tb == seq and tb % SUBLANES == 0
    half = ATTN_HEAD_DIM // 2
    inv_freq = ROPE_THETA ** (-jnp.arange(0, ATTN_HEAD_DIM, 2, dtype=F32) / ATTN_HEAD_DIM)
    invf = jnp.tile(inv_freq, LANES // half).reshape(1, LANES)
    lane = jnp.arange(LANES)
    sign = jnp.where((lane % ATTN_HEAD_DIM) < half, -1.0, 1.0).astype(F32).reshape(1, LANES)
    table = jax.ShapeDtypeStruct((seq, LANES), F32)
    return pl.pallas_call(
        _mod_rope_kernel,
        grid=(steps,),
        in_specs=[
            pl.BlockSpec((d, 1), lambda j: (0, 0)),
            pl.BlockSpec((1, d, MOD_TN), lambda j: (layer, 0, j)),
            pl.BlockSpec((1, 1, MOD_TN), lambda j: (layer, 0, j)),
            pl.BlockSpec((tb, 1), lambda j: (j, 0)),
            pl.BlockSpec((1, LANES), lambda j: (0, 0)),
            pl.BlockSpec((1, LANES), lambda j: (0, 0)),
        ],
        out_specs=[pl.BlockSpec((1, 1, MOD_TN), lambda j: (0, 0, j)),
                   pl.BlockSpec((tb, LANES), lambda j: (j, 0)),
                   pl.BlockSpec((tb, LANES), lambda j: (j, 0))],
        out_shape=[jax.ShapeDtypeStruct((1, 1, n), F32), table, table],
        compiler_params=_cparams("parallel"),
        name="adaln_mod_rope",
    )(c_col, ada_w, ada_b, pos_col, invf, sign)


def _mod_spec(k):
    return pl.BlockSpec((None, 1, D_MODEL), lambda *grid_idx: (0, 0, k))


NORM_ROWS = 256


def _prenorm_to(x_ref, nw_ref, sc_ref, sh_ref, h_ref):
    tm = x_ref.shape[0]
    gain = nw_ref[...] * (1.0 + sc_ref[...])
    shift = sh_ref[...]

    def body(r, carry):
        rows = pl.ds(pl.multiple_of(r * NORM_ROWS, NORM_ROWS), NORM_ROWS)
        x = x_ref[rows, :]
        inv = lax.rsqrt(jnp.mean(x * x, axis=-1, keepdims=True) + EPS)
        h_ref[rows, :] = (x * inv * gain + shift).astype(BF16)
        return carry

    lax.fori_loop(0, tm // NORM_ROWS, body, 0)


INPROJ_TM = 1024

MOD_SHIFT_A, MOD_SCALE_A, MOD_GATE_A, MOD_SHIFT_M, MOD_SCALE_M, MOD_GATE_M = range(6)


def _layer_spec(shape, layer, *rest):
    rest = rest or (0,) * len(shape)

    def index_map(*grid_idx):
        return (layer,) + tuple(r(*grid_idx) if callable(r) else r for r in rest)

    return pl.BlockSpec((None,) + tuple(shape), index_map)


def _inproj_kernel(x_ref, nw_ref, sc_ref, sh_ref, wt_ref, tail_ref, wo_ref, o_ref, wo16_ref, h_ref):
    j = pl.program_id(1)
    wo16_ref[...] = wo_ref[...].astype(BF16)

    @pl.when(j == 0)
    def _():
        _prenorm_to(x_ref, nw_ref, sc_ref, sh_ref, h_ref)

    def project(w_ref):
        o_ref[...] = _mm_nt(h_ref[...], w_ref[...])

    @pl.when(j < len(CONTIGUOUS_GROUPS))
    def _():
        project(wt_ref)

    @pl.when(j == len(CONTIGUOUS_GROUPS))
    def _():
        project(tail_ref)


def _group_row(j):
    starts = [IN_OFFSET[grp[0]][0] // SUBLANES for grp in CONTIGUOUS_GROUPS]
    row8 = starts[-1]
    for k in range(len(starts) - 2, -1, -1):
        row8 = jnp.where(j <= k, starts[k], row8)
    return row8 * SUBLANES


def _in_projection(x, layer, norm_w, mod, w_t, w_gathered, w_out):
    seq, d = x.shape
    tm = min(INPROJ_TM, seq)
    n_groups = IN_COLS // IN_GROUP
    wo_rows = w_out.shape[1] // ((seq // tm) * n_groups)
    assert wo_rows * (seq // tm) * n_groups == w_out.shape[1] and wo_rows % (2 * SUBLANES) == 0
    return pl.pallas_call(
        _inproj_kernel,
        grid=(seq // tm, n_groups),
        in_specs=[
            pl.BlockSpec((tm, d), lambda i, j: (i, 0)),
            _layer_spec((1, d), layer),
            _mod_spec(MOD_SCALE_A),
            _mod_spec(MOD_SHIFT_A),
            pl.BlockSpec((pl.Element(IN_GROUP), pl.Element(d)), lambda i, j: (_group_row(j), 0)),
            pl.BlockSpec((IN_GROUP, d), lambda i, j: (0, 0), pipeline_mode=pl.Buffered(1)),
            pl.BlockSpec((None, wo_rows, d), lambda i, j: (layer, i * n_groups + j, 0)),
        ],
        out_specs=[pl.BlockSpec((tm, IN_GROUP), lambda i, j: (i, j)),
                   pl.BlockSpec((wo_rows, d), lambda i, j: (i * n_groups + j, 0))],
        out_shape=[jax.ShapeDtypeStruct((seq, IN_COLS), F32), jax.ShapeDtypeStruct(w_out.shape[1:], BF16)],
        scratch_shapes=[pltpu.VMEM((tm, d), BF16)],
        compiler_params=_cparams("arbitrary", "arbitrary"),
        name="in_proj",
    )(x, norm_w, mod, mod, w_t, w_gathered, w_out)


GDN_TB = 512


def _l2norm(x):
    return x * lax.rsqrt(jnp.sum(x * x, axis=-1, keepdims=True) + EPS)


GDN_BLK = 128
GDN_PREP_STAGES_PER_CHUNK = 2


def _unit_lower_inverse_minus_identity(ms, r, s):
    same8 = _blk(r, 8) == _blk(s, 8)
    same16 = _blk(r, 16) == _blk(s, 16)
    same32 = _blk(r, 32) == _blk(s, 32)
    n1 = [jnp.where(same8, -m, 0.0) for m in ms]
    n2 = [_mm(a, a) for a in n1]
    yield
    n3 = [_mm(a, b) for a, b in zip(n1, n2)]
    n4 = [_mm(b, b) for b in n2]
    yield
    ys = [a + b + c for a, b, c in zip(n1, n2, n3)]
    ts = [_mm(y, d) for y, d in zip(ys, n4)]
    yield
    ys = [y + d + t for y, d, t in zip(ys, n4, ts)]
    for sel in (same16 & ~same8, same32 & ~same16, ~same32):
        blks = [jnp.where(sel, m, 0.0) for m in ms]
        cs = [b + _mm(y, b) for y, b in zip(ys, blks)]
        yield
        ts = [_mm(c, y) for c, y in zip(cs, ys)]
        yield
        ys = [y - c - t for y, c, t in zip(ys, cs, ts)]
    return ys


def _run_interleaved(main, side, main_stages_per_side_stage):
    side_live = True
    done = 0
    while True:
        try:
            next(main)
        except StopIteration as stop:
            result = stop.value
            break
        done += 1
        if side_live and done % main_stages_per_side_stage == 0:
            side_live = next(side, StopIteration) is not StopIteration
    while side_live:
        side_live = next(side, StopIteration) is not StopIteration
    return result


def _gdn_prepare(q_ref, k_ref, v_ref, qp_ref, kp_ref, vp_ref, sm_ref, cw_ref, lp_ref, cum_ref, conv_ref, first_block):
    tb = q_ref.shape[0]
    hd = GDN_HEAD_DIM
    nblk = tb // GDN_BLK

    keep_prev = jnp.where(first_block, 0.0, 1.0)
    cw = cw_ref[...]

    def conv_silu(x_ref, p_ref, idx):
        w = cw[:, idx * GW:(idx + 1) * GW]
        return _silu(_causal_conv_staged(x_ref, p_ref[...] * keep_prev, w, 4, conv_ref.at[idx]))

    q = conv_silu(q_ref, qp_ref, 0)
    yield
    k = conv_silu(k_ref, kp_ref, 1)
    yield
    v = conv_silu(v_ref, vp_ref, 2)
    yield

    small = sm_ref[...]
    a_log = lp_ref[0:1, :]
    dt_bias = lp_ref[1:2, :]
    beta_all = _sigmoid(small)
    g_all = -jnp.exp(a_log) * jax.nn.softplus(small + dt_bias)
    gc_all, gtot_all = _chunk_cumsum(g_all, cum_ref[...])
    gc_t = gc_all.T
    yield

    r = lax.broadcasted_iota(jnp.int32, (GDN_BLK, GDN_BLK), 0)
    s = lax.broadcasted_iota(jnp.int32, (GDN_BLK, GDN_BLK), 1)
    same_chunk = _blk(r, GDN_CHUNK) == _blk(s, GDN_CHUNK)
    mask_incl = same_chunk & (s <= r)
    mask_strict = same_chunk & (s < r)

    problems = [(slice(b * GDN_BLK, (b + 1) * GDN_BLK), h) for b in range(nblk) for h in range(GDN_HEADS)]
    head_cols = lambda h: slice(h * hd, (h + 1) * hd)
    gate = lambda arr, rows, lane: arr[rows, lane:lane + 1]

    qn = [_l2norm(q[rows, head_cols(h)]) * (hd ** -0.5) for rows, h in problems]
    kn = [_l2norm(k[rows, head_cols(h)]) for rows, h in problems]
    beta = [gate(beta_all, rows, LANE_BETA + h) for rows, h in problems]
    gcol = [gate(gc_all, rows, LANE_ALPHA + h) for rows, h in problems]
    gtot = [gate(gtot_all, rows, LANE_ALPHA + h) for rows, h in problems]
    grow = [gc_t[LANE_ALPHA + h:LANE_ALPHA + h + 1, rows] for rows, h in problems]
    yield

    decay = [jnp.exp(jnp.where(mask_incl, gc - gr, -jnp.inf)) for gc, gr in zip(gcol, grow)]
    k_beta = [kk * b for kk, b in zip(kn, beta)]
    k16 = [kk.astype(BF16) for kk in kn]
    yield
    ms = [jnp.where(mask_strict, _mm_nt(kb, kk) * d, 0.0) for kb, kk, d in zip(k_beta, k16, decay)]
    yield
    attn = [(_mm_nt(qq, kk) * d).astype(BF16) for qq, kk, d in zip(qn, k16, decay)]
    ys = yield from _unit_lower_inverse_minus_identity(ms, r, s)
    yield

    e_gc = [jnp.exp(gc) for gc in gcol]
    rhs = [jnp.concatenate([v[rows, head_cols(h)] * b, kb * e], axis=1)
           for (rows, h), b, kb, e in zip(problems, beta, k_beta, e_gc)]
    uw = [x + _mm(y, x) for y, x in zip(ys, rhs)]
    yield
    u = [x[:, :hd] for x in uw]
    w = [x[:, hd:].astype(BF16) for x in uw]
    q_dec = [(qq * e).astype(BF16) for qq, e in zip(qn, e_gc)]
    k_dec = [(kk * jnp.exp(gt - gc)).astype(BF16) for kk, gt, gc in zip(kn, gtot, gcol)]
    return u, w, q_dec, k_dec, attn, gtot_all


def _gdn_recurrence(u, w, q_dec, k_dec, attn, gtot_all, state_ref, z, nw_ref, o_ref):
    hd = GDN_HEAD_DIM
    nblk = o_ref.shape[0] // GDN_BLK
    chunks_per_blk = GDN_BLK // GDN_CHUNK
    head_cols = lambda h: slice(h * hd, (h + 1) * hd)

    states = [state_ref[h] for h in range(GDN_HEADS)]
    v_new = [[] for _ in range(nblk * GDN_HEADS)]
    o_inter = [[] for _ in range(nblk * GDN_HEADS)]
    for b in range(nblk):
        for c in range(chunks_per_blk):
            rows = slice(c * GDN_CHUNK, (c + 1) * GDN_CHUNK)
            first = b * GDN_BLK + c * GDN_CHUNK
            for h in range(GDN_HEADS):
                p = b * GDN_HEADS + h
                st16 = states[h].astype(BF16)
                vn = u(p, rows) - jnp.dot(w(p, rows), st16, preferred_element_type=F32)
                o_inter[p].append(jnp.dot(q_dec(p, rows), st16, preferred_element_type=F32))
                chunk_dec = jnp.exp(gtot_all[first:first + 1, LANE_ALPHA + h:LANE_ALPHA + h + 1])
                states[h] = states[h] * chunk_dec + _mm_tn(k_dec(p, rows), vn)
                v_new[p].append(vn)
            yield
        rows = slice(b * GDN_BLK, (b + 1) * GDN_BLK)
        for h in range(GDN_HEADS):
            p = b * GDN_HEADS + h
            o = jnp.concatenate(o_inter[p], axis=0) + _mm(attn(p, slice(None)), jnp.concatenate(v_new[p], axis=0))
            o = o * lax.rsqrt(jnp.mean(o * o, axis=-1, keepdims=True) + EPS) * nw_ref[...]
            o_ref[rows, head_cols(h)] = (o * _silu(z[rows, head_cols(h)])).astype(o_ref.dtype)
    for h in range(GDN_HEADS):
        state_ref[h] = states[h]


def _gdn_kernel(q_ref, k_ref, v_ref, z_ref, qp_ref, kp_ref, vp_ref, sm_ref, cw_ref, lp_ref, nw_ref, cum_ref,
                o_ref, state_ref, conv_ref, u_ref, w_ref, qd_ref, kd_ref, at_ref, gt_ref):
    step = pl.program_id(0)
    write_slot = step % 2
    read_slot = 1 - write_slot
    n_problems = u_ref.shape[1]

    @pl.when(step == 0)
    def _():
        state_ref[...] = jnp.zeros_like(state_ref)
        for ref in (u_ref, w_ref, qd_ref, kd_ref, at_ref, gt_ref):
            ref[1] = jnp.zeros(ref.shape[1:], ref.dtype)

    recurrence = _gdn_recurrence(*[(lambda p, rows, ref=ref: ref[read_slot, p, rows])
                                   for ref in (u_ref, w_ref, qd_ref, kd_ref, at_ref)],
                                 gt_ref[read_slot], state_ref, z_ref[...], nw_ref, o_ref)
    prepare = _gdn_prepare(q_ref, k_ref, v_ref, qp_ref, kp_ref, vp_ref, sm_ref, cw_ref, lp_ref, cum_ref, conv_ref,
                           step == 0)
    u, w, q_dec, k_dec, attn, gtot_all = _run_interleaved(prepare, recurrence, GDN_PREP_STAGES_PER_CHUNK)
    for p in range(n_problems):
        u_ref[write_slot, p] = u[p]
        w_ref[write_slot, p] = w[p]
        qd_ref[write_slot, p] = q_dec[p]
        kd_ref[write_slot, p] = k_dec[p]
        at_ref[write_slot, p] = attn[p]
    gt_ref[write_slot] = gtot_all


def _prev_rows_spec(tb, width, slab):
    blocks = tb // SUBLANES
    return pl.BlockSpec((SUBLANES, width), lambda i: (jnp.maximum(i * blocks - 1, 0), slab))


def _lane_params(lane0, *rows):
    stacked = jnp.stack([r.astype(F32) for r in rows], axis=1)
    nrows, heads = stacked.shape[1:]
    return jnp.pad(stacked, ((0, 0), (0, SUBLANES - nrows), (lane0, LANES - lane0 - heads)))


def _gated_deltanet(proj, layer, conv_w, lane_params, norm_w):
    seq = proj.shape[0]
    tb = min(GDN_TB, seq)
    nsteps = seq // tb
    n_problems = (tb // GDN_BLK) * GDN_HEADS
    prep = lambda t: jnp.minimum(t, nsteps - 1)
    done = lambda t: jnp.maximum(t - 1, 0)
    slab = lambda j: pl.BlockSpec((tb, GW), lambda t: (prep(t), j))
    prev = lambda j: pl.BlockSpec((SUBLANES, GW),
                                  lambda t: (jnp.maximum(prep(t) * (tb // SUBLANES) - 1, 0), j))
    full = lambda shape: _layer_spec(shape, layer)
    factor = lambda dtype: pltpu.VMEM((2, n_problems, GDN_BLK, GDN_HEAD_DIM), dtype)
    return pl.pallas_call(
        _gdn_kernel,
        grid=(nsteps + 1,),
        in_specs=[
            slab(SLAB_GQ), slab(SLAB_GK), slab(SLAB_GV),
            pl.BlockSpec((tb, GW), lambda t: (done(t), SLAB_GZ)),
            prev(SLAB_GQ), prev(SLAB_GK), prev(SLAB_GV),
            pl.BlockSpec((tb, LANES), lambda t: (prep(t), SLAB128_SMALL)),
            full((4, 3 * GW)), full((SUBLANES, LANES)), full((1, GDN_HEAD_DIM)),
            pl.BlockSpec((2 * tb, tb), lambda t: (0, 0)),
        ],
        out_specs=pl.BlockSpec((tb, GW), lambda t: (done(t), 0)),
        out_shape=jax.ShapeDtypeStruct((seq, GW), BF16),
        scratch_shapes=[pltpu.VMEM((GDN_HEADS, GDN_HEAD_DIM, GDN_HEAD_DIM), F32),
                        pltpu.VMEM((3, tb + SUBLANES, GW), F32),
                        factor(F32), factor(BF16), factor(BF16), factor(BF16), factor(BF16),
                        pltpu.VMEM((2, tb, LANES), F32)],
        compiler_params=_cparams("arbitrary"),
        name="gated_deltanet",
    )(proj, proj, proj, proj, proj, proj, proj, proj, conv_w, lane_params, norm_w, _cumsum_matrix(tb, GDN_CHUNK))


SSD_SUB = 256


def _ssd_block(first_block, z_ref, x_ref, bc_ref, xp_ref, bcp_ref, sm_ref, cw_ref, cb_ref, lp_ref, nw_ref, cum_ref,
               o_ref, state_ref):
    keep_prev = jnp.where(first_block, 0.0, 1.0)
    for sub in range(z_ref.shape[0] // SSD_SUB):
        r0 = sub * SSD_SUB
        if sub == 0:
            x_prev, bc_prev = xp_ref[...] * keep_prev, bcp_ref[...] * keep_prev
        else:
            x_prev, bc_prev = x_ref[r0 - SUBLANES:r0, :], bc_ref[r0 - SUBLANES:r0, :]
        _ssd_sub_block(r0, x_prev, bc_prev, z_ref, x_ref, bc_ref, sm_ref, cw_ref, cb_ref, lp_ref, nw_ref, cum_ref,
                       o_ref, state_ref)


def _ssd_sub_block(r0, x_prev, bc_prev, z_ref, x_ref, bc_ref, sm_ref, cw_ref, cb_ref, lp_ref, nw_ref, cum_ref,
                   o_ref, state_ref):
    tb = SSD_SUB
    block = slice(r0, r0 + tb)
    nstate = SSM_STATE
    chunk = SSM_CHUNK
    nchunks = tb // chunk
    n_pairs = SSM_HEADS // 2
    pairs_per_group = n_pairs // 2

    cw = cw_ref[...]
    cbias = cb_ref[...]
    xs = _silu(_causal_conv(x_ref[block, :], x_prev, cw[:, :GW], 4) + cbias[:, :GW])
    bc = _silu(_causal_conv(bc_ref[block, :], bc_prev, cw[:, GW:], 4) + cbias[:, GW:])
    z = z_ref[block, :]

    small = sm_ref[block, :]
    a_neg = -jnp.exp(lp_ref[0:1, :])
    dt_all = jax.nn.softplus(small + lp_ref[1:2, :])
    dskip_all = lp_ref[2:3, :]
    gc_all, gtot_all = _chunk_cumsum(dt_all * a_neg, cum_ref[...])
    gc_t = gc_all.T

    r = lax.broadcasted_iota(jnp.int32, (chunk, chunk), 0)
    s = lax.broadcasted_iota(jnp.int32, (chunk, chunk), 1)
    mask_incl = s <= r
    lane = lax.broadcasted_iota(jnp.int32, (1, LANES), 1)
    first_half = lane < SSM_HEAD_DIM
    by_half = lambda a, b: jnp.where(first_half, a, b)

    problems = [(c, p) for c in range(nchunks) for p in range(n_pairs)]
    rows_of = lambda c: slice(c * chunk, (c + 1) * chunk)
    cols_of = lambda p: slice(p * LANES, (p + 1) * LANES)
    lanes_of = lambda p: (LANE_DT + 2 * p, LANE_DT + 2 * p + 1)
    col = lambda arr, c, l: arr[rows_of(c), l:l + 1]

    b_mat = {(c, g): bc[rows_of(c), g * nstate:(g + 1) * nstate].astype(BF16)
             for c in range(nchunks) for g in range(2)}
    c_mat = {(c, g): bc[rows_of(c), (2 + g) * nstate:(3 + g) * nstate].astype(BF16)
             for c in range(nchunks) for g in range(2)}
    cb = {key: _mm_nt(c_mat[key], b_mat[key]) for key in b_mat}

    x_pair = [xs[rows_of(c), cols_of(p)] for c, p in problems]
    xdt = [x * by_half(col(dt_all, c, lanes_of(p)[0]), col(dt_all, c, lanes_of(p)[1]))
           for x, (c, p) in zip(x_pair, problems)]
    y_diag = []
    for (c, p), xd in zip(problems, xdt):
        halves = (by_half(xd, 0.0), by_half(0.0, xd))
        acc = None
        for l, half in zip(lanes_of(p), halves):
            grow = gc_t[l:l + 1, rows_of(c)]
            lmat = jnp.exp(jnp.where(mask_incl, col(gc_all, c, l) - grow, -jnp.inf))
            t = _mm(cb[c, p // pairs_per_group] * lmat, half)
            acc = t if acc is None else acc + t
        y_diag.append(acc)

    e_in = [by_half(jnp.exp(col(gc_all, c, lanes_of(p)[0])), jnp.exp(col(gc_all, c, lanes_of(p)[1])))
            for c, p in problems]
    e_out = [by_half(jnp.exp(col(gtot_all, c, lanes_of(p)[0]) - col(gc_all, c, lanes_of(p)[0])),
                     jnp.exp(col(gtot_all, c, lanes_of(p)[1]) - col(gc_all, c, lanes_of(p)[1])))
             for c, p in problems]
    states = [state_ref[p] for p in range(n_pairs)]
    y_off = []
    for idx, (c, p) in enumerate(problems):
        la, lb = lanes_of(p)
        g = p // pairs_per_group
        y_off.append(e_in[idx] * _mm(c_mat[c, g], states[p]))
        first = c * chunk
        cd_lane = by_half(jnp.exp(gtot_all[first:first + 1, la:la + 1]),
                          jnp.exp(gtot_all[first:first + 1, lb:lb + 1]))
        states[p] = states[p] * cd_lane + _mm_tn(b_mat[c, g], xdt[idx] * e_out[idx])
    for p in range(n_pairs):
        state_ref[p] = states[p]

    ys = []
    for idx, (c, p) in enumerate(problems):
        la, lb = lanes_of(p)
        y = y_diag[idx] + y_off[idx] + x_pair[idx] * by_half(dskip_all[:, la:la + 1], dskip_all[:, lb:lb + 1])
        ys.append(y * _silu(z[rows_of(c), cols_of(p)]))
    sumsq = [jnp.sum(y * y, axis=-1, keepdims=True) for y in ys]
    for c in range(nchunks):
        for g in range(2):
            members = [c * n_pairs + g * pairs_per_group + k for k in range(pairs_per_group)]
            total = sumsq[members[0]]
            for idx in members[1:]:
                total = total + sumsq[idx]
            inv = lax.rsqrt(total / (pairs_per_group * LANES) + EPS)
            for idx in members:
                cols = cols_of(problems[idx][1])
                out_rows = slice(r0 + c * chunk, r0 + (c + 1) * chunk)
                o_ref[out_rows, cols] = (ys[idx] * inv * nw_ref[:, cols]).astype(o_ref.dtype)


def _rotate_half(x, lane_in_head_low):
    width = x.shape[1]
    half = ATTN_HEAD_DIM // 2
    ahead = pltpu.roll(x, width - half, axis=1)
    behind = pltpu.roll(x, half, axis=1)
    return jnp.where(lane_in_head_low, ahead, behind)


SWA_TB = 512


def _swa_kernel(q_ref, k_ref, v_ref, cos_ref, sin_ref, sink_ref, cb_ref, cc_ref, ch_ref, ccp_ref, chp_ref, cw_ref,
                sz_ref, sx_ref, sbc_ref, sxp_ref, sbcp_ref, sm_ref, scw_ref, scb_ref, slp_ref, snw_ref, cum_ref,
                o_ref, od_ref, ob_ref, kprev_ref, vprev_ref, ssm_state_ref):
    step = pl.program_id(0)
    tb = q_ref.shape[0]
    win = WINDOW
    hd = ATTN_HEAD_DIM

    @pl.when(step == 0)
    def _():
        kprev_ref[...] = jnp.zeros_like(kprev_ref)
        vprev_ref[...] = jnp.zeros_like(vprev_ref)
        ssm_state_ref[...] = jnp.zeros_like(ssm_state_ref)

    keep_prev = jnp.where(step == 0, 0.0, 1.0)
    c_h = cc_ref[...] * ch_ref[...]
    c_h_prev = ccp_ref[...] * chp_ref[...] * keep_prev
    od_ref[...] = (cb_ref[...] * _causal_conv(c_h, c_h_prev, cw_ref[...], 3)).astype(od_ref.dtype)

    cos = cos_ref[...]
    sin = sin_ref[...]
    lane = lax.broadcasted_iota(jnp.int32, (1, LANES), 1)
    low = (lane & (hd - 1)) < (hd // 2)
    first_half = lane < hd

    k_cur = k_ref[...]
    k_cur = k_cur * cos + _rotate_half(k_cur, low) * sin
    v_cur = v_ref[...]
    k_all = jnp.concatenate([kprev_ref[...], k_cur], axis=0)
    v_all = jnp.concatenate([vprev_ref[...], v_cur], axis=0)
    k_swap = pltpu.roll(k_all, hd, axis=1)
    v_swap = pltpu.roll(v_all, hd, axis=1)

    def placed(x, x_swap, kv_head, pos):
        src = x if kv_head == pos else x_swap
        keep = first_half if pos == 0 else ~first_half
        return jnp.where(keep, src, 0.0).astype(BF16)

    qi = lax.broadcasted_iota(jnp.int32, (win, 2 * win), 0)
    kj = lax.broadcasted_iota(jnp.int32, (win, 2 * win), 1)
    rel = qi + win - kj
    band = (rel >= 0) & (rel < win)
    band_first = band & ((kj >= win) | (step > 0))

    problems = [(j, h) for j in range(tb // win) for h in range(ATTN_Q_HEADS)]
    kv_of = lambda head: head // (ATTN_Q_HEADS // 2)
    pair_cols = lambda pair: slice(pair * LANES, (pair + 1) * LANES)
    k_at = {(g, pos): placed(k_all, k_swap, g, pos) for g in range(2) for pos in range(2)}
    v_at = {(g, pos): placed(v_all, v_swap, g, pos) for g in range(2) for pos in range(2)}
    q_pairs = []
    for pair in range(ATTN_Q_HEADS // 2):
        q_pair = q_ref[:, pair_cols(pair)]
        q_pairs.append((q_pair * cos + _rotate_half(q_pair, low) * sin).astype(BF16))
    sinks = [sink_ref[h:h + 1, 0:1] for _, h in problems]
    sc = [jnp.where(band_first if j == 0 else band,
                    _mm_nt(q_pairs[h // 2][j * win:(j + 1) * win],
                           k_at[kv_of(h), h % 2][j * win:(j + 2) * win]) * (hd ** -0.5),
                    -jnp.inf)
          for j, h in problems]
    mx = [jnp.maximum(jnp.max(s_, axis=-1, keepdims=True), sk) for s_, sk in zip(sc, sinks)]
    p = [jnp.exp(s_ - m_) for s_, m_ in zip(sc, mx)]
    denom = [jnp.sum(p_, axis=-1, keepdims=True) + jnp.exp(sk - m_) for p_, sk, m_ in zip(p, sinks, mx)]
    o = [_mm(p_, v_at[kv_of(h), h % 2][j * win:(j + 2) * win]) / d_
         for (j, h), p_, d_ in zip(problems, p, denom)]
    for idx in range(0, len(problems), 2):
        j, h = problems[idx]
        o_ref[j * win:(j + 1) * win, pair_cols(h // 2)] = (o[idx] + o[idx + 1]).astype(o_ref.dtype)

    kprev_ref[...] = k_cur[tb - win:]
    vprev_ref[...] = v_cur[tb - win:]

    _ssd_block(step == 0, sz_ref, sx_ref, sbc_ref, sxp_ref, sbcp_ref, sm_ref, scw_ref, scb_ref, slp_ref, snw_ref,
               cum_ref, ob_ref, ssm_state_ref)


def _ssd_attention_short_conv(proj, layer, cos, sin, sink_rows, sc_conv_w, ssm_conv_w, ssm_conv_b, ssm_lanes,
                              ssm_norm_w):
    seq = proj.shape[0]
    tb = min(SWA_TB, seq)
    assert tb % SSD_SUB == 0
    slab = lambda j: pl.BlockSpec((tb, GW), lambda i: (i, j))
    full = lambda shape: _layer_spec(shape, layer)
    out = jax.ShapeDtypeStruct((seq, GW), BF16)
    return pl.pallas_call(
        _swa_kernel,
        grid=(seq // tb,),
        in_specs=[
            slab(SLAB_AQ),
            pl.BlockSpec((tb, LANES), lambda i: (i, SLAB128_AK)),
            pl.BlockSpec((tb, LANES), lambda i: (i, SLAB128_AV)),
            pl.BlockSpec((tb, LANES), lambda i: (i, 0)),
            pl.BlockSpec((tb, LANES), lambda i: (i, 0)),
            _layer_spec((ATTN_Q_HEADS, LANES), layer),
            slab(SLAB_CB), slab(SLAB_CC), slab(SLAB_CH),
            _prev_rows_spec(tb, GW, SLAB_CC), _prev_rows_spec(tb, GW, SLAB_CH),
            _layer_spec((3, GW), layer),
            slab(SLAB_SZ), slab(SLAB_SX), slab(SLAB_SBC),
            _prev_rows_spec(tb, GW, SLAB_SX), _prev_rows_spec(tb, GW, SLAB_SBC),
            pl.BlockSpec((tb, LANES), lambda i: (i, SLAB128_SMALL)),
            full((4, 2 * GW)), full((1, 2 * GW)), full((SUBLANES, LANES)), full((1, GW)),
            pl.BlockSpec((2 * SSD_SUB, SSD_SUB), lambda i: (0, 0)),
        ],
        out_specs=[pl.BlockSpec((tb, GW), lambda i: (i, 0))] * 3,
        out_shape=[out, out, out],
        scratch_shapes=[pltpu.VMEM((WINDOW, LANES), F32), pltpu.VMEM((WINDOW, LANES), F32),
                        pltpu.VMEM((SSM_HEADS // 2, SSM_STATE, LANES), F32)],
        compiler_params=_cparams("arbitrary"),
        name="ssd_swa_sconv",
    )(proj, proj, proj, cos, sin, sink_rows, proj, proj, proj, proj, proj, sc_conv_w,
      proj, proj, proj, proj, proj, proj, ssm_conv_w, ssm_conv_b, ssm_lanes, ssm_norm_w,
      _cumsum_matrix(SSD_SUB, SSM_CHUNK))


OUTPROJ_TM = 512


def _outproj_kernel(ya_ref, yb_ref, yc_ref, yd_ref, w_ref, x_ref, gate_ref, nw_ref, o_ref):
    y = jnp.concatenate([ya_ref[...], yb_ref[...], yc_ref[...], yd_ref[...]], axis=1)
    y = jnp.dot(y, w_ref[...], preferred_element_type=F32)
    inv = lax.rsqrt(jnp.mean(y * y, axis=-1, keepdims=True) + EPS)
    o_ref[...] = x_ref[...] + y * inv * (gate_ref[...] * nw_ref[...])


def _out_projection(ys, layer, w_bf16, x, mod, norm_w):
    seq, d = x.shape
    tm = min(OUTPROJ_TM, seq)
    return pl.pallas_call(
        _outproj_kernel,
        grid=(seq // tm,),
        in_specs=[pl.BlockSpec((tm, GW), lambda i: (i, 0))] * 4 + [
            pl.BlockSpec((4 * GW, d), lambda i: (0, 0)),
            pl.BlockSpec((tm, d), lambda i: (i, 0)),
            _mod_spec(MOD_GATE_A),
            _layer_spec((1, d), layer),
        ],
        out_specs=pl.BlockSpec((tm, d), lambda i: (i, 0)),
        out_shape=jax.ShapeDtypeStruct((seq, d), F32),
        compiler_params=_cparams("parallel"),
        name="out_proj",
    )(*ys, w_bf16, x, mod, norm_w)


MLP_TM = 1024
MLP_TH = 512
NEXT_WIN_ROWS = 48


def _mlp_kernel(*refs, next_layer):
    if next_layer:
        (x_ref, nw_ref, sc_ref, sh_ref, wu_ref, wd_ref, gate_ref, pw_ref, c_ref, aw_ref, ab_ref, wt_ref,
         o_ref, mod_ref, wt16_ref, h_ref) = refs
    else:
        x_ref, nw_ref, sc_ref, sh_ref, wu_ref, wd_ref, gate_ref, pw_ref, o_ref, h_ref = refs
    j = pl.program_id(1)

    @pl.when(j == 0)
    def _():
        _prenorm_to(x_ref, nw_ref, sc_ref, sh_ref, h_ref)
        o_ref[...] = jnp.zeros_like(o_ref)

    hid = _mm(h_ref[...], wu_ref[...])
    hid = jnp.square(jnp.maximum(hid, 0.0))
    o_ref[...] += _mm(hid, wd_ref[...])

    if next_layer:
        prod = aw_ref[...] * _silu(c_ref[...])
        part = jnp.sum(prod.reshape(prod.shape[0] // SUBLANES, SUBLANES, LANES), axis=0)
        mod_ref[...] = jnp.sum(part, axis=0, keepdims=True) + ab_ref[...]
        wt16_ref[...] = wt_ref[...].astype(BF16)

    @pl.when(j == pl.num_programs(1) - 1)
    def _():
        gain = gate_ref[...] * pw_ref[...]

        def body(r, carry):
            rows = pl.ds(pl.multiple_of(r * NORM_ROWS, NORM_ROWS), NORM_ROWS)
            y = o_ref[rows, :]
            inv = lax.rsqrt(jnp.mean(y * y, axis=-1, keepdims=True) + EPS)
            o_ref[rows, :] = x_ref[rows, :] + y * inv * gain
            return carry

        lax.fori_loop(0, x_ref.shape[0] // NORM_ROWS, body, 0)


def _mlp(x, layer, norm_w, mod, w_up, w_down, post_w, next_layer=None):
    seq, d = x.shape
    hidden = w_up.shape[2]
    tm = min(MLP_TM, seq)
    steps_per_tile = hidden // MLP_TH
    vmem = (2 * 2 * tm * d * 4 + tm * d * 2 + 2 * 2 * d * MLP_TH * w_up.dtype.itemsize
            + d * MLP_TH * 2 + tm * MLP_TH * 6)
    in_specs = [
        pl.BlockSpec((tm, d), lambda i, j: (i, 0)),
        _layer_spec((1, d), layer),
        _mod_spec(MOD_SCALE_M),
        _mod_spec(MOD_SHIFT_M),
        _layer_spec((d, MLP_TH), layer, 0, lambda i, j: j),
        _layer_spec((MLP_TH, d), layer, lambda i, j: j, 0),
        _mod_spec(MOD_GATE_M),
        _layer_spec((1, d), layer),
    ]
    out_specs = [pl.BlockSpec((tm, d), lambda i, j: (i, 0))]
    out_shape = [jax.ShapeDtypeStruct((seq, d), F32)]
    args = [x, norm_w, mod, mod, w_up, w_down, mod, post_w]
    if next_layer:
        c_col, ada_w, ada_b, w_in_t = next_layer
        n = ada_w.shape[2]
        n_steps = (seq // tm) * steps_per_tile
        wt_rows = NEXT_WIN_ROWS
        assert n_steps >= n // LANES and n_steps * wt_rows >= w_in_t.shape[1] and w_in_t.shape[1] % wt_rows == 0
        col = lambda i, j: jnp.minimum(i * steps_per_tile + j, n // LANES - 1)
        wt_blk = lambda i, j: jnp.minimum(i * steps_per_tile + j, w_in_t.shape[1] // wt_rows - 1)
        in_specs += [pl.BlockSpec((d, 1), lambda i, j: (0, 0)),
                     pl.BlockSpec((None, d, LANES), lambda i, j: (layer + 1, 0, col(i, j))),
                     pl.BlockSpec((None, 1, LANES), lambda i, j: (layer + 1, 0, col(i, j))),
                     pl.BlockSpec((None, wt_rows, d), lambda i, j: (layer + 1, wt_blk(i, j), 0))]
        out_specs += [pl.BlockSpec((None, 1, LANES), lambda i, j: (0, 0, col(i, j))),
                      pl.BlockSpec((wt_rows, d), lambda i, j: (wt_blk(i, j), 0))]
        out_shape += [jax.ShapeDtypeStruct((1, 1, n), F32), jax.ShapeDtypeStruct(w_in_t.shape[1:], BF16)]
        args += [c_col, ada_w, ada_b, w_in_t]
        vmem += 2 * 2 * d * LANES * 4 + 2 * wt_rows * d * 6
    outs = pl.pallas_call(
        functools.partial(_mlp_kernel, next_layer=bool(next_layer)),
        grid=(seq // tm, steps_per_tile),
        in_specs=in_specs,
        out_specs=out_specs,
        out_shape=out_shape,
        scratch_shapes=[pltpu.VMEM((tm, d), BF16)],
        compiler_params=_cparams("arbitrary", "arbitrary", vmem_limit=vmem),
        name="mlp",
    )(*args)
    return outs if next_layer else outs[0]


def _gathered_group(w_t):
    parts = [w_t[IN_OFFSET[name][0]:IN_OFFSET[name][1]] for name in GATHERED_GROUP]
    used = sum(p.shape[0] for p in parts)
    parts.append(jnp.zeros((IN_GROUP - used, w_t.shape[1]), w_t.dtype))
    return jnp.concatenate(parts, axis=0)


def kernel(x, c, positions, ada_w, ada_b, norm_pre_mix, norm_post_mix, norm_pre_mlp, norm_post_mlp,
           w_in, w_out, gdn_conv_w, gdn_a_log, gdn_dt_bias, gdn_norm_w, ssm_conv_w, ssm_conv_b,
           ssm_a_log, ssm_dt_bias, ssm_d, ssm_norm_w, attn_sinks, sc_conv_w, w_up, w_down):
    batch, seq, d = x.shape
    assert batch == 1 and d == D_MODEL
    depth = ada_w.shape[0]
    xs = x.reshape(seq, d)

    adaln = (c.reshape(d, 1), ada_w, ada_b.reshape(depth, 1, -1))
    mod, cos, sin = _modulation_and_rope(*adaln, 0, positions.reshape(seq, 1))

    rows = lambda v: v.reshape(depth, 1, -1)
    w_in_t = jnp.swapaxes(w_in, 1, 2)
    w_in_t16 = w_in_t[0].astype(BF16)
    gdn_lanes = _lane_params(LANE_ALPHA, gdn_a_log, gdn_dt_bias)
    ssm_lanes = _lane_params(LANE_DT, ssm_a_log, ssm_dt_bias, ssm_d)
    sink_rows = jnp.broadcast_to(attn_sinks.astype(F32)[:, :, None], (depth, ATTN_Q_HEADS, LANES))

    for i in range(depth):
        proj, w_out_bf16 = _in_projection(xs, i, rows(norm_pre_mix), mod, w_in_t16, _gathered_group(w_in_t16), w_out)
        y_a = _gated_deltanet(proj, i, gdn_conv_w, gdn_lanes, rows(gdn_norm_w))
        y_c, y_d, y_b = _ssd_attention_short_conv(proj, i, cos, sin, sink_rows, sc_conv_w,
                                                  ssm_conv_w, rows(ssm_conv_b), ssm_lanes, rows(ssm_norm_w))
        xs = _out_projection((y_a, y_b, y_c, y_d), i, w_out_bf16, xs, mod, rows(norm_post_mix))
        if i + 1 < depth:
            xs, mod, w_in_t16 = _mlp(xs, i, rows(norm_pre_mlp), mod, w_up, w_down, rows(norm_post_mlp),
                                     next_layer=adaln + (w_in_t,))
        else:
            xs = _mlp(xs, i, rows(norm_pre_mlp), mod, w_up, w_down, rows(norm_post_mlp))
    return xs.reshape(batch, seq, d)
```

```python
import functools

import jax
import jax.numpy as jnp
from jax import lax
from jax.experimental import pallas as pl
from jax.experimental.pallas import tpu as pltpu

F32 = jnp.float32
BF16 = jnp.bfloat16

D_MODEL = 2048
GW = 512
GDN_HEADS, GDN_HEAD_DIM, GDN_CHUNK = 4, 128, 64
SSM_HEADS, SSM_HEAD_DIM, SSM_STATE, SSM_CHUNK = 8, 64, 128, 128
ATTN_HEAD_DIM, ATTN_Q_HEADS, WINDOW = 64, 8, 128
ROPE_THETA = 10000.0
MLP_HIDDEN = 4 * D_MODEL
EPS = 1e-6

SUBLANES = 8
LANES = 128

_IN_SEGMENTS = (("gq", GW), ("gk", GW), ("gv", GW), ("gz", GW), ("gb", GDN_HEADS), ("ga", GDN_HEADS),
                ("sz", GW), ("sx", GW), ("sbc", 4 * SSM_STATE), ("sdt", SSM_HEADS),
                ("aq", GW), ("ak", 2 * ATTN_HEAD_DIM), ("av", 2 * ATTN_HEAD_DIM),
                ("cb", GW), ("cc", GW), ("ch", GW))
IN_OFFSET = {}
_off = 0
for _name, _size in _IN_SEGMENTS:
    IN_OFFSET[_name] = (_off, _off + _size)
    _off += _size
IN_WIDTH = _off

IN_GROUP = 3 * GW
CONTIGUOUS_GROUPS = (("gq", "gk", "gv"), ("sz", "sx", "sbc"), ("cb", "cc", "ch"))
GATHERED_GROUP = ("gz", "aq", "ak", "av", "gb", "ga", "sdt")
IN_COLS = (len(CONTIGUOUS_GROUPS) + 1) * IN_GROUP
for _grp in CONTIGUOUS_GROUPS:
    assert IN_OFFSET[_grp[0]][0] % SUBLANES == 0
    assert all(IN_OFFSET[a][1] == IN_OFFSET[b][0] for a, b in zip(_grp, _grp[1:]))
(SLAB_GQ, SLAB_GK, SLAB_GV, SLAB_SZ, SLAB_SX, SLAB_SBC, SLAB_CB, SLAB_CC, SLAB_CH, SLAB_GZ, SLAB_AQ) = range(11)
SLAB128_AK, SLAB128_AV, SLAB128_SMALL = 44, 45, 46
LANE_BETA, LANE_ALPHA, LANE_DT = 0, 4, 8

MIB = 1024 * 1024
V7X_VMEM_BYTES = 64 * MIB
VMEM_LIMIT = 56 * MIB


def _cparams(*sem, vmem_limit=VMEM_LIMIT):
    assert vmem_limit < V7X_VMEM_BYTES
    return pltpu.CompilerParams(dimension_semantics=sem, vmem_limit_bytes=vmem_limit)


def _sigmoid(x):
    return jax.nn.sigmoid(x)


def _silu(x):
    return x * jax.nn.sigmoid(x)


def _mm(a, b):
    return jnp.dot(a.astype(BF16), b.astype(BF16), preferred_element_type=F32)


def _mm_nt(a, b):
    return lax.dot_general(a.astype(BF16), b.astype(BF16), (((1,), (1,)), ((), ())),
                           preferred_element_type=F32)


def _mm_tn(a, b):
    return lax.dot_general(a.astype(BF16), b.astype(BF16), (((0,), (0,)), ((), ())),
                           preferred_element_type=F32)


def _blk(idx, size):
    assert size & (size - 1) == 0
    return lax.shift_right_logical(idx, size.bit_length() - 1)


def _split3(x):
    x1 = x.astype(BF16)
    r1 = x - x1.astype(F32)
    x2 = r1.astype(BF16)
    r2 = r1 - x2.astype(F32)
    return x1, x2, r2.astype(BF16)


def _cumsum_matrix(rows, chunk):
    r = lax.broadcasted_iota(jnp.int32, (2 * rows, rows), 0)
    s = lax.broadcasted_iota(jnp.int32, (2 * rows, rows), 1)
    rr = jnp.where(r >= rows, r - rows, r)
    same = _blk(rr, chunk) == _blk(s, chunk)
    sel = same & ((r >= rows) | (s <= rr))
    return jnp.where(sel, 1.0, 0.0).astype(BF16)


def _chunk_cumsum(g, mat):
    rows = g.shape[0]
    acc = None
    for part in _split3(g):
        t = jnp.dot(mat, part, preferred_element_type=F32)
        acc = t if acc is None else acc + t
    return acc[:rows], acc[rows:]


def _causal_conv(x, prev, w, taps):
    rows = x.shape[0]
    xp = jnp.concatenate([prev, x], axis=0)
    acc = x * w[taps - 1:taps]
    for d in range(1, taps):
        acc = acc + xp[SUBLANES - d:SUBLANES - d + rows] * w[taps - 1 - d:taps - d]
    return acc


def _causal_conv_staged(x_ref, prev, w, taps, stage_ref):
    rows = x_ref.shape[0]
    x = x_ref[...]
    stage_ref[0:SUBLANES, :] = prev
    stage_ref[SUBLANES:, :] = x
    acc = x * w[taps - 1:taps]
    for d in range(1, taps):
        acc = acc + stage_ref[pl.ds(SUBLANES - d, rows), :] * w[taps - 1 - d:taps - d]
    return acc


MOD_TN = 1536
MOD_ROWS = 64


def _mod_rope_kernel(c_ref, w_ref, b_ref, pos_ref, invf_ref, sign_ref, o_ref, cos_ref, sin_ref):
    d = c_ref.shape[0]

    def body(r, acc):
        rows = pl.ds(pl.multiple_of(r * MOD_ROWS, MOD_ROWS), MOD_ROWS)
        cc = _silu(c_ref[rows, :])
        prod = w_ref[0, rows, :] * cc
        return acc + jnp.sum(prod.reshape(MOD_ROWS // SUBLANES, SUBLANES, MOD_TN), axis=0)

    acc = lax.fori_loop(0, d // MOD_ROWS, body, jnp.zeros((SUBLANES, MOD_TN), F32))
    o_ref[0] = jnp.sum(acc, axis=0, keepdims=True) + b_ref[0]

    ang = pos_ref[...].astype(F32) * invf_ref[...]
    cos_ref[...] = jnp.cos(ang)
    sin_ref[...] = jnp.sin(ang) * sign_ref[...]


def _modulation_and_rope(c_col, ada_w, ada_b, layer, pos_col):
    depth, d, n = ada_w.shape
    seq = pos_col.shape[0]
    steps = n // MOD_TN
    tb = seq // steps
    assert steps * tb == seq and tb % SUBLANES == 0
    half = ATTN_HEAD_DIM // 2
    inv_freq = ROPE_THETA ** (-jnp.arange(0, ATTN_HEAD_DIM, 2, dtype=F32) / ATTN_HEAD_DIM)
    invf = jnp.tile(inv_freq, LANES // half).reshape(1, LANES)
    lane = jnp.arange(LANES)
    sign = jnp.where((lane % ATTN_HEAD_DIM) < half, -1.0, 1.0).astype(F32).reshape(1, LANES)
    table = jax.ShapeDtypeStruct((seq, LANES), F32)
    return pl.pallas_call(
        _mod_rope_kernel,
        grid=(steps,),
        in_specs=[
            pl.BlockSpec((d, 1), lambda j: (0, 0)),
            pl.BlockSpec((1, d, MOD_TN), lambda j: (layer, 0, j)),
            pl.BlockSpec((1, 1, MOD_TN), lambda j: (layer, 0, j)),
            pl.BlockSpec((tb, 1), lambda j: (j, 0)),
            pl.BlockSpec((1, LANES), lambda j: (0, 0)),
            pl.BlockSpec((1, LANES), lambda j: (0, 0)),
        ],
        out_specs=[pl.BlockSpec((1, 1, MOD_TN), lambda j: (0, 0, j)),
                   pl.BlockSpec((tb, LANES), lambda j: (j, 0)),
                   pl.BlockSpec((tb, LANES), lambda j: (j, 0))],
        out_shape=[jax.ShapeDtypeStruct((1, 1, n), F32), table, table],
        compiler_params=_cparams("parallel"),
        name="adaln_mod_rope",
    )(c_col, ada_w, ada_b, pos_col, invf, sign)


def _mod_spec(k):
    return pl.BlockSpec((None, 1, D_MODEL), lambda *grid_idx: (0, 0, k))


NORM_ROWS = 256


def _prenorm_to(x_ref, nw_ref, sc_ref, sh_ref, h_ref):
    tm = x_ref.shape[0]
    gain = nw_ref[...] * (1.0 + sc_ref[...])
    shift = sh_ref[...]

    def body(r, carry):
        rows = pl.ds(pl.multiple_of(r * NORM_ROWS, NORM_ROWS), NORM_ROWS)
        x = x_ref[rows, :]
        inv = lax.rsqrt(jnp.mean(x * x, axis=-1, keepdims=True) + EPS)
        h_ref[rows, :] = (x * inv * gain + shift).astype(BF16)
        return carry

    lax.fori_loop(0, tm // NORM_ROWS, body, 0)


INPROJ_TM = 1024

MOD_SHIFT_A, MOD_SCALE_A, MOD_GATE_A, MOD_SHIFT_M, MOD_SCALE_M, MOD_GATE_M = range(6)


def _layer_spec(shape, layer, *rest):
    rest = rest or (0,) * len(shape)

    def index_map(*grid_idx):
        return (layer,) + tuple(r(*grid_idx) if callable(r) else r for r in rest)

    return pl.BlockSpec((None,) + tuple(shape), index_map)


def _inproj_kernel(x_ref, nw_ref, sc_ref, sh_ref, wt_ref, tail_ref, wo_ref, o_ref, wo16_ref, h_ref):
    j = pl.program_id(1)
    wo16_ref[...] = wo_ref[...].astype(BF16)

    @pl.when(j == 0)
    def _():
        _prenorm_to(x_ref, nw_ref, sc_ref, sh_ref, h_ref)

    def project(w_ref):
        o_ref[...] = _mm_nt(h_ref[...], w_ref[...])

    @pl.when(j < len(CONTIGUOUS_GROUPS))
    def _():
        project(wt_ref)

    @pl.when(j == len(CONTIGUOUS_GROUPS))
    def _():
        project(tail_ref)


def _group_row(j):
    starts = [IN_OFFSET[grp[0]][0] // SUBLANES for grp in CONTIGUOUS_GROUPS]
    row8 = starts[-1]
    for k in range(len(starts) - 2, -1, -1):
        row8 = jnp.where(j <= k, starts[k], row8)
    return row8 * SUBLANES


def _in_projection(x, layer, norm_w, mod, w_t, w_gathered, w_out):
    seq, d = x.shape
    tm = min(INPROJ_TM, seq)
    n_groups = IN_COLS // IN_GROUP
    wo_rows = w_out.shape[1] // ((seq // tm) * n_groups)
    assert wo_rows * (seq // tm) * n_groups == w_out.shape[1] and wo_rows % (2 * SUBLANES) == 0
    return pl.pallas_call(
        _inproj_kernel,
        grid=(seq // tm, n_groups),
        in_specs=[
            pl.BlockSpec((tm, d), lambda i, j: (i, 0)),
            _layer_spec((1, d), layer),
            _mod_spec(MOD_SCALE_A),
            _mod_spec(MOD_SHIFT_A),
            pl.BlockSpec((pl.Element(IN_GROUP), pl.Element(d)), lambda i, j: (_group_row(j), 0)),
            pl.BlockSpec((IN_GROUP, d), lambda i, j: (0, 0), pipeline_mode=pl.Buffered(1)),
            pl.BlockSpec((None, wo_rows, d), lambda i, j: (layer, i * n_groups + j, 0)),
        ],
        out_specs=[pl.BlockSpec((tm, IN_GROUP), lambda i, j: (i, j)),
                   pl.BlockSpec((wo_rows, d), lambda i, j: (i * n_groups + j, 0))],
        out_shape=[jax.ShapeDtypeStruct((seq, IN_COLS), F32), jax.ShapeDtypeStruct(w_out.shape[1:], BF16)],
        scratch_shapes=[pltpu.VMEM((tm, d), BF16)],
        compiler_params=_cparams("arbitrary", "arbitrary"),
        name="in_proj",
    )(x, norm_w, mod, mod, w_t, w_gathered, w_out)


GDN_TB = 512


def _l2norm(x):
    return x * lax.rsqrt(jnp.sum(x * x, axis=-1, keepdims=True) + EPS)


GDN_BLK = 128
GDN_PREP_STAGES_PER_CHUNK = 2


def _unit_lower_inverse_minus_identity(ms, r, s):
    same8 = _blk(r, 8) == _blk(s, 8)
    same16 = _blk(r, 16) == _blk(s, 16)
    same32 = _blk(r, 32) == _blk(s, 32)
    n1 = [jnp.where(same8, -m, 0.0) for m in ms]
    n2 = [_mm(a, a) for a in n1]
    yield
    n3 = [_mm(a, b) for a, b in zip(n1, n2)]
    n4 = [_mm(b, b) for b in n2]
    yield
    ys = [a + b + c for a, b, c in zip(n1, n2, n3)]
    ts = [_mm(y, d) for y, d in zip(ys, n4)]
    yield
    ys = [y + d + t for y, d, t in zip(ys, n4, ts)]
    for sel in (same16 & ~same8, same32 & ~same16, ~same32):
        blks = [jnp.where(sel, m, 0.0) for m in ms]
        cs = [b + _mm(y, b) for y, b in zip(ys, blks)]
        yield
        ts = [_mm(c, y) for c, y in zip(cs, ys)]
        yield
        ys = [y - c - t for y, c, t in zip(ys, cs, ts)]
    return ys


def _run_interleaved(main, side, main_stages_per_side_stage):
    side_live = True
    done = 0
    while True:
        try:
            next(main)
        except StopIteration as stop:
            result = stop.value
            break
        done += 1
        if side_live and done % main_stages_per_side_stage == 0:
            side_live = next(side, StopIteration) is not StopIteration
    while side_live:
        side_live = next(side, StopIteration) is not StopIteration
    return result


def _gdn_prepare(q_ref, k_ref, v_ref, qp_ref, kp_ref, vp_ref, sm_ref, cw_ref, lp_ref, cum_ref, conv_ref, first_block):
    tb = q_ref.shape[0]
    hd = GDN_HEAD_DIM
    nblk = tb // GDN_BLK

    keep_prev = jnp.where(first_block, 0.0, 1.0)
    cw = cw_ref[...]

    def conv_silu(x_ref, p_ref, idx):
        w = cw[:, idx * GW:(idx + 1) * GW]
        return _silu(_causal_conv_staged(x_ref, p_ref[...] * keep_prev, w, 4, conv_ref.at[idx]))

    q = conv_silu(q_ref, qp_ref, 0)
    yield
    k = conv_silu(k_ref, kp_ref, 1)
    yield
    v = conv_silu(v_ref, vp_ref, 2)
    yield

    small = sm_ref[...]
    a_log = lp_ref[0:1, :]
    dt_bias = lp_ref[1:2, :]
    beta_all = _sigmoid(small)
    g_all = -jnp.exp(a_log) * jax.nn.softplus(small + dt_bias)
    gc_all, gtot_all = _chunk_cumsum(g_all, cum_ref[...])
    gc_t = gc_all.T
    yield

    r = lax.broadcasted_iota(jnp.int32, (GDN_BLK, GDN_BLK), 0)
    s = lax.broadcasted_iota(jnp.int32, (GDN_BLK, GDN_BLK), 1)
    same_chunk = _blk(r, GDN_CHUNK) == _blk(s, GDN_CHUNK)
    mask_incl = same_chunk & (s <= r)
    mask_strict = same_chunk & (s < r)

    problems = [(slice(b * GDN_BLK, (b + 1) * GDN_BLK), h) for b in range(nblk) for h in range(GDN_HEADS)]
    head_cols = lambda h: slice(h * hd, (h + 1) * hd)
    gate = lambda arr, rows, lane: arr[rows, lane:lane + 1]

    qn = [_l2norm(q[rows, head_cols(h)]) * (hd ** -0.5) for rows, h in problems]
    kn = [_l2norm(k[rows, head_cols(h)]) for rows, h in problems]
    beta = [gate(beta_all, rows, LANE_BETA + h) for rows, h in problems]
    gcol = [gate(gc_all, rows, LANE_ALPHA + h) for rows, h in problems]
    gtot = [gate(gtot_all, rows, LANE_ALPHA + h) for rows, h in problems]
    grow = [gc_t[LANE_ALPHA + h:LANE_ALPHA + h + 1, rows] for rows, h in problems]
    yield

    decay = [jnp.exp(jnp.where(mask_incl, gc - gr, -jnp.inf)) for gc, gr in zip(gcol, grow)]
    k_beta = [kk * b for kk, b in zip(kn, beta)]
    k16 = [kk.astype(BF16) for kk in kn]
    yield
    ms = [jnp.where(mask_strict, _mm_nt(kb, kk) * d, 0.0) for kb, kk, d in zip(k_beta, k16, decay)]
    yield
    attn = [(_mm_nt(qq, kk) * d).astype(BF16) for qq, kk, d in zip(qn, k16, decay)]
    ys = yield from _unit_lower_inverse_minus_identity(ms, r, s)
    yield

    e_gc = [jnp.exp(gc) for gc in gcol]
    rhs = [jnp.concatenate([v[rows, head_cols(h)] * b, kb * e], axis=1)
           for (rows, h), b, kb, e in zip(problems, beta, k_beta, e_gc)]
    uw = [x + _mm(y, x) for y, x in zip(ys, rhs)]
    yield
    u = [x[:, :hd] for x in uw]
    w = [x[:, hd:].astype(BF16) for x in uw]
    q_dec = [(qq * e).astype(BF16) for qq, e in zip(qn, e_gc)]
    k_dec = [(kk * jnp.exp(gt - gc)).astype(BF16) for kk, gt, gc in zip(kn, gtot, gcol)]
    return u, w, q_dec, k_dec, attn, gtot_all


def _gdn_recurrence(u, w, q_dec, k_dec, attn, gtot_all, state_ref, z, nw_ref, o_ref):
    hd = GDN_HEAD_DIM
    nblk = o_ref.shape[0] // GDN_BLK
    chunks_per_blk = GDN_BLK // GDN_CHUNK
    head_cols = lambda h: slice(h * hd, (h + 1) * hd)

    states = [state_ref[h] for h in range(GDN_HEADS)]
    v_new = [[] for _ in range(nblk * GDN_HEADS)]
    o_inter = [[] for _ in range(nblk * GDN_HEADS)]
    for b in range(nblk):
        for c in range(chunks_per_blk):
            rows = slice(c * GDN_CHUNK, (c + 1) * GDN_CHUNK)
            first = b * GDN_BLK + c * GDN_CHUNK
            for h in range(GDN_HEADS):
                p = b * GDN_HEADS + h
                st16 = states[h].astype(BF16)
                vn = u(p, rows) - jnp.dot(w(p, rows), st16, preferred_element_type=F32)
                o_inter[p].append(jnp.dot(q_dec(p, rows), st16, preferred_element_type=F32))
                chunk_dec = jnp.exp(gtot_all[first:first + 1, LANE_ALPHA + h:LANE_ALPHA + h + 1])
                states[h] = states[h] * chunk_dec + _mm_tn(k_dec(p, rows), vn)
                v_new[p].append(vn)
            yield
        rows = slice(b * GDN_BLK, (b + 1) * GDN_BLK)
        for h in range(GDN_HEADS):
            p = b * GDN_HEADS + h
            o = jnp.concatenate(o_inter[p], axis=0) + _mm(attn(p, slice(None)), jnp.concatenate(v_new[p], axis=0))
            o = o * lax.rsqrt(jnp.mean(o * o, axis=-1, keepdims=True) + EPS) * nw_ref[...]
            o_ref[rows, head_cols(h)] = (o * _silu(z[rows, head_cols(h)])).astype(o_ref.dtype)
    for h in range(GDN_HEADS):
        state_ref[h] = states[h]


def _gdn_kernel(q_ref, k_ref, v_ref, z_ref, qp_ref, kp_ref, vp_ref, sm_ref, cw_ref, lp_ref, nw_ref, cum_ref,
                o_ref, state_ref, conv_ref, u_ref, w_ref, qd_ref, kd_ref, at_ref, gt_ref):
    step = pl.program_id(0)
    write_slot = step % 2
    read_slot = 1 - write_slot
    n_problems = u_ref.shape[1]

    @pl.when(step == 0)
    def _():
        state_ref[...] = jnp.zeros_like(state_ref)
        for ref in (u_ref, w_ref, qd_ref, kd_ref, at_ref, gt_ref):
            ref[1] = jnp.zeros(ref.shape[1:], ref.dtype)

    recurrence = _gdn_recurrence(*[(lambda p, rows, ref=ref: ref[read_slot, p, rows])
                                   for ref in (u_ref, w_ref, qd_ref, kd_ref, at_ref)],
                                 gt_ref[read_slot], state_ref, z_ref[...], nw_ref, o_ref)
    prepare = _gdn_prepare(q_ref, k_ref, v_ref, qp_ref, kp_ref, vp_ref, sm_ref, cw_ref, lp_ref, cum_ref, conv_ref,
                           step == 0)
    u, w, q_dec, k_dec, attn, gtot_all = _run_interleaved(prepare, recurrence, GDN_PREP_STAGES_PER_CHUNK)
    for p in range(n_problems):
        u_ref[write_slot, p] = u[p]
        w_ref[write_slot, p] = w[p]
        qd_ref[write_slot, p] = q_dec[p]
        kd_ref[write_slot, p] = k_dec[p]
        at_ref[write_slot, p] = attn[p]
    gt_ref[write_slot] = gtot_all


def _prev_rows_spec(tb, width, slab):
    blocks = tb // SUBLANES
    return pl.BlockSpec((SUBLANES, width), lambda i: (jnp.maximum(i * blocks - 1, 0), slab))


def _lane_params(lane0, *rows):
    stacked = jnp.stack([r.astype(F32) for r in rows], axis=1)
    nrows, heads = stacked.shape[1:]
    return jnp.pad(stacked, ((0, 0), (0, SUBLANES - nrows), (lane0, LANES - lane0 - heads)))


def _gated_deltanet(proj, layer, conv_w, lane_params, norm_w):
    seq = proj.shape[0]
    tb = min(GDN_TB, seq)
    nsteps = seq // tb
    n_problems = (tb // GDN_BLK) * GDN_HEADS
    prep = lambda t: jnp.minimum(t, nsteps - 1)
    done = lambda t: jnp.maximum(t - 1, 0)
    slab = lambda j: pl.BlockSpec((tb, GW), lambda t: (prep(t), j))
    prev = lambda j: pl.BlockSpec((SUBLANES, GW),
                                  lambda t: (jnp.maximum(prep(t) * (tb // SUBLANES) - 1, 0), j))
    full = lambda shape: _layer_spec(shape, layer)
    factor = lambda dtype: pltpu.VMEM((2, n_problems, GDN_BLK, GDN_HEAD_DIM), dtype)
    return pl.pallas_call(
        _gdn_kernel,
        grid=(nsteps + 1,),
        in_specs=[
            slab(SLAB_GQ), slab(SLAB_GK), slab(SLAB_GV),
            pl.BlockSpec((tb, GW), lambda t: (done(t), SLAB_GZ)),
            prev(SLAB_GQ), prev(SLAB_GK), prev(SLAB_GV),
            pl.BlockSpec((tb, LANES), lambda t: (prep(t), SLAB128_SMALL)),
            full((4, 3 * GW)), full((SUBLANES, LANES)), full((1, GDN_HEAD_DIM)),
            pl.BlockSpec((2 * tb, tb), lambda t: (0, 0)),
        ],
        out_specs=pl.BlockSpec((tb, GW), lambda t: (done(t), 0)),
        out_shape=jax.ShapeDtypeStruct((seq, GW), BF16),
        scratch_shapes=[pltpu.VMEM((GDN_HEADS, GDN_HEAD_DIM, GDN_HEAD_DIM), F32),
                        pltpu.VMEM((3, tb + SUBLANES, GW), F32),
                        factor(F32), factor(BF16), factor(BF16), factor(BF16), factor(BF16),
                        pltpu.VMEM((2, tb, LANES), F32)],
        compiler_params=_cparams("arbitrary"),
        name="gated_deltanet",
    )(proj, proj, proj, proj, proj, proj, proj, proj, conv_w, lane_params, norm_w, _cumsum_matrix(tb, GDN_CHUNK))


SSD_SUB = 256


def _ssd_block(first_block, z_ref, x_ref, bc_ref, xp_ref, bcp_ref, sm_ref, cw_ref, cb_ref, lp_ref, nw_ref, cum_ref,
               o_ref, state_ref):
    keep_prev = jnp.where(first_block, 0.0, 1.0)
    for sub in range(z_ref.shape[0] // SSD_SUB):
        r0 = sub * SSD_SUB
        if sub == 0:
            x_prev, bc_prev = xp_ref[...] * keep_prev, bcp_ref[...] * keep_prev
        else:
            x_prev, bc_prev = x_ref[r0 - SUBLANES:r0, :], bc_ref[r0 - SUBLANES:r0, :]
        _ssd_sub_block(r0, x_prev, bc_prev, z_ref, x_ref, bc_ref, sm_ref, cw_ref, cb_ref, lp_ref, nw_ref, cum_ref,
                       o_ref, state_ref)


def _ssd_sub_block(r0, x_prev, bc_prev, z_ref, x_ref, bc_ref, sm_ref, cw_ref, cb_ref, lp_ref, nw_ref, cum_ref,
                   o_ref, state_ref):
    tb = SSD_SUB
    block = slice(r0, r0 + tb)
    nstate = SSM_STATE
    chunk = SSM_CHUNK
    nchunks = tb // chunk
    n_pairs = SSM_HEADS // 2
    pairs_per_group = n_pairs // 2

    cw = cw_ref[...]
    cbias = cb_ref[...]
    xs = _silu(_causal_conv(x_ref[block, :], x_prev, cw[:, :GW], 4) + cbias[:, :GW])
    bc = _silu(_causal_conv(bc_ref[block, :], bc_prev, cw[:, GW:], 4) + cbias[:, GW:])
    z = z_ref[block, :]

    small = sm_ref[block, :]
    a_neg = -jnp.exp(lp_ref[0:1, :])
    dt_all = jax.nn.softplus(small + lp_ref[1:2, :])
    dskip_all = lp_ref[2:3, :]
    gc_all, gtot_all = _chunk_cumsum(dt_all * a_neg, cum_ref[...])
    gc_t = gc_all.T

    r = lax.broadcasted_iota(jnp.int32, (chunk, chunk), 0)
    s = lax.broadcasted_iota(jnp.int32, (chunk, chunk), 1)
    mask_incl = s <= r
    lane = lax.broadcasted_iota(jnp.int32, (1, LANES), 1)
    first_half = lane < SSM_HEAD_DIM
    by_half = lambda a, b: jnp.where(first_half, a, b)

    problems = [(c, p) for c in range(nchunks) for p in range(n_pairs)]
    rows_of = lambda c: slice(c * chunk, (c + 1) * chunk)
    cols_of = lambda p: slice(p * LANES, (p + 1) * LANES)
    lanes_of = lambda p: (LANE_DT + 2 * p, LANE_DT + 2 * p + 1)
    col = lambda arr, c, l: arr[rows_of(c), l:l + 1]

    b_mat = {(c, g): bc[rows_of(c), g * nstate:(g + 1) * nstate].astype(BF16)
             for c in range(nchunks) for g in range(2)}
    c_mat = {(c, g): bc[rows_of(c), (2 + g) * nstate:(3 + g) * nstate].astype(BF16)
             for c in range(nchunks) for g in range(2)}
    cb = {key: _mm_nt(c_mat[key], b_mat[key]) for key in b_mat}

    x_pair = [xs[rows_of(c), cols_of(p)] for c, p in problems]
    xdt = [x * by_half(col(dt_all, c, lanes_of(p)[0]), col(dt_all, c, lanes_of(p)[1]))
           for x, (c, p) in zip(x_pair, problems)]
    y_diag = []
    for (c, p), xd in zip(problems, xdt):
        halves = (by_half(xd, 0.0), by_half(0.0, xd))
        acc = None
        for l, half in zip(lanes_of(p), halves):
            grow = gc_t[l:l + 1, rows_of(c)]
            lmat = jnp.exp(jnp.where(mask_incl, col(gc_all, c, l) - grow, -jnp.inf))
            t = _mm(cb[c, p // pairs_per_group] * lmat, half)
            acc = t if acc is None else acc + t
        y_diag.append(acc)

    e_in = [by_half(jnp.exp(col(gc_all, c, lanes_of(p)[0])), jnp.exp(col(gc_all, c, lanes_of(p)[1])))
            for c, p in problems]
    e_out = [by_half(jnp.exp(col(gtot_all, c, lanes_of(p)[0]) - col(gc_all, c, lanes_of(p)[0])),
                     jnp.exp(col(gtot_all, c, lanes_of(p)[1]) - col(gc_all, c, lanes_of(p)[1])))
             for c, p in problems]
    states = [state_ref[p] for p in range(n_pairs)]
    y_off = []
    for idx, (c, p) in enumerate(problems):
        la, lb = lanes_of(p)
        g = p // pairs_per_group
        y_off.append(e_in[idx] * _mm(c_mat[c, g], states[p]))
        first = c * chunk
        cd_lane = by_half(jnp.exp(gtot_all[first:first + 1, la:la + 1]),
                          jnp.exp(gtot_all[first:first + 1, lb:lb + 1]))
        states[p] = states[p] * cd_lane + _mm_tn(b_mat[c, g], xdt[idx] * e_out[idx])
    for p in range(n_pairs):
        state_ref[p] = states[p]

    ys = []
    for idx, (c, p) in enumerate(problems):
        la, lb = lanes_of(p)
        y = y_diag[idx] + y_off[idx] + x_pair[idx] * by_half(dskip_all[:, la:la + 1], dskip_all[:, lb:lb + 1])
        ys.append(y * _silu(z[rows_of(c), cols_of(p)]))
    sumsq = [jnp.sum(y * y, axis=-1, keepdims=True) for y in ys]
    for c in range(nchunks):
        for g in range(2):
            members = [c * n_pairs + g * pairs_per_group + k for k in range(pairs_per_group)]
            total = sumsq[members[0]]
            for idx in members[1:]:
                total = total + sumsq[idx]
            inv = lax.rsqrt(total / (pairs_per_group * LANES) + EPS)
            for idx in members:
                cols = cols_of(problems[idx][1])
                out_rows = slice(r0 + c * chunk, r0 + (c + 1) * chunk)
                o_ref[out_rows, cols] = (ys[idx] * inv * nw_ref[:, cols]).astype(o_ref.dtype)


def _rotate_half(x, lane_in_head_low):
    width = x.shape[1]
    half = ATTN_HEAD_DIM // 2
    ahead = pltpu.roll(x, width - half, axis=1)
    behind = pltpu.roll(x, half, axis=1)
    return jnp.where(lane_in_head_low, ahead, behind)


SWA_TB = 512


def _swa_kernel(q_ref, k_ref, v_ref, cos_ref, sin_ref, sink_ref, cb_ref, cc_ref, ch_ref, ccp_ref, chp_ref, cw_ref,
                sz_ref, sx_ref, sbc_ref, sxp_ref, sbcp_ref, sm_ref, scw_ref, scb_ref, slp_ref, snw_ref, cum_ref,
                o_ref, od_ref, ob_ref, kprev_ref, vprev_ref, ssm_state_ref):
    step = pl.program_id(0)
    tb = q_ref.shape[0]
    win = WINDOW
    hd = ATTN_HEAD_DIM

    @pl.when(step == 0)
    def _():
        kprev_ref[...] = jnp.zeros_like(kprev_ref)
        vprev_ref[...] = jnp.zeros_like(vprev_ref)
        ssm_state_ref[...] = jnp.zeros_like(ssm_state_ref)

    keep_prev = jnp.where(step == 0, 0.0, 1.0)
    c_h = cc_ref[...] * ch_ref[...]
    c_h_prev = ccp_ref[...] * chp_ref[...] * keep_prev
    od_ref[...] = (cb_ref[...] * _causal_conv(c_h, c_h_prev, cw_ref[...], 3)).astype(od_ref.dtype)

    cos = cos_ref[...]
    sin = sin_ref[...]
    lane = lax.broadcasted_iota(jnp.int32, (1, LANES), 1)
    low = (lane & (hd - 1)) < (hd // 2)
    first_half = lane < hd

    k_cur = k_ref[...]
    k_cur = k_cur * cos + _rotate_half(k_cur, low) * sin
    v_cur = v_ref[...]
    k_all = jnp.concatenate([kprev_ref[...], k_cur], axis=0)
    v_all = jnp.concatenate([vprev_ref[...], v_cur], axis=0)
    k_swap = pltpu.roll(k_all, hd, axis=1)
    v_swap = pltpu.roll(v_all, hd, axis=1)

    def placed(x, x_swap, kv_head, pos):
        src = x if kv_head == pos else x_swap
        keep = first_half if pos == 0 else ~first_half
        return jnp.where(keep, src, 0.0).astype(BF16)

    qi = lax.broadcasted_iota(jnp.int32, (win, 2 * win), 0)
    kj = lax.broadcasted_iota(jnp.int32, (win, 2 * win), 1)
    rel = qi + win - kj
    band = (rel >= 0) & (rel < win)
    band_first = band & ((kj >= win) | (step > 0))

    problems = [(j, h) for j in range(tb // win) for h in range(ATTN_Q_HEADS)]
    kv_of = lambda head: head // (ATTN_Q_HEADS // 2)
    pair_cols = lambda pair: slice(pair * LANES, (pair + 1) * LANES)
    k_at = {(g, pos): placed(k_all, k_swap, g, pos) for g in range(2) for pos in range(2)}
    v_at = {(g, pos): placed(v_all, v_swap, g, pos) for g in range(2) for pos in range(2)}
    q_pairs = []
    for pair in range(ATTN_Q_HEADS // 2):
        q_pair = q_ref[:, pair_cols(pair)]
        q_pairs.append(((q_pair * cos + _rotate_half(q_pair, low) * sin) * (hd ** -0.5)).astype(BF16))
    sinks = [sink_ref[h:h + 1, 0:1] for _, h in problems]
    sc = [jnp.where(band_first if j == 0 else band,
                    _mm_nt(q_pairs[h // 2][j * win:(j + 1) * win],
                           k_at[kv_of(h), h % 2][j * win:(j + 2) * win]),
                    -jnp.inf)
          for j, h in problems]
    mx = [jnp.maximum(jnp.max(s_, axis=-1, keepdims=True), sk) for s_, sk in zip(sc, sinks)]
    p = [jnp.exp(s_ - m_) for s_, m_ in zip(sc, mx)]
    denom = [jnp.sum(p_, axis=-1, keepdims=True) + jnp.exp(sk - m_) for p_, sk, m_ in zip(p, sinks, mx)]
    o = [_mm(p_, v_at[kv_of(h), h % 2][j * win:(j + 2) * win]) / d_
         for (j, h), p_, d_ in zip(problems, p, denom)]
    for idx in range(0, len(problems), 2):
        j, h = problems[idx]
        o_ref[j * win:(j + 1) * win, pair_cols(h // 2)] = (o[idx] + o[idx + 1]).astype(o_ref.dtype)

    kprev_ref[...] = k_cur[tb - win:]
    vprev_ref[...] = v_cur[tb - win:]

    _ssd_block(step == 0, sz_ref, sx_ref, sbc_ref, sxp_ref, sbcp_ref, sm_ref, scw_ref, scb_ref, slp_ref, snw_ref,
               cum_ref, ob_ref, ssm_state_ref)


def _ssd_attention_short_conv(proj, layer, cos, sin, sink_rows, sc_conv_w, ssm_conv_w, ssm_conv_b, ssm_lanes,
                              ssm_norm_w):
    seq = proj.shape[0]
    tb = min(SWA_TB, seq)
    assert tb % SSD_SUB == 0
    slab = lambda j: pl.BlockSpec((tb, GW), lambda i: (i, j))
    full = lambda shape: _layer_spec(shape, layer)
    out = jax.ShapeDtypeStruct((seq, GW), BF16)
    return pl.pallas_call(
        _swa_kernel,
        grid=(seq // tb,),
        in_specs=[
            slab(SLAB_AQ),
            pl.BlockSpec((tb, LANES), lambda i: (i, SLAB128_AK)),
            pl.BlockSpec((tb, LANES), lambda i: (i, SLAB128_AV)),
            pl.BlockSpec((tb, LANES), lambda i: (i, 0)),
            pl.BlockSpec((tb, LANES), lambda i: (i, 0)),
            _layer_spec((ATTN_Q_HEADS, LANES), layer),
            slab(SLAB_CB), slab(SLAB_CC), slab(SLAB_CH),
            _prev_rows_spec(tb, GW, SLAB_CC), _prev_rows_spec(tb, GW, SLAB_CH),
            _layer_spec((3, GW), layer),
            slab(SLAB_SZ), slab(SLAB_SX), slab(SLAB_SBC),
            _prev_rows_spec(tb, GW, SLAB_SX), _prev_rows_spec(tb, GW, SLAB_SBC),
            pl.BlockSpec((tb, LANES), lambda i: (i, SLAB128_SMALL)),
            full((4, 2 * GW)), full((1, 2 * GW)), full((SUBLANES, LANES)), full((1, GW)),
            pl.BlockSpec((2 * SSD_SUB, SSD_SUB), lambda i: (0, 0)),
        ],
        out_specs=[pl.BlockSpec((tb, GW), lambda i: (i, 0))] * 3,
        out_shape=[out, out, out],
        scratch_shapes=[pltpu.VMEM((WINDOW, LANES), F32), pltpu.VMEM((WINDOW, LANES), F32),
                        pltpu.VMEM((SSM_HEADS // 2, SSM_STATE, LANES), F32)],
        compiler_params=_cparams("arbitrary"),
        name="ssd_swa_sconv",
    )(proj, proj, proj, cos, sin, sink_rows, proj, proj, proj, proj, proj, sc_conv_w,
      proj, proj, proj, proj, proj, proj, ssm_conv_w, ssm_conv_b, ssm_lanes, ssm_norm_w,
      _cumsum_matrix(SSD_SUB, SSM_CHUNK))


OUTPROJ_TM = 512


def _outproj_kernel(ya_ref, yb_ref, yc_ref, yd_ref, w_ref, x_ref, gate_ref, nw_ref, o_ref):
    y = jnp.concatenate([ya_ref[...], yb_ref[...], yc_ref[...], yd_ref[...]], axis=1)
    y = jnp.dot(y, w_ref[...], preferred_element_type=F32)
    inv = lax.rsqrt(jnp.mean(y * y, axis=-1, keepdims=True) + EPS)
    o_ref[...] = x_ref[...] + y * inv * (gate_ref[...] * nw_ref[...])


def _out_projection(ys, layer, w_bf16, x, mod, norm_w):
    seq, d = x.shape
    tm = min(OUTPROJ_TM, seq)
    return pl.pallas_call(
        _outproj_kernel,
        grid=(seq // tm,),
        in_specs=[pl.BlockSpec((tm, GW), lambda i: (i, 0))] * 4 + [
            pl.BlockSpec((4 * GW, d), lambda i: (0, 0)),
            pl.BlockSpec((tm, d), lambda i: (i, 0)),
            _mod_spec(MOD_GATE_A),
            _layer_spec((1, d), layer),
        ],
        out_specs=pl.BlockSpec((tm, d), lambda i: (i, 0)),
        out_shape=jax.ShapeDtypeStruct((seq, d), F32),
        compiler_params=_cparams("parallel"),
        name="out_proj",
    )(*ys, w_bf16, x, mod, norm_w)


MLP_TM = 1024
MLP_TH = 512
NEXT_WIN_ROWS = 48


def _mlp_kernel(*refs, next_layer):
    if next_layer:
        (x_ref, nw_ref, sc_ref, sh_ref, wu_ref, wd_ref, gate_ref, pw_ref, c_ref, aw_ref, ab_ref, wt_ref,
         o_ref, mod_ref, wt16_ref, h_ref) = refs
    else:
        x_ref, nw_ref, sc_ref, sh_ref, wu_ref, wd_ref, gate_ref, pw_ref, o_ref, h_ref = refs
    j = pl.program_id(1)

    @pl.when(j == 0)
    def _():
        _prenorm_to(x_ref, nw_ref, sc_ref, sh_ref, h_ref)
        o_ref[...] = jnp.zeros_like(o_ref)

    hid = _mm(h_ref[...], wu_ref[...])
    hid = jnp.square(jnp.maximum(hid, 0.0))
    o_ref[...] += _mm(hid, wd_ref[...])

    if next_layer:
        prod = aw_ref[...] * _silu(c_ref[...])
        part = jnp.sum(prod.reshape(prod.shape[0] // SUBLANES, SUBLANES, LANES), axis=0)
        mod_ref[...] = jnp.sum(part, axis=0, keepdims=True) + ab_ref[...]
        wt16_ref[...] = wt_ref[...].astype(BF16)

    @pl.when(j == pl.num_programs(1) - 1)
    def _():
        gain = gate_ref[...] * pw_ref[...]

        def body(r, carry):
            rows = pl.ds(pl.multiple_of(r * NORM_ROWS, NORM_ROWS), NORM_ROWS)
            y = o_ref[rows, :]
            inv = lax.rsqrt(jnp.mean(y * y, axis=-1, keepdims=True) + EPS)
            o_ref[rows, :] = x_ref[rows, :] + y * inv * gain
            return carry

        lax.fori_loop(0, x_ref.shape[0] // NORM_ROWS, body, 0)


def _mlp(x, layer, norm_w, mod, w_up, w_down, post_w, next_layer=None):
    seq, d = x.shape
    hidden = w_up.shape[2]
    tm = min(MLP_TM, seq)
    steps_per_tile = hidden // MLP_TH
    vmem = (2 * 2 * tm * d * 4 + tm * d * 2 + 2 * 2 * d * MLP_TH * w_up.dtype.itemsize
            + d * MLP_TH * 2 + tm * MLP_TH * 6)
    in_specs = [
        pl.BlockSpec((tm, d), lambda i, j: (i, 0)),
        _layer_spec((1, d), layer),
        _mod_spec(MOD_SCALE_M),
        _mod_spec(MOD_SHIFT_M),
        _layer_spec((d, MLP_TH), layer, 0, lambda i, j: j),
        _layer_spec((MLP_TH, d), layer, lambda i, j: j, 0),
        _mod_spec(MOD_GATE_M),
        _layer_spec((1, d), layer),
    ]
    out_specs = [pl.BlockSpec((tm, d), lambda i, j: (i, 0))]
    out_shape = [jax.ShapeDtypeStruct((seq, d), F32)]
    args = [x, norm_w, mod, mod, w_up, w_down, mod, post_w]
    if next_layer:
        c_col, ada_w, ada_b, w_in_t = next_layer
        n = ada_w.shape[2]
        n_steps = (seq // tm) * steps_per_tile
        wt_rows = NEXT_WIN_ROWS
        assert n_steps >= n // LANES and n_steps * wt_rows >= w_in_t.shape[1] and w_in_t.shape[1] % wt_rows == 0
        col = lambda i, j: jnp.minimum(i * steps_per_tile + j, n // LANES - 1)
        wt_blk = lambda i, j: jnp.minimum(i * steps_per_tile + j, w_in_t.shape[1] // wt_rows - 1)
        in_specs += [pl.BlockSpec((d, 1), lambda i, j: (0, 0)),
                     pl.BlockSpec((None, d, LANES), lambda i, j: (layer + 1, 0, col(i, j))),
                     pl.BlockSpec((None, 1, LANES), lambda i, j: (layer + 1, 0, col(i, j))),
                     pl.BlockSpec((None, wt_rows, d), lambda i, j: (layer + 1, wt_blk(i, j), 0))]
        out_specs += [pl.BlockSpec((None, 1, LANES), lambda i, j: (0, 0, col(i, j))),
                      pl.BlockSpec((wt_rows, d), lambda i, j: (wt_blk(i, j), 0))]
        out_shape += [jax.ShapeDtypeStruct((1, 1, n), F32), jax.ShapeDtypeStruct(w_in_t.shape[1:], BF16)]
        args += [c_col, ada_w, ada_b, w_in_t]
        vmem += 2 * 2 * d * LANES * 4 + 2 * wt_rows * d * 6
    outs = pl.pallas_call(
        functools.partial(_mlp_kernel, next_layer=bool(next_layer)),
        grid=(seq // tm, steps_per_tile),
        in_specs=in_specs,
        out_specs=out_specs,
        out_shape=out_shape,
        scratch_shapes=[pltpu.VMEM((tm, d), BF16)],
        compiler_params=_cparams("arbitrary", "arbitrary", vmem_limit=vmem),
        name="mlp",
    )(*args)
    return outs if next_layer else outs[0]


def _gathered_group(w_t):
    parts = [w_t[IN_OFFSET[name][0]:IN_OFFSET[name][1]] for name in GATHERED_GROUP]
    used = sum(p.shape[0] for p in parts)
    parts.append(jnp.zeros((IN_GROUP - used, w_t.shape[1]), w_t.dtype))
    return jnp.concatenate(parts, axis=0)


def kernel(x, c, positions, ada_w, ada_b, norm_pre_mix, norm_post_mix, norm_pre_mlp, norm_post_mlp,
           w_in, w_out, gdn_conv_w, gdn_a_log, gdn_dt_bias, gdn_norm_w, ssm_conv_w, ssm_conv_b,
           ssm_a_log, ssm_dt_bias, ssm_d, ssm_norm_w, attn_sinks, sc_conv_w, w_up, w_down):
    batch, seq, d = x.shape
    assert batch == 1 and d == D_MODEL
    depth = ada_w.shape[0]
    xs = x.reshape(seq, d)

    adaln = (c.reshape(d, 1), ada_w, ada_b.reshape(depth, 1, -1))
    mod, cos, sin = _modulation_and_rope(*adaln, 0, positions.reshape(seq, 1))

    rows = lambda v: v.reshape(depth, 1, -1)
    w_in_t = jnp.swapaxes(w_in, 1, 2)
    w_in_t16 = w_in_t[0].astype(BF16)
    gdn_lanes = _lane_params(LANE_ALPHA, gdn_a_log, gdn_dt_bias)
    ssm_lanes = _lane_params(LANE_DT, ssm_a_log, ssm_dt_bias, ssm_d)
    sink_rows = jnp.broadcast_to(attn_sinks.astype(F32)[:, :, None], (depth, ATTN_Q_HEADS, LANES))

    for i in range(depth):
        proj, w_out_bf16 = _in_projection(xs, i, rows(norm_pre_mix), mod, w_in_t16, _gathered_group(w_in_t16), w_out)
        y_a = _gated_deltanet(proj, i, gdn_conv_w, gdn_lanes, rows(gdn_norm_w))
        y_c, y_d, y_b = _ssd_attention_short_conv(proj, i, cos, sin, sink_rows, sc_conv_w,
                                                  ssm_conv_w, rows(ssm_conv_b), ssm_lanes, rows(ssm_norm_w))
        xs = _out_projection((y_a, y_b, y_c, y_d), i, w_out_bf16, xs, mod, rows(norm_post_mix))
        if i + 1 < depth:
            xs, mod, w_in_t16 = _mlp(xs, i, rows(norm_pre_mlp), mod, w_up, w_down, rows(norm_post_mlp),
                                     next_layer=adaln + (w_in_t,))
        else:
            xs = _mlp(xs, i, rows(norm_pre_mlp), mod, w_up, w_down, rows(norm_post_mlp))
    return xs.reshape(batch, seq, d)
```
